```python
import numpy as np
import jax
import jax.numpy as jnp
from jax import lax

D_MODEL = 1024
BATCH = 2
SEQ = 8192
DEPTH = 2

CHUNK = 64
N_MIXERS = 4
D_MIX = D_MODEL
GROUP_WIDTH = D_MIX // N_MIXERS
N_HEADS = 4
HEAD_DIM = GROUP_WIDTH // N_HEADS
SSM_STATE = 128
SSM_GROUPS = 2
CONV_K = 4
D_FF = 2816
EPS = 1e-6
NEG_BIG = -1e30
F32 = jnp.float32

MLSTM_SIZES = (GROUP_WIDTH, GROUP_WIDTH, GROUP_WIDTH, GROUP_WIDTH, N_HEADS, N_HEADS)
MAMBA_XBC = GROUP_WIDTH + 2 * SSM_GROUPS * SSM_STATE
MAMBA_SIZES = (GROUP_WIDTH, MAMBA_XBC, N_HEADS)
HGRN_SIZES = (GROUP_WIDTH, GROUP_WIDTH, GROUP_WIDTH, GROUP_WIDTH)
GDN_SIZES = (3 * GROUP_WIDTH, GROUP_WIDTH, N_HEADS, N_HEADS)
MIXER_WIDTHS = (sum(MLSTM_SIZES), sum(MAMBA_SIZES), sum(HGRN_SIZES), sum(GDN_SIZES))
D_IN = sum(MIXER_WIDTHS)

kernel_name = "hybrid_parallel_mlstm_ssd_hgrn2_gdn_macaron"


def split_cols(u, sizes):
    bounds = [int(b) for b in np.cumsum(sizes)[:-1]]
    return jnp.split(u, bounds, axis=-1)


def rmsnorm(x, w):
    xf = x.astype(F32)
    y = xf * lax.rsqrt(jnp.mean(xf * xf, axis=-1, keepdims=True) + EPS)
    return (y * w.astype(F32)).astype(x.dtype)


def group_rmsnorm(y, w, n_groups):
    b, s, wd = y.shape
    yf = y.astype(F32).reshape(b, s, n_groups, wd // n_groups)
    yf = yf * lax.rsqrt(jnp.mean(yf * yf, axis=-1, keepdims=True) + EPS)
    return yf.reshape(b, s, wd) * w.astype(F32)


def l2norm(x):
    return x * lax.rsqrt(jnp.sum(x * x, axis=-1, keepdims=True) + EPS)


def masked_exp(log_val, mask):
    return jnp.where(mask, jnp.exp(jnp.where(mask, log_val, 0.0)), 0.0)


def swiglu(x, w_gate, w_up, w_down):
    return (jax.nn.silu(x @ w_gate) * (x @ w_up)) @ w_down


def causal_conv(x, w):
    k, ch = w.shape
    return lax.conv_general_dilated(x, w[:, None, :].astype(x.dtype), window_strides=(1,),
                                    padding=[(k - 1, 0)], dimension_numbers=("NWC", "WIO", "NWC"),
                                    feature_group_count=ch)


def to_chunks(x, n_heads):
    b, s, w = x.shape
    return x.reshape(b, s // CHUNK, CHUNK, n_heads, w // n_heads).transpose(1, 0, 3, 2, 4)


def gate_chunks(g):
    b, s, h = g.shape
    return g.reshape(b, s // CHUNK, CHUNK, h).transpose(1, 0, 3, 2)


def from_chunks(y):
    nc, b, h, c, d = y.shape
    return y.transpose(1, 0, 3, 2, 4).reshape(b, nc * c, h * d)


def causal_mask():
    return jnp.tril(jnp.ones((CHUNK, CHUNK), dtype=bool))


def mlstm_mixer(q, k, v, o_pre, i_pre, f_pre, i_bias, f_bias, norm_w):
    b_ = q.shape[0]
    qc = to_chunks(q, N_HEADS) * HEAD_DIM ** -0.5
    kc = to_chunks(k, N_HEADS)
    vc = to_chunks(v, N_HEADS)
    log_i = gate_chunks(i_pre + i_bias.astype(F32))
    log_f = gate_chunks(jax.nn.log_sigmoid(f_pre + f_bias.astype(F32)))
    causal = causal_mask()

    def step(carry, inp):
        c_st, n_st, m_st = carry
        qt, kt, vt, li, lf = inp
        bcum = jnp.cumsum(lf, axis=-1)
        log_d = jnp.where(causal, bcum[..., :, None] - bcum[..., None, :] + li[..., None, :], NEG_BIG)
        log_inter = bcum + m_st[..., None]
        m = jnp.maximum(log_inter, jnp.max(log_d, axis=-1))
        scores = jnp.einsum("bhtd,bhsd->bhts", qt, kt) * masked_exp(log_d - m[..., None], causal)
        w_inter = jnp.exp(log_inter - m)
        num = jnp.einsum("bhts,bhse->bhte", scores, vt) + w_inter[..., None] * jnp.einsum("bhtd,bhde->bhte", qt, c_st)
        den = jnp.sum(scores, axis=-1) + w_inter * jnp.einsum("bhtd,bhd->bht", qt, n_st)
        h = num / jnp.maximum(jnp.abs(den), jnp.exp(-m))[..., None]
        b_last = bcum[..., -1]
        log_w = b_last[..., None] - bcum + li
        m_new = jnp.maximum(b_last + m_st, jnp.max(log_w, axis=-1))
        w = jnp.exp(log_w - m_new[..., None])
        decay = jnp.exp(b_last + m_st - m_new)
        c_new = decay[..., None, None] * c_st + jnp.einsum("bhs,bhsd,bhse->bhde", w, kt, vt)
        n_new = decay[..., None] * n_st + jnp.einsum("bhs,bhsd->bhd", w, kt)
        return (c_new, n_new, m_new), h

    carry0 = (jnp.zeros((b_, N_HEADS, HEAD_DIM, HEAD_DIM), F32),
              jnp.zeros((b_, N_HEADS, HEAD_DIM), F32),
              jnp.zeros((b_, N_HEADS), F32))
    _, hc = lax.scan(step, carry0, (qc, kc, vc, log_i, log_f))
    return group_rmsnorm(from_chunks(hc), norm_w, N_HEADS) * jax.nn.sigmoid(o_pre)


def mamba2_mixer(z, xbc, dt_pre, conv_w, conv_b, dt_bias, a_log, d_skip, norm_w):
    xbc = jax.nn.silu(causal_conv(xbc, conv_w) + conv_b.astype(F32))
    xs, b_in, c_in = split_cols(xbc, (GROUP_WIDTH, SSM_GROUPS * SSM_STATE, SSM_GROUPS * SSM_STATE))
    dt = jax.nn.softplus(dt_pre + dt_bias.astype(F32))
    a = -jnp.exp(a_log.astype(F32))
    rep = N_HEADS // SSM_GROUPS
    xc = to_chunks(xs, N_HEADS)
    bc = jnp.repeat(to_chunks(b_in, SSM_GROUPS), rep, axis=2)
    cc = jnp.repeat(to_chunks(c_in, SSM_GROUPS), rep, axis=2)
    dtc = gate_chunks(dt)
    lac = gate_chunks(dt * a)
    causal = causal_mask()

    def step(state, inp):
        xt, bt, ct, dtt, lat = inp
        cum = jnp.cumsum(lat, axis=-1)
        seg = masked_exp(cum[..., :, None] - cum[..., None, :], causal)
        scores = jnp.einsum("bhtn,bhsn->bhts", ct, bt) * seg * dtt[..., None, :]
        y = (jnp.einsum("bhts,bhsp->bhtp", scores, xt)
             + jnp.exp(cum)[..., None] * jnp.einsum("bhtn,bhpn->bhtp", ct, state))
        cum_last = cum[..., -1]
        w = jnp.exp(cum_last[..., None] - cum) * dtt
        state = jnp.exp(cum_last)[..., None, None] * state + jnp.einsum("bhs,bhsp,bhsn->bhpn", w, xt, bt)
        return state, y

    state0 = jnp.zeros((xs.shape[0], N_HEADS, HEAD_DIM, SSM_STATE), F32)
    _, yc = lax.scan(step, state0, (xc, bc, cc, dtc, lac))
    y = from_chunks(yc) + xs * jnp.repeat(d_skip.astype(F32), HEAD_DIM)
    return group_rmsnorm(y * jax.nn.silu(z), norm_w, SSM_GROUPS)


def hgrn2_mixer(q, f_pre, i_in, g, lb, norm_w):
    f = lb + (1.0 - lb) * jax.nn.sigmoid(f_pre)
    log_f = jnp.log(f)
    k = 1.0 - f
    qc = to_chunks(jax.nn.silu(q), N_HEADS)
    kc = to_chunks(k, N_HEADS)
    ic = to_chunks(i_in, N_HEADS)
    lfc = to_chunks(log_f, N_HEADS)
    causal = causal_mask()[..., None]

    def step(state, inp):
        qt, kt, it, lft = inp
        gc = jnp.cumsum(lft, axis=-2)
        dec = masked_exp(gc[..., :, None, :] - gc[..., None, :, :], causal)
        scores = jnp.einsum("bhtd,bhsd,bhtsd->bhts", qt, kt, dec)
        o = (jnp.einsum("bhts,bhse->bhte", scores, it)
             + jnp.einsum("bhtd,bhde->bhte", qt * jnp.exp(gc), state))
        g_last = gc[..., -1, :]
        state = (jnp.exp(g_last)[..., None] * state
                 + jnp.einsum("bhsd,bhse->bhde", kt * jnp.exp(g_last[..., None, :] - gc), it))
        return state, o

    state0 = jnp.zeros((q.shape[0], N_HEADS, HEAD_DIM, HEAD_DIM), F32)
    _, oc = lax.scan(step, state0, (qc, kc, ic, lfc))
    return group_rmsnorm(from_chunks(oc), norm_w, N_HEADS) * jax.nn.silu(g)


def gated_deltanet_mixer(qkv, z, b_pre, a_pre, conv_w, a_log, dt_bias, norm_w):
    qkv = jax.nn.silu(causal_conv(qkv, conv_w))
    q, k, v = split_cols(qkv, (GROUP_WIDTH, GROUP_WIDTH, GROUP_WIDTH))
    qc = l2norm(to_chunks(q, N_HEADS)) * HEAD_DIM ** -0.5
    kc = l2norm(to_chunks(k, N_HEADS))
    vc = to_chunks(v, N_HEADS)
    beta = gate_chunks(jax.nn.sigmoid(b_pre))
    g = gate_chunks(-jnp.exp(a_log.astype(F32)) * jax.nn.softplus(a_pre + dt_bias.astype(F32)))
    gcum = jnp.cumsum(g, axis=-1)
    causal = causal_mask()
    strict = jnp.tril(jnp.ones((CHUNK, CHUNK), dtype=bool), -1)
    decay = masked_exp(gcum[..., :, None] - gcum[..., None, :], causal)
    kb = kc * beta[..., None]
    vb = vc * beta[..., None]
    a_mat = jnp.where(strict, jnp.einsum("nbhtd,nbhsd->nbhts", kb, kc) * decay, 0.0)
    eye = jnp.broadcast_to(jnp.eye(CHUNK, dtype=F32), a_mat.shape)
    t_mat = lax.linalg.triangular_solve(eye + a_mat, eye, left_side=True, lower=True)
    u = t_mat @ vb
    w = t_mat @ (kb * jnp.exp(gcum)[..., None])
    attn = jnp.einsum("nbhtd,nbhsd->nbhts", qc, kc) * decay
    q_dec = qc * jnp.exp(gcum)[..., None]
    g_last = gcum[..., -1]
    k_dec = kc * jnp.exp(g_last[..., None] - gcum)[..., None]

    def step(state, inp):
        ut, wt, at, qdt, kdt, glt = inp
        v_new = ut - jnp.einsum("bhtd,bhde->bhte", wt, state)
        o = jnp.einsum("bhtd,bhde->bhte", qdt, state) + jnp.einsum("bhts,bhse->bhte", at, v_new)
        state = jnp.exp(glt)[..., None, None] * state + jnp.einsum("bhsd,bhse->bhde", kdt, v_new)
        return state, o

    state0 = jnp.zeros((qkv.shape[0], N_HEADS, HEAD_DIM, HEAD_DIM), F32)
    _, oc = lax.scan(step, state0, (u, w, attn, q_dec, k_dec, g_last))
    return group_rmsnorm(from_chunks(oc), norm_w, N_HEADS) * jax.nn.silu(z)


def setup_inputs(seed: int = 0) -> dict:
    key = jax.random.key(seed)
    ks = jax.random.split(key, 32)
    nrm = lambda k, shape, scale: jax.random.normal(k, shape, F32) * scale
    gain = lambda k, shape: 1.0 + 0.02 * jax.random.normal(k, shape, F32)

    def dt_bias_init(k, shape):
        dt = jnp.exp(jax.random.uniform(k, shape, F32, np.log(1e-3), np.log(1e-1)))
        return dt + jnp.log(-jnp.expm1(-dt))

    return {
        "x": jax.random.normal(ks[0], (BATCH, SEQ, D_MODEL), F32),
        "ffn1_norm": gain(ks[1], (DEPTH, D_MODEL)),
        "ffn1_w_gate": nrm(ks[2], (DEPTH, D_MODEL, D_FF), D_MODEL ** -0.5),
        "ffn1_w_up": nrm(ks[3], (DEPTH, D_MODEL, D_FF), D_MODEL ** -0.5),
        "ffn1_w_down": nrm(ks[4], (DEPTH, D_FF, D_MODEL), D_FF ** -0.5),
        "mix_norm": gain(ks[5], (DEPTH, D_MODEL)),
        "w_in": nrm(ks[6], (DEPTH, D_MODEL, D_IN), D_MODEL ** -0.5),
        "w_out": nrm(ks[7], (DEPTH, D_MIX, D_MODEL), D_MIX ** -0.5),
        "mlstm_i_bias": nrm(ks[8], (DEPTH, N_HEADS), 0.1),
        "mlstm_f_bias": jnp.linspace(3.0, 6.0, N_HEADS, dtype=F32)[None, :] + nrm(ks[9], (DEPTH, N_HEADS), 0.1),
        "mlstm_norm": gain(ks[10], (DEPTH, GROUP_WIDTH)),
        "mamba_conv_w": nrm(ks[11], (DEPTH, CONV_K, MAMBA_XBC), CONV_K ** -0.5),
        "mamba_conv_b": nrm(ks[12], (DEPTH, MAMBA_XBC), 0.02),
        "mamba_dt_bias": dt_bias_init(ks[13], (DEPTH, N_HEADS)),
        "mamba_a_log": jnp.log(jax.random.uniform(ks[14], (DEPTH, N_HEADS), F32, 1.0, 16.0)),
        "mamba_d": 1.0 + nrm(ks[15], (DEPTH, N_HEADS), 0.1),
        "mamba_norm": gain(ks[16], (DEPTH, GROUP_WIDTH)),
        "hgrn_lb_logits": nrm(ks[17], (DEPTH, GROUP_WIDTH), 0.5),
        "hgrn_norm": gain(ks[18], (DEPTH, GROUP_WIDTH)),
        "gdn_conv_w": nrm(ks[19], (DEPTH, CONV_K, 3 * GROUP_WIDTH), CONV_K ** -0.5),
        "gdn_a_log": jnp.log(jax.random.uniform(ks[20], (DEPTH, N_HEADS), F32, 1.0, 16.0)),
        "gdn_dt_bias": dt_bias_init(ks[21], (DEPTH, N_HEADS)),
        "gdn_norm": gain(ks[22], (DEPTH, GROUP_WIDTH)),
        "ffn2_norm": gain(ks[23], (DEPTH, D_MODEL)),
        "ffn2_w_gate": nrm(ks[24], (DEPTH, D_MODEL, D_FF), D_MODEL ** -0.5),
        "ffn2_w_up": nrm(ks[25], (DEPTH, D_MODEL, D_FF), D_MODEL ** -0.5),
        "ffn2_w_down": nrm(ks[26], (DEPTH, D_FF, D_MODEL), D_FF ** -0.5),
        "final_norm": gain(ks[27], (D_MODEL,)),
    }


def reference(x, ffn1_norm, ffn1_w_gate, ffn1_w_up, ffn1_w_down, mix_norm, w_in, w_out,
              mlstm_i_bias, mlstm_f_bias, mlstm_norm,
              mamba_conv_w, mamba_conv_b, mamba_dt_bias, mamba_a_log, mamba_d, mamba_norm,
              hgrn_lb_logits, hgrn_norm,
              gdn_conv_w, gdn_a_log, gdn_dt_bias, gdn_norm,
              ffn2_norm, ffn2_w_gate, ffn2_w_up, ffn2_w_down, final_norm):
    p = jax.nn.softmax(hgrn_lb_logits.astype(F32), axis=0)
    lower_bounds = jnp.cumsum(p, axis=0) - p[0:1]

    for l in range(DEPTH):
        x = x + 0.5 * swiglu(rmsnorm(x, ffn1_norm[l]), ffn1_w_gate[l], ffn1_w_up[l], ffn1_w_down[l])
        u = (rmsnorm(x, mix_norm[l]) @ w_in[l]).astype(F32)
        ua, ub, uc, ud = split_cols(u, MIXER_WIDTHS)
        q_a, k_a, v_a, o_a, i_a, f_a = split_cols(ua, MLSTM_SIZES)
        y_a = mlstm_mixer(q_a, k_a, v_a, o_a, i_a, f_a, mlstm_i_bias[l], mlstm_f_bias[l], mlstm_norm[l])
        z_b, xbc_b, dt_b = split_cols(ub, MAMBA_SIZES)
        y_b = mamba2_mixer(z_b, xbc_b, dt_b, mamba_conv_w[l], mamba_conv_b[l], mamba_dt_bias[l],
                           mamba_a_log[l], mamba_d[l], mamba_norm[l])
        q_c, f_c, i_c, g_c = split_cols(uc, HGRN_SIZES)
        y_c = hgrn2_mixer(q_c, f_c, i_c, g_c, lower_bounds[l], hgrn_norm[l])
        qkv_d, z_d, b_d, a_d = split_cols(ud, GDN_SIZES)
        y_d = gated_deltanet_mixer(qkv_d, z_d, b_d, a_d, gdn_conv_w[l], gdn_a_log[l], gdn_dt_bias[l], gdn_norm[l])
        y = jnp.concatenate([y_a, y_b, y_c, y_d], axis=-1).astype(x.dtype)
        x = x + y @ w_out[l]
        x = x + 0.5 * swiglu(rmsnorm(x, ffn2_norm[l]), ffn2_w_gate[l], ffn2_w_up[l], ffn2_w_down[l])
    return rmsnorm(x, final_norm)
```

```python
import functools

import numpy as np
import jax
import jax.numpy as jnp
from jax import lax
from jax.experimental import pallas as pl
from jax.experimental.pallas import tpu as pltpu

F32 = jnp.float32
BF16 = jnp.bfloat16

D_MODEL = 1024
CHUNK = 64
N_HEADS = 4
HEAD_DIM = 64
GROUP_WIDTH = N_HEADS * HEAD_DIM
SSM_STATE = 128
CONV_K = 4
D_FF = 2816
EPS = 1e-6
NEG_BIG = -1e30
LANES = 128
N_GATES = 5

FFN_TOKENS = 512
FF_TILE = 1408
MIX_TOKENS = 256
VMEM_LIMIT = 56 * 1024 * 1024


def _iota(shape, dim):
    return lax.broadcasted_iota(jnp.int32, shape, dim)


def _mm(a, b):
    return jnp.dot(a.astype(BF16), b.astype(BF16), preferred_element_type=F32)


def _mm_nt(a, b):
    return lax.dot_general(a.astype(BF16), b.astype(BF16), (((1,), (1,)), ((), ())),
                           preferred_element_type=F32)


def _mm_tn(a, b):
    return lax.dot_general(a.astype(BF16), b.astype(BF16), (((0,), (0,)), ((), ())),
                           preferred_element_type=F32)


def _mm_f32(a, b):
    return jnp.dot(a, b, precision=lax.Precision.HIGHEST, preferred_element_type=F32)


def _mm_hilo(a, b_bf16):
    hi = a.astype(BF16)
    lo = (a - hi.astype(F32)).astype(BF16)
    return (jnp.dot(hi, b_bf16, preferred_element_type=F32)
            + jnp.dot(lo, b_bf16, preferred_element_type=F32))


def _sigmoid(x):
    return jax.nn.sigmoid(x)


def _silu(x):
    return x * jax.nn.sigmoid(x)


def _softplus(x):
    return jnp.maximum(x, 0.0) + jnp.log1p(jnp.exp(-jnp.abs(x)))


def _masks():
    row = _iota((4 * CHUNK, 1), 0)
    lane = _iota((1, GROUP_WIDTH), 1)
    row_head = row >> 6
    lane_head = lane >> 6
    block_diag = row_head == lane_head
    causal = (row & 63) >= _iota((1, CHUNK), 1)
    return row, lane, row_head, lane_head, block_diag, causal


def _tril64():
    return (_iota((CHUNK, CHUNK), 0) >= _iota((CHUNK, CHUNK), 1)).astype(F32)


def _stack4(x, block_diag):
    return jnp.where(block_diag, jnp.concatenate([x, x, x, x], axis=0), 0.0)


def _unstack4(y, block_diag):
    ym = jnp.where(block_diag, y, 0.0)
    return ym[0:64] + ym[64:128] + ym[128:192] + ym[192:256]


def _col4(g):
    return jnp.concatenate([g[:, h:h + 1] for h in range(N_HEADS)], axis=0)


def _scalar_col4(g):
    return jnp.concatenate([jnp.broadcast_to(g[:, h:h + 1], (CHUNK, 1)) for h in range(N_HEADS)], axis=0)


def _rowblk4(g):
    gt = g.T
    return jnp.concatenate([jnp.broadcast_to(gt[h:h + 1, :], (CHUNK, CHUNK)) for h in range(N_HEADS)], axis=0)


def _expand4(g, lane_head):
    return jnp.where(lane_head == 0, g[:, 0:1],
                     jnp.where(lane_head == 1, g[:, 1:2],
                               jnp.where(lane_head == 2, g[:, 2:3], g[:, 3:4])))


def _head_ones(lane_head):
    row_head = _iota((GROUP_WIDTH, 1), 0) >> 6
    return (row_head == lane_head).astype(BF16)


def _causal_conv_chunk(x, prev8, w):
    acc = x * w[CONV_K - 1:CONV_K, :]
    r8 = _iota((8, 1), 0)
    for j in range(1, CONV_K):
        xr = pltpu.roll(x, j, 0)
        top = jnp.where(r8 < j, pltpu.roll(prev8, j, 0), xr[0:8])
        xs = jnp.concatenate([top, xr[8:]], axis=0)
        acc = acc + xs * w[CONV_K - 1 - j:CONV_K - j, :]
    return acc


def _rms(x, w):
    return x * lax.rsqrt(jnp.mean(x * x, axis=-1, keepdims=True) + EPS) * w


def _ffn_body(*refs, with_mix, with_final):
    it = iter(refs)
    x_ref = next(it)
    if with_mix:
        y_refs = [next(it) for _ in range(4)]
        wo_ref = next(it)
    nw_ref, wg_ref, wu_ref, wd_ref = next(it), next(it), next(it), next(it)
    if with_final:
        fw_ref = next(it)
    o_ref = next(it)

    x = x_ref[...]
    if with_mix:
        y = jnp.concatenate([r[...] for r in y_refs], axis=-1)
        x = x + jnp.dot(y.astype(BF16), wo_ref[...], preferred_element_type=F32)
    xb = _rms(x, nw_ref[...]).astype(BF16)
    acc = jnp.zeros(x.shape, F32)
    for j in range(D_FF // FF_TILE):
        sl = slice(j * FF_TILE, (j + 1) * FF_TILE)
        g = jnp.dot(xb, wg_ref[:, sl], preferred_element_type=F32)
        u = jnp.dot(xb, wu_ref[:, sl], preferred_element_type=F32)
        h = (_silu(g) * u).astype(BF16)
        acc = acc + jnp.dot(h, wd_ref[sl, :], preferred_element_type=F32)
    out = x + 0.5 * acc
    if with_final:
        out = _rms(out, fw_ref[...])
    o_ref[...] = out


def _const_spec(shape):
    return pl.BlockSpec(shape, lambda i: (0,) * len(shape), pipeline_mode=pl.Buffered(1))


def _ffn_call(x2d, nw, wg, wu, wd, mix=None, final_w=None):
    n = x2d.shape[0]
    tm = min(FFN_TOKENS, n)
    tok = lambda w: pl.BlockSpec((tm, w), lambda i: (i, 0))
    args, specs = [x2d], [tok(D_MODEL)]
    if mix is not None:
        ys, wo = mix
        args += list(ys) + [wo]
        specs += [tok(GROUP_WIDTH)] * 4 + [_const_spec((D_MODEL, D_MODEL))]
    args += [nw, wg, wu, wd]
    specs += [_const_spec((1, D_MODEL)), _const_spec((D_MODEL, D_FF)), _const_spec((D_MODEL, D_FF)),
              _const_spec((D_FF, D_MODEL))]
    if final_w is not None:
        args.append(final_w)
        specs.append(_const_spec((1, D_MODEL)))
    return pl.pallas_call(
        functools.partial(_ffn_body, with_mix=mix is not None, with_final=final_w is not None),
        grid=(n // tm,),
        in_specs=specs,
        out_specs=tok(D_MODEL),
        out_shape=jax.ShapeDtypeStruct((n, D_MODEL), F32),
        compiler_params=pltpu.CompilerParams(dimension_semantics=("arbitrary",),
                                             vmem_limit_bytes=VMEM_LIMIT),
        name="ffn",
    )(*args)


def _inproj_body(x_ref, nw_ref, wm_ref, wgate_ref, um_ref, ug_ref):
    xb = _rms(x_ref[...], nw_ref[...]).astype(BF16)
    um_ref[...] = jnp.dot(xb, wm_ref[...], preferred_element_type=F32)
    ug_ref[...] = jnp.dot(xb, wgate_ref[...], preferred_element_type=F32)


def _inproj_call(x2d, nw, w_main, w_gate):
    n = x2d.shape[0]
    tm = min(FFN_TOKENS, n)
    wm, wg = w_main.shape[1], w_gate.shape[1]
    return pl.pallas_call(
        _inproj_body,
        grid=(n // tm,),
        in_specs=[pl.BlockSpec((tm, D_MODEL), lambda i: (i, 0)), _const_spec((1, D_MODEL)),
                  _const_spec((D_MODEL, wm)), _const_spec((D_MODEL, wg))],
        out_specs=[pl.BlockSpec((tm, wm), lambda i: (i, 0)), pl.BlockSpec((tm, wg), lambda i: (i, 0))],
        out_shape=[jax.ShapeDtypeStruct((n, wm), F32), jax.ShapeDtypeStruct((n, wg), F32)],
        compiler_params=pltpu.CompilerParams(dimension_semantics=("arbitrary",),
                                             vmem_limit_bytes=VMEM_LIMIT),
        name="inproj",
    )(x2d, nw, w_main, w_gate)


def _mlstm_chunk(q, k, v, o_pre, gi, gf, ib, fb, nw, c_ref, n_ref, m_ref):
    _, _, _, lane_head, bd, causal = _masks()
    li = gi + ib
    xf = gf + fb
    lf = jnp.minimum(xf, 0.0) - jnp.log1p(jnp.exp(-jnp.abs(xf)))
    bcum = _mm_f32(_tril64(), lf)
    bc = _col4(bcum)
    log_d = jnp.where(causal, bc + _rowblk4(li - bcum), NEG_BIG)
    m_st = m_ref[0:1, :]
    log_inter = bc + _scalar_col4(m_st)
    m = jnp.maximum(log_inter, jnp.max(log_d, axis=-1, keepdims=True))
    d_mat = jnp.where(causal, jnp.exp(jnp.where(causal, log_d - m, 0.0)), 0.0)
    q_st = _stack4(q * (HEAD_DIM ** -0.5), bd)
    p = _mm_nt(q_st, k) * d_mat
    w_inter = jnp.exp(log_inter - m)
    c_bd = c_ref[...]
    n_row = n_ref[0:1, :]
    num = _mm(p, v) + w_inter * _mm(q_st, c_bd)
    den = jnp.sum(p, axis=-1, keepdims=True) + w_inter * jnp.sum(q_st * n_row, axis=-1, keepdims=True)
    h = jnp.where(bd, num / jnp.maximum(jnp.abs(den), jnp.exp(-m)), 0.0)
    hn = h * lax.rsqrt(jnp.sum(h * h, axis=-1, keepdims=True) * (1.0 / HEAD_DIM) + EPS)
    y = (hn[0:64] + hn[64:128] + hn[128:192] + hn[192:256]) * nw * _sigmoid(o_pre)

    b_last = bcum[CHUNK - 1:CHUNK, :]
    log_w = b_last - bcum + li
    m_new = jnp.maximum(b_last + m_st, jnp.max(log_w, axis=0, keepdims=True))
    w = jnp.exp(log_w - m_new)
    decay = _expand4(jnp.exp(b_last + m_st - m_new), lane_head)
    kw = k * _expand4(w, lane_head)
    c_ref[...] = decay * c_bd + jnp.where(bd, _mm_tn(kw, v), 0.0)
    n_ref[0:1, :] = decay * n_row + jnp.sum(kw, axis=0, keepdims=True)
    m_ref[0:1, :] = m_new
    return y


def _mlstm_body(u_ref, gi_ref, gf_ref, ib_ref, fb_ref, nw_ref, y_ref, c_ref, n_ref, m_ref):
    @pl.when(pl.program_id(1) == 0)
    def _():
        c_ref[...] = jnp.zeros_like(c_ref)
        n_ref[...] = jnp.zeros_like(n_ref)
        m_ref[...] = jnp.zeros_like(m_ref)

    def step(c, carry):
        rows = pl.ds(pl.multiple_of(c * CHUNK, CHUNK), CHUNK)
        y_ref[rows, :] = _mlstm_chunk(
            u_ref[rows, 0:256], u_ref[rows, 256:512], u_ref[rows, 512:768], u_ref[rows, 768:1024],
            gi_ref[rows, :], gf_ref[rows, :], ib_ref[...], fb_ref[...], nw_ref[...], c_ref, n_ref, m_ref)
        return carry

    lax.fori_loop(0, u_ref.shape[0] // CHUNK, step, 0)


def _mamba_chunk(z, xbc_raw, gdt, conv_w, conv_b, dtb, alog, dskip, nw, s_ref, prev_ref):
    row, lane, row_head, lane_head, bd, causal = _masks()
    xbc = _silu(_causal_conv_chunk(xbc_raw, prev_ref[...], conv_w) + conv_b)
    prev_ref[...] = xbc_raw[CHUNK - 8:CHUNK, :]
    xs, b_mat, c_mat = xbc[:, 0:256], xbc[:, 256:512], xbc[:, 512:768]
    dt = _softplus(gdt + dtb)
    la = dt * (-jnp.exp(alog))
    cum = _mm_f32(_tril64(), la)
    ccol = _col4(cum)
    seg = jnp.where(causal, jnp.exp(jnp.where(causal, ccol - _rowblk4(cum), 0.0)), 0.0)
    grp = (row_head >> 1) == (lane >> 7)
    c_st = jnp.where(grp, jnp.concatenate([c_mat] * 4, axis=0), 0.0)
    p = _mm_nt(c_st, b_mat) * seg * _rowblk4(dt)
    s_mat = s_ref[...]
    y_full = _mm(p, xs) + jnp.exp(ccol) * _mm_nt(c_st, s_mat)
    y = _unstack4(y_full, bd) + xs * dskip

    cum_last = cum[CHUNK - 1:CHUNK, :]
    w = jnp.exp(cum_last - cum) * dt
    xw = xs * _expand4(w, lane_head)
    s_ref[...] = _scalar_col4(jnp.exp(cum_last)) * s_mat + jnp.where(grp, _mm_tn(xw, b_mat), 0.0)

    yz = y * _silu(z)
    halves = []
    for g in range(2):
        yg = yz[:, g * 128:(g + 1) * 128]
        halves.append(yg * lax.rsqrt(jnp.mean(yg * yg, axis=-1, keepdims=True) + EPS))
    return jnp.concatenate(halves, axis=-1) * nw


def _mamba_body(u_ref, gdt_ref, cw_ref, cb_ref, dtb_ref, alog_ref, dskip_ref, nw_ref, y_ref, s_ref, prev_ref):
    @pl.when(pl.program_id(1) == 0)
    def _():
        s_ref[...] = jnp.zeros_like(s_ref)
        prev_ref[...] = jnp.zeros_like(prev_ref)

    def step(c, carry):
        rows = pl.ds(pl.multiple_of(c * CHUNK, CHUNK), CHUNK)
        y_ref[rows, :] = _mamba_chunk(
            u_ref[rows, 0:256], u_ref[rows, 256:1024], gdt_ref[rows, :], cw_ref[...], cb_ref[...],
            dtb_ref[...], alog_ref[...], dskip_ref[...], nw_ref[...], s_ref, prev_ref)
        return carry

    lax.fori_loop(0, u_ref.shape[0] // CHUNK, step, 0)


HG_SUB = 16


def _hgrn_chunk(qp, fp, i_in, gp, lb, nw, st_ref):
    _, _, _, lane_head, bd, _ = _masks()
    ones_bd = _head_ones(lane_head)
    f = lb + (1.0 - lb) * _sigmoid(fp)
    kk = 1.0 - f
    qq = _silu(qp)
    gc = _mm_f32(_tril64(), jnp.log(f))
    st = st_ref[...]
    inter = _mm_nt(qq * jnp.exp(gc), st)

    bd_sub = (_iota((4 * HG_SUB, 1), 0) >> 4) == lane_head
    t_sub = _iota((HG_SUB, 1), 0)
    outs = []
    for a in range(CHUNK // HG_SUB):
        lo = a * HG_SUB
        q_a, k_a, g_a, i_a = qq[lo:lo + HG_SUB], kk[lo:lo + HG_SUB], gc[lo:lo + HG_SUB], i_in[lo:lo + HG_SUB]
        prods = []
        for s in range(HG_SUB):
            msk = t_sub >= s
            e = jnp.exp(jnp.where(msk, g_a - g_a[s:s + 1, :], 0.0))
            prods.append(jnp.where(msk, q_a * k_a[s:s + 1, :] * e, 0.0))
        z = _mm_hilo(jnp.concatenate(prods, axis=0), ones_bd)
        o = z[0:HG_SUB] * i_a[0:1, :]
        for s in range(1, HG_SUB):
            o = o + z[s * HG_SUB:(s + 1) * HG_SUB] * i_a[s:s + 1, :]
        if a > 0:
            r = gc[lo - 1:lo, :]
            q_st = jnp.where(bd_sub, jnp.concatenate([q_a * jnp.exp(g_a - r)] * 4, axis=0), 0.0)
            sc = _mm_nt(q_st, kk[0:lo] * jnp.exp(r - gc[0:lo]))
            om = jnp.where(bd_sub, _mm(sc, i_in[0:lo]), 0.0)
            o = o + om[0:16] + om[16:32] + om[32:48] + om[48:64]
        outs.append(o)
    o_all = jnp.concatenate(outs, axis=0) + inter

    g_last = gc[CHUNK - 1:CHUNK, :]
    k_dec = kk * jnp.exp(g_last - gc)
    st_ref[...] = jnp.exp(g_last) * st + jnp.where(bd, _mm_tn(i_in, k_dec), 0.0)

    ms = _mm_hilo(o_all * o_all, ones_bd) * (1.0 / HEAD_DIM)
    return o_all * lax.rsqrt(ms + EPS) * nw * _silu(gp)


def _hgrn_body(u_ref, lbl_ref, nw_ref, y_ref, st_ref, *, layer):
    @pl.when(pl.program_id(1) == 0)
    def _():
        st_ref[...] = jnp.zeros_like(st_ref)

    logits = lbl_ref[...]
    e = jnp.exp(logits - jnp.max(logits, axis=0, keepdims=True))
    prob = e / jnp.sum(e, axis=0, keepdims=True)
    lb = jnp.sum(prob[0:layer + 1], axis=0, keepdims=True) - prob[0:1]

    def step(c, carry):
        rows = pl.ds(pl.multiple_of(c * CHUNK, CHUNK), CHUNK)
        y_ref[rows, :] = _hgrn_chunk(
            u_ref[rows, 0:256], u_ref[rows, 256:512], u_ref[rows, 512:768], u_ref[rows, 768:1024],
            lb, nw_ref[...], st_ref)
        return carry

    lax.fori_loop(0, u_ref.shape[0] // CHUNK, step, 0)


def _gdn_chunk(qkv_raw, z, gb, ga, conv_w, alog, dtb, nw, s_ref, prev_ref):
    row, lane, _, lane_head, bd, _ = _masks()
    ones_bd = _head_ones(lane_head)
    qkv = _silu(_causal_conv_chunk(qkv_raw, prev_ref[...], conv_w))
    prev_ref[...] = qkv_raw[CHUNK - 8:CHUNK, :]
    q, k, v = qkv[:, 0:256], qkv[:, 256:512], qkv[:, 512:768]
    qn = q * lax.rsqrt(_mm_hilo(q * q, ones_bd) + EPS) * (HEAD_DIM ** -0.5)
    kn = k * lax.rsqrt(_mm_hilo(k * k, ones_bd) + EPS)
    beta = _sigmoid(gb)
    g = -jnp.exp(alog) * _softplus(ga + dtb)
    gcum = _mm_f32(_tril64(), g)
    be = _expand4(beta, lane_head)
    ge = _expand4(gcum, lane_head)
    kb = kn * be
    vb = v * be

    tt, ss = row & 63, lane & 63
    diag64 = _iota((CHUNK, 1), 0) == ss
    grow = jnp.sum(jnp.where(diag64, ge, 0.0), axis=0, keepdims=True)
    arg = _col4(gcum) - grow
    causal_bd = bd & (tt >= ss)
    strict_bd = bd & (tt > ss)
    dec = jnp.where(causal_bd, jnp.exp(jnp.where(causal_bd, arg, 0.0)), 0.0)
    kn_st = _stack4(kn, bd)
    a_mat = jnp.where(strict_bd, _mm_nt(_stack4(kb, bd), kn_st) * dec, 0.0)
    attn = _mm_nt(_stack4(qn, bd), kn_st) * dec

    pw = -a_mat
    t_mat = jnp.where(row == lane, 1.0, 0.0) + pw
    for _ in range(5):
        pw = _mm_f32(pw, pw)
        t_mat = t_mat + _mm_f32(t_mat, pw)

    rhs = jnp.concatenate([vb, kb * jnp.exp(ge)], axis=-1)
    uw = _mm(t_mat, jnp.concatenate([rhs] * 4, axis=0))
    u = _unstack4(uw[:, 0:256], bd)
    w = _unstack4(uw[:, 256:512], bd)
    s_bd = s_ref[...]
    v_new = u - _mm(w, s_bd)
    o = _mm(qn * jnp.exp(ge), s_bd) + _unstack4(_mm(attn, jnp.concatenate([v_new] * 4, axis=0)), bd)

    g_last = gcum[CHUNK - 1:CHUNK, :]
    k_dec = kn * _expand4(jnp.exp(g_last - gcum), lane_head)
    s_ref[...] = _expand4(jnp.exp(g_last), lane_head) * s_bd + jnp.where(bd, _mm_tn(k_dec, v_new), 0.0)

    ms = _mm_hilo(o * o, ones_bd) * (1.0 / HEAD_DIM)
    return o * lax.rsqrt(ms + EPS) * nw * _silu(z)


def _gdn_body(u_ref, gb_ref, ga_ref, cw_ref, alog_ref, dtb_ref, nw_ref, y_ref, s_ref, prev_ref):
    @pl.when(pl.program_id(1) == 0)
    def _():
        s_ref[...] = jnp.zeros_like(s_ref)
        prev_ref[...] = jnp.zeros_like(prev_ref)

    def step(c, carry):
        rows = pl.ds(pl.multiple_of(c * CHUNK, CHUNK), CHUNK)
        y_ref[rows, :] = _gdn_chunk(
            u_ref[rows, 0:768], u_ref[rows, 768:1024], gb_ref[rows, :], ga_ref[rows, :], cw_ref[...],
            alog_ref[...], dtb_ref[...], nw_ref[...], s_ref, prev_ref)
        return carry

    lax.fori_loop(0, u_ref.shape[0] // CHUNK, step, 0)


def _mixer_call(body, name, u_main, u_gate, mix_idx, gate_ids, params, scratch):
    b, s, _ = u_main.shape
    tb = min(MIX_TOKENS, s)
    in_specs = [pl.BlockSpec((None, tb, 4 * GROUP_WIDTH), lambda bi, i: (bi, i, mix_idx))]
    args = [u_main]
    for gid in gate_ids:
        in_specs.append(pl.BlockSpec((None, tb, LANES), lambda bi, i, gid=gid: (bi, i, gid)))
        args.append(u_gate)
    for p in params:
        in_specs.append(pl.BlockSpec(p.shape, lambda bi, i, nd=p.ndim: (0,) * nd))
        args.append(p)
    return pl.pallas_call(
        body,
        grid=(b, s // tb),
        in_specs=in_specs,
        out_specs=pl.BlockSpec((None, tb, GROUP_WIDTH), lambda bi, i: (bi, i, 0)),
        out_shape=jax.ShapeDtypeStruct((b, s, GROUP_WIDTH), F32),
        scratch_shapes=scratch,
        compiler_params=pltpu.CompilerParams(dimension_semantics=("arbitrary", "arbitrary"),
                                             vmem_limit_bytes=VMEM_LIMIT),
        name=name,
    )(*args)


def _pad_lanes(v):
    v = v.astype(F32).reshape(1, -1)
    return jnp.pad(v, ((0, 0), (0, LANES - v.shape[1])))


def _row(v):
    return v.astype(F32).reshape(1, -1)


_MLSTM0, _MAMBA0, _HGRN0, _GDN0 = 0, 1032, 2060, 3084
_MAIN_COLS = ((_MLSTM0, _MLSTM0 + 1024), (_MAMBA0, _MAMBA0 + 1024), (_HGRN0, _HGRN0 + 1024),
              (_GDN0, _GDN0 + 1024))
_GATE_COLS = (_MLSTM0 + 1024, _MLSTM0 + 1028, _MAMBA0 + 1024, _GDN0 + 1024, _GDN0 + 1028)


def kernel(x, ffn1_norm, ffn1_w_gate, ffn1_w_up, ffn1_w_down, mix_norm, w_in, w_out, mlstm_i_bias, mlstm_f_bias, mlstm_norm, mamba_conv_w, mamba_conv_b, mamba_dt_bias, mamba_a_log, mamba_d, mamba_norm, hgrn_lb_logits, hgrn_norm, gdn_conv_w, gdn_a_log, gdn_dt_bias, gdn_norm, ffn2_norm, ffn2_w_gate, ffn2_w_up, ffn2_w_down, final_norm):
    b, s, d = x.shape
    depth = w_in.shape[0]
    x2d = x.reshape(b * s, d)
    lbl = hgrn_lb_logits.astype(F32)
    for l in range(depth):
        wi = w_in[l]
        w_main = jnp.concatenate([wi[:, a:e] for a, e in _MAIN_COLS], axis=1).astype(BF16)
        w_gate = jnp.concatenate(
            [jnp.pad(wi[:, c:c + N_HEADS], ((0, 0), (0, LANES - N_HEADS))) for c in _GATE_COLS], axis=1).astype(BF16)

        x2d = _ffn_call(x2d, _row(ffn1_norm[l]), ffn1_w_gate[l].astype(BF16), ffn1_w_up[l].astype(BF16),
                        ffn1_w_down[l].astype(BF16))
        um, ug = _inproj_call(x2d, _row(mix_norm[l]), w_main, w_gate)
        um = um.reshape(b, s, 4 * 4 * GROUP_WIDTH)
        ug = ug.reshape(b, s, N_GATES * LANES)

        y_a = _mixer_call(
            _mlstm_body, "mlstm", um, ug, 0, (0, 1),
            [_pad_lanes(mlstm_i_bias[l]), _pad_lanes(mlstm_f_bias[l]), _row(mlstm_norm[l])],
            [pltpu.VMEM((GROUP_WIDTH, GROUP_WIDTH), F32), pltpu.VMEM((8, GROUP_WIDTH), F32),
             pltpu.VMEM((8, LANES), F32)])
        y_b = _mixer_call(
            _mamba_body, "mamba", um, ug, 1, (2,),
            [mamba_conv_w[l].astype(F32), _row(mamba_conv_b[l]), _pad_lanes(mamba_dt_bias[l]),
             _pad_lanes(mamba_a_log[l]), _row(jnp.repeat(mamba_d[l].astype(F32), HEAD_DIM)), _row(mamba_norm[l])],
            [pltpu.VMEM((GROUP_WIDTH, 2 * SSM_STATE), F32), pltpu.VMEM((8, 3 * GROUP_WIDTH), F32)])
        y_c = _mixer_call(
            functools.partial(_hgrn_body, layer=l), "hgrn", um, ug, 2, (),
            [lbl, _row(hgrn_norm[l])],
            [pltpu.VMEM((GROUP_WIDTH, GROUP_WIDTH), F32)])
        y_d = _mixer_call(
            _gdn_body, "gdn", um, ug, 3, (3, 4),
            [gdn_conv_w[l].astype(F32), _pad_lanes(gdn_a_log[l]), _pad_lanes(gdn_dt_bias[l]), _row(gdn_norm[l])],
            [pltpu.VMEM((GROUP_WIDTH, GROUP_WIDTH), F32), pltpu.VMEM((8, 3 * GROUP_WIDTH), F32)])

        ys = [y.reshape(b * s, GROUP_WIDTH) for y in (y_a, y_b, y_c, y_d)]
        x2d = _ffn_call(x2d, _row(ffn2_norm[l]), ffn2_w_gate[l].astype(BF16), ffn2_w_up[l].astype(BF16),
                        ffn2_w_down[l].astype(BF16), mix=(ys, w_out[l].astype(BF16)),
                        final_w=_row(final_norm) if l == depth - 1 else None)
    return x2d.reshape(b, s, d)
```

```python
import functools

import numpy as np
import jax
import jax.numpy as jnp
from jax import lax
from jax.experimental import pallas as pl
from jax.experimental.pallas import tpu as pltpu

F32 = jnp.float32
BF16 = jnp.bfloat16

D_MODEL = 1024
CHUNK = 64
N_HEADS = 4
HEAD_DIM = 64
GROUP_WIDTH = N_HEADS * HEAD_DIM
SSM_STATE = 128
CONV_K = 4
D_FF = 2816
EPS = 1e-6
NEG_BIG = -1e30
LANES = 128
N_GATES = 5

FFN_TOKENS = 512
FF_TILE = 1408
MIX_TOKENS = 256
MIX_UNROLL = 4
VMEM_LIMIT = 56 * 1024 * 1024


def _iota(shape, dim):
    return lax.broadcasted_iota(jnp.int32, shape, dim)


def _mm(a, b):
    return jnp.dot(a.astype(BF16), b.astype(BF16), preferred_element_type=F32)


def _mm_nt(a, b):
    return lax.dot_general(a.astype(BF16), b.astype(BF16), (((1,), (1,)), ((), ())),
                           preferred_element_type=F32)


def _mm_tn(a, b):
    return lax.dot_general(a.astype(BF16), b.astype(BF16), (((0,), (0,)), ((), ())),
                           preferred_element_type=F32)


def _mm_f32(a, b):
    return jnp.dot(a, b, precision=lax.Precision.HIGHEST, preferred_element_type=F32)


def _mm_hilo(a, b_bf16):
    hi = a.astype(BF16)
    lo = (a - hi.astype(F32)).astype(BF16)
    return (jnp.dot(hi, b_bf16, preferred_element_type=F32)
            + jnp.dot(lo, b_bf16, preferred_element_type=F32))


def _sigmoid(x):
    return jax.nn.sigmoid(x)


def _silu(x):
    return x * jax.nn.sigmoid(x)


def _softplus(x):
    return jnp.maximum(x, 0.0) + jnp.log1p(jnp.exp(-jnp.abs(x)))


def _masks():
    row = _iota((4 * CHUNK, 1), 0)
    lane = _iota((1, GROUP_WIDTH), 1)
    row_head = row >> 6
    lane_head = lane >> 6
    block_diag = row_head == lane_head
    causal = (row & 63) >= _iota((1, CHUNK), 1)
    return row, lane, row_head, lane_head, block_diag, causal


def _tril64():
    return (_iota((CHUNK, CHUNK), 0) >= _iota((CHUNK, CHUNK), 1)).astype(F32)


def _stack4(x, block_diag):
    return jnp.where(block_diag, jnp.concatenate([x, x, x, x], axis=0), 0.0)


def _unstack4(y, block_diag):
    ym = jnp.where(block_diag, y, 0.0)
    return ym[0:64] + ym[64:128] + ym[128:192] + ym[192:256]


def _col4(g):
    return jnp.concatenate([g[:, h:h + 1] for h in range(N_HEADS)], axis=0)


def _scalar_col4(g):
    return jnp.concatenate([jnp.broadcast_to(g[:, h:h + 1], (CHUNK, 1)) for h in range(N_HEADS)], axis=0)


def _rowblk4(g):
    gt = g.T
    return jnp.concatenate([jnp.broadcast_to(gt[h:h + 1, :], (CHUNK, CHUNK)) for h in range(N_HEADS)], axis=0)


def _expand4(g, lane_head):
    return jnp.where(lane_head == 0, g[:, 0:1],
                     jnp.where(lane_head == 1, g[:, 1:2],
                               jnp.where(lane_head == 2, g[:, 2:3], g[:, 3:4])))


def _head_ones(lane_head):
    row_head = _iota((GROUP_WIDTH, 1), 0) >> 6
    return (row_head == lane_head).astype(BF16)


def _causal_conv_chunk(x, prev8, w):
    acc = x * w[CONV_K - 1:CONV_K, :]
    r8 = _iota((8, 1), 0)
    for j in range(1, CONV_K):
        xr = pltpu.roll(x, j, 0)
        top = jnp.where(r8 < j, pltpu.roll(prev8, j, 0), xr[0:8])
        xs = jnp.concatenate([top, xr[8:]], axis=0)
        acc = acc + xs * w[CONV_K - 1 - j:CONV_K - j, :]
    return acc


def _rms(x, w):
    return x * lax.rsqrt(jnp.mean(x * x, axis=-1, keepdims=True) + EPS) * w


def _ffn_body(*refs, with_mix, with_final):
    it = iter(refs)
    x_ref = next(it)
    if with_mix:
        y_refs = [next(it) for _ in range(4)]
        wo_ref = next(it)
    nw_ref, wg_ref, wu_ref, wd_ref = next(it), next(it), next(it), next(it)
    if with_final:
        fw_ref = next(it)
    o_ref = next(it)

    x = x_ref[...]
    if with_mix:
        y = jnp.concatenate([r[...] for r in y_refs], axis=-1)
        x = x + jnp.dot(y.astype(BF16), wo_ref[...], preferred_element_type=F32)
    xb = _rms(x, nw_ref[...]).astype(BF16)
    acc = jnp.zeros(x.shape, F32)
    for j in range(D_FF // FF_TILE):
        sl = slice(j * FF_TILE, (j + 1) * FF_TILE)
        g = jnp.dot(xb, wg_ref[:, sl], preferred_element_type=F32)
        u = jnp.dot(xb, wu_ref[:, sl], preferred_element_type=F32)
        h = (_silu(g) * u).astype(BF16)
        acc = acc + jnp.dot(h, wd_ref[sl, :], preferred_element_type=F32)
    out = x + 0.5 * acc
    if with_final:
        out = _rms(out, fw_ref[...])
    o_ref[...] = out


def _const_spec(shape):
    return pl.BlockSpec(shape, lambda i: (0,) * len(shape), pipeline_mode=pl.Buffered(1))


def _ffn_call(x2d, nw, wg, wu, wd, mix=None, final_w=None):
    n = x2d.shape[0]
    tm = min(FFN_TOKENS, n)
    tok = lambda w: pl.BlockSpec((tm, w), lambda i: (i, 0))
    args, specs = [x2d], [tok(D_MODEL)]
    if mix is not None:
        ys, wo = mix
        args += list(ys) + [wo]
        specs += [tok(GROUP_WIDTH)] * 4 + [_const_spec((D_MODEL, D_MODEL))]
    args += [nw, wg, wu, wd]
    specs += [_const_spec((1, D_MODEL)), _const_spec((D_MODEL, D_FF)), _const_spec((D_MODEL, D_FF)),
              _const_spec((D_FF, D_MODEL))]
    if final_w is not None:
        args.append(final_w)
        specs.append(_const_spec((1, D_MODEL)))
    return pl.pallas_call(
        functools.partial(_ffn_body, with_mix=mix is not None, with_final=final_w is not None),
        grid=(n // tm,),
        in_specs=specs,
        out_specs=tok(D_MODEL),
        out_shape=jax.ShapeDtypeStruct((n, D_MODEL), F32),
        compiler_params=pltpu.CompilerParams(dimension_semantics=("arbitrary",),
                                             vmem_limit_bytes=VMEM_LIMIT),
        name="ffn",
    )(*args)


def _inproj_body(x_ref, nw_ref, wm_ref, wgate_ref, um_ref, ug_ref):
    xb = _rms(x_ref[...], nw_ref[...]).astype(BF16)
    um_ref[...] = jnp.dot(xb, wm_ref[...], preferred_element_type=F32)
    ug_ref[...] = jnp.dot(xb, wgate_ref[...], preferred_element_type=F32)


def _inproj_call(x2d, nw, w_main, w_gate):
    n = x2d.shape[0]
    tm = min(FFN_TOKENS, n)
    wm, wg = w_main.shape[1], w_gate.shape[1]
    return pl.pallas_call(
        _inproj_body,
        grid=(n // tm,),
        in_specs=[pl.BlockSpec((tm, D_MODEL), lambda i: (i, 0)), _const_spec((1, D_MODEL)),
                  _const_spec((D_MODEL, wm)), _const_spec((D_MODEL, wg))],
        out_specs=[pl.BlockSpec((tm, wm), lambda i: (i, 0)), pl.BlockSpec((tm, wg), lambda i: (i, 0))],
        out_shape=[jax.ShapeDtypeStruct((n, wm), F32), jax.ShapeDtypeStruct((n, wg), F32)],
        compiler_params=pltpu.CompilerParams(dimension_semantics=("arbitrary",),
                                             vmem_limit_bytes=VMEM_LIMIT),
        name="inproj",
    )(x2d, nw, w_main, w_gate)


def _mlstm_chunk(q, k, v, o_pre, gi, gf, ib, fb, nw, c_ref, n_ref, m_ref):
    _, _, _, lane_head, bd, causal = _masks()
    li = gi + ib
    xf = gf + fb
    lf = jnp.minimum(xf, 0.0) - jnp.log1p(jnp.exp(-jnp.abs(xf)))
    bcum = _mm_f32(_tril64(), lf)
    bc = _col4(bcum)
    log_d = jnp.where(causal, bc + _rowblk4(li - bcum), NEG_BIG)
    m_st = m_ref[0:1, :]
    log_inter = bc + _scalar_col4(m_st)
    m = jnp.maximum(log_inter, jnp.max(log_d, axis=-1, keepdims=True))
    d_mat = jnp.where(causal, jnp.exp(jnp.where(causal, log_d - m, 0.0)), 0.0)
    q_st = _stack4(q * (HEAD_DIM ** -0.5), bd)
    p = _mm_nt(q_st, k) * d_mat
    w_inter = jnp.exp(log_inter - m)
    c_bd = c_ref[...]
    n_row = n_ref[0:1, :]
    num = _mm(p, v) + w_inter * _mm(q_st, c_bd)
    den = jnp.sum(p, axis=-1, keepdims=True) + w_inter * jnp.sum(q_st * n_row, axis=-1, keepdims=True)
    h = jnp.where(bd, num / jnp.maximum(jnp.abs(den), jnp.exp(-m)), 0.0)
    hn = h * lax.rsqrt(jnp.sum(h * h, axis=-1, keepdims=True) * (1.0 / HEAD_DIM) + EPS)
    y = (hn[0:64] + hn[64:128] + hn[128:192] + hn[192:256]) * nw * _sigmoid(o_pre)

    b_last = bcum[CHUNK - 1:CHUNK, :]
    log_w = b_last - bcum + li
    m_new = jnp.maximum(b_last + m_st, jnp.max(log_w, axis=0, keepdims=True))
    w = jnp.exp(log_w - m_new)
    decay = _expand4(jnp.exp(b_last + m_st - m_new), lane_head)
    kw = k * _expand4(w, lane_head)
    c_ref[...] = decay * c_bd + jnp.where(bd, _mm_tn(kw, v), 0.0)
    n_ref[0:1, :] = decay * n_row + jnp.sum(kw, axis=0, keepdims=True)
    m_ref[0:1, :] = m_new
    return y


def _mlstm_body(u_ref, gi_ref, gf_ref, ib_ref, fb_ref, nw_ref, y_ref, c_ref, n_ref, m_ref):
    @pl.when(pl.program_id(1) == 0)
    def _():
        c_ref[...] = jnp.zeros_like(c_ref)
        n_ref[...] = jnp.zeros_like(n_ref)
        m_ref[...] = jnp.zeros_like(m_ref)

    def step(c, carry):
        rows = pl.ds(pl.multiple_of(c * CHUNK, CHUNK), CHUNK)
        y_ref[rows, :] = _mlstm_chunk(
            u_ref[rows, 0:256], u_ref[rows, 256:512], u_ref[rows, 512:768], u_ref[rows, 768:1024],
            gi_ref[rows, :], gf_ref[rows, :], ib_ref[...], fb_ref[...], nw_ref[...], c_ref, n_ref, m_ref)
        return carry

    lax.fori_loop(0, u_ref.shape[0] // CHUNK, step, 0, unroll=MIX_UNROLL)


def _mamba_chunk(z, xbc_raw, gdt, conv_w, conv_b, dtb, alog, dskip, nw, s_ref, prev_ref):
    row, lane, row_head, lane_head, bd, causal = _masks()
    xbc = _silu(_causal_conv_chunk(xbc_raw, prev_ref[...], conv_w) + conv_b)
    prev_ref[...] = xbc_raw[CHUNK - 8:CHUNK, :]
    xs, b_mat, c_mat = xbc[:, 0:256], xbc[:, 256:512], xbc[:, 512:768]
    dt = _softplus(gdt + dtb)
    la = dt * (-jnp.exp(alog))
    cum = _mm_f32(_tril64(), la)
    ccol = _col4(cum)
    seg = jnp.where(causal, jnp.exp(jnp.where(causal, ccol - _rowblk4(cum), 0.0)), 0.0)
    grp = (row_head >> 1) == (lane >> 7)
    c_st = jnp.where(grp, jnp.concatenate([c_mat] * 4, axis=0), 0.0)
    p = _mm_nt(c_st, b_mat) * seg * _rowblk4(dt)
    s_mat = s_ref[...]
    y_full = _mm(p, xs) + jnp.exp(ccol) * _mm_nt(c_st, s_mat)
    y = _unstack4(y_full, bd) + xs * dskip

    cum_last = cum[CHUNK - 1:CHUNK, :]
    w = jnp.exp(cum_last - cum) * dt
    xw = xs * _expand4(w, lane_head)
    s_ref[...] = _scalar_col4(jnp.exp(cum_last)) * s_mat + jnp.where(grp, _mm_tn(xw, b_mat), 0.0)

    yz = y * _silu(z)
    halves = []
    for g in range(2):
        yg = yz[:, g * 128:(g + 1) * 128]
        halves.append(yg * lax.rsqrt(jnp.mean(yg * yg, axis=-1, keepdims=True) + EPS))
    return jnp.concatenate(halves, axis=-1) * nw


def _mamba_body(u_ref, gdt_ref, cw_ref, cb_ref, dtb_ref, alog_ref, dskip_ref, nw_ref, y_ref, s_ref, prev_ref):
    @pl.when(pl.program_id(1) == 0)
    def _():
        s_ref[...] = jnp.zeros_like(s_ref)
        prev_ref[...] = jnp.zeros_like(prev_ref)

    def step(c, carry):
        rows = pl.ds(pl.multiple_of(c * CHUNK, CHUNK), CHUNK)
        y_ref[rows, :] = _mamba_chunk(
            u_ref[rows, 0:256], u_ref[rows, 256:1024], gdt_ref[rows, :], cw_ref[...], cb_ref[...],
            dtb_ref[...], alog_ref[...], dskip_ref[...], nw_ref[...], s_ref, prev_ref)
        return carry

    lax.fori_loop(0, u_ref.shape[0] // CHUNK, step, 0, unroll=MIX_UNROLL)


HG_SUB = 16


def _hgrn_chunk(qp, fp, i_in, gp, lb, nw, st_ref):
    _, _, _, lane_head, bd, _ = _masks()
    ones_bd = _head_ones(lane_head)
    f = lb + (1.0 - lb) * _sigmoid(fp)
    kk = 1.0 - f
    qq = _silu(qp)
    gc = _mm_f32(_tril64(), jnp.log(f))
    st = st_ref[...]
    inter = _mm_nt(qq * jnp.exp(gc), st)

    bd_sub = (_iota((4 * HG_SUB, 1), 0) >> 4) == lane_head
    t_sub = _iota((HG_SUB, 1), 0)
    outs = []
    for a in range(CHUNK // HG_SUB):
        lo = a * HG_SUB
        q_a, k_a, g_a, i_a = qq[lo:lo + HG_SUB], kk[lo:lo + HG_SUB], gc[lo:lo + HG_SUB], i_in[lo:lo + HG_SUB]
        prods = []
        for s in range(HG_SUB):
            msk = t_sub >= s
            e = jnp.exp(jnp.where(msk, g_a - g_a[s:s + 1, :], 0.0))
            prods.append(jnp.where(msk, q_a * k_a[s:s + 1, :] * e, 0.0))
        z = _mm_hilo(jnp.concatenate(prods, axis=0), ones_bd)
        o = z[0:HG_SUB] * i_a[0:1, :]
        for s in range(1, HG_SUB):
            o = o + z[s * HG_SUB:(s + 1) * HG_SUB] * i_a[s:s + 1, :]
        if a > 0:
            r = gc[lo - 1:lo, :]
            q_st = jnp.where(bd_sub, jnp.concatenate([q_a * jnp.exp(g_a - r)] * 4, axis=0), 0.0)
            sc = _mm_nt(q_st, kk[0:lo] * jnp.exp(r - gc[0:lo]))
            om = jnp.where(bd_sub, _mm(sc, i_in[0:lo]), 0.0)
            o = o + om[0:16] + om[16:32] + om[32:48] + om[48:64]
        outs.append(o)
    o_all = jnp.concatenate(outs, axis=0) + inter

    g_last = gc[CHUNK - 1:CHUNK, :]
    k_dec = kk * jnp.exp(g_last - gc)
    st_ref[...] = jnp.exp(g_last) * st + jnp.where(bd, _mm_tn(i_in, k_dec), 0.0)

    ms = _mm_hilo(o_all * o_all, ones_bd) * (1.0 / HEAD_DIM)
    return o_all * lax.rsqrt(ms + EPS) * nw * _silu(gp)


def _hgrn_body(u_ref, lbl_ref, nw_ref, y_ref, st_ref, *, layer):
    @pl.when(pl.program_id(1) == 0)
    def _():
        st_ref[...] = jnp.zeros_like(st_ref)

    logits = lbl_ref[...]
    e = jnp.exp(logits - jnp.max(logits, axis=0, keepdims=True))
    prob = e / jnp.sum(e, axis=0, keepdims=True)
    lb = jnp.sum(prob[0:layer + 1], axis=0, keepdims=True) - prob[0:1]

    def step(c, carry):
        rows = pl.ds(pl.multiple_of(c * CHUNK, CHUNK), CHUNK)
        y_ref[rows, :] = _hgrn_chunk(
            u_ref[rows, 0:256], u_ref[rows, 256:512], u_ref[rows, 512:768], u_ref[rows, 768:1024],
            lb, nw_ref[...], st_ref)
        return carry

    lax.fori_loop(0, u_ref.shape[0] // CHUNK, step, 0, unroll=MIX_UNROLL)


def _gdn_chunk(qkv_raw, z, gb, ga, conv_w, alog, dtb, nw, s_ref, prev_ref):
    row, lane, _, lane_head, bd, _ = _masks()
    ones_bd = _head_ones(lane_head)
    qkv = _silu(_causal_conv_chunk(qkv_raw, prev_ref[...], conv_w))
    prev_ref[...] = qkv_raw[CHUNK - 8:CHUNK, :]
    q, k, v = qkv[:, 0:256], qkv[:, 256:512], qkv[:, 512:768]
    qn = q * lax.rsqrt(_mm_hilo(q * q, ones_bd) + EPS) * (HEAD_DIM ** -0.5)
    kn = k * lax.rsqrt(_mm_hilo(k * k, ones_bd) + EPS)
    beta = _sigmoid(gb)
    g = -jnp.exp(alog) * _softplus(ga + dtb)
    gcum = _mm_f32(_tril64(), g)
    be = _expand4(beta, lane_head)
    ge = _expand4(gcum, lane_head)
    kb = kn * be
    vb = v * be

    tt, ss = row & 63, lane & 63
    diag64 = _iota((CHUNK, 1), 0) == ss
    grow = jnp.sum(jnp.where(diag64, ge, 0.0), axis=0, keepdims=True)
    arg = _col4(gcum) - grow
    causal_bd = bd & (tt >= ss)
    strict_bd = bd & (tt > ss)
    dec = jnp.where(causal_bd, jnp.exp(jnp.where(causal_bd, arg, 0.0)), 0.0)
    kn_st = _stack4(kn, bd)
    a_mat = jnp.where(strict_bd, _mm_nt(_stack4(kb, bd), kn_st) * dec, 0.0)
    attn = _mm_nt(_stack4(qn, bd), kn_st) * dec

    eye = jnp.where(row == lane, 1.0, 0.0)
    pw = -a_mat
    t0 = eye + pw
    for _ in range(5):
        pw = _mm(pw, pw)
        t0 = t0 + _mm(t0, pw)
    a_hi, t_hi = a_mat.astype(BF16), t0.astype(BF16)
    a_lo, t_lo = (a_mat - a_hi.astype(F32)).astype(BF16), (t0 - t_hi.astype(F32)).astype(BF16)
    a_t0 = (jnp.dot(a_hi, t_hi, preferred_element_type=F32) + jnp.dot(a_hi, t_lo, preferred_element_type=F32)
            + jnp.dot(a_lo, t_hi, preferred_element_type=F32))
    t_mat = t0 + _mm(t0, eye - t0 - a_t0)

    rhs = jnp.concatenate([vb, kb * jnp.exp(ge)], axis=-1)
    uw = _mm(t_mat, jnp.concatenate([rhs] * 4, axis=0))
    u = _unstack4(uw[:, 0:256], bd)
    w = _unstack4(uw[:, 256:512], bd)
    s_bd = s_ref[...]
    v_new = u - _mm(w, s_bd)
    o = _mm(qn * jnp.exp(ge), s_bd) + _unstack4(_mm(attn, jnp.concatenate([v_new] * 4, axis=0)), bd)

    g_last = gcum[CHUNK - 1:CHUNK, :]
    k_dec = kn * _expand4(jnp.exp(g_last - gcum), lane_head)
    s_ref[...] = _expand4(jnp.exp(g_last), lane_head) * s_bd + jnp.where(bd, _mm_tn(k_dec, v_new), 0.0)

    ms = _mm_hilo(o * o, ones_bd) * (1.0 / HEAD_DIM)
    return o * lax.rsqrt(ms + EPS) * nw * _silu(z)


def _gdn_body(u_ref, gb_ref, ga_ref, cw_ref, alog_ref, dtb_ref, nw_ref, y_ref, s_ref, prev_ref):
    @pl.when(pl.program_id(1) == 0)
    def _():
        s_ref[...] = jnp.zeros_like(s_ref)
        prev_ref[...] = jnp.zeros_like(prev_ref)

    def step(c, carry):
        rows = pl.ds(pl.multiple_of(c * CHUNK, CHUNK), CHUNK)
        y_ref[rows, :] = _gdn_chunk(
            u_ref[rows, 0:768], u_ref[rows, 768:1024], gb_ref[rows, :], ga_ref[rows, :], cw_ref[...],
            alog_ref[...], dtb_ref[...], nw_ref[...], s_ref, prev_ref)
        return carry

    lax.fori_loop(0, u_ref.shape[0] // CHUNK, step, 0, unroll=MIX_UNROLL)


def _mixer_call(body, name, u_main, u_gate, mix_idx, gate_ids, params, scratch):
    b, s, _ = u_main.shape
    tb = min(MIX_TOKENS, s)
    in_specs = [pl.BlockSpec((None, tb, 4 * GROUP_WIDTH), lambda bi, i: (bi, i, mix_idx))]
    args = [u_main]
    for gid in gate_ids:
        in_specs.append(pl.BlockSpec((None, tb, LANES), lambda bi, i, gid=gid: (bi, i, gid)))
        args.append(u_gate)
    for p in params:
        in_specs.append(pl.BlockSpec(p.shape, lambda bi, i, nd=p.ndim: (0,) * nd))
        args.append(p)
    return pl.pallas_call(
        body,
        grid=(b, s // tb),
        in_specs=in_specs,
        out_specs=pl.BlockSpec((None, tb, GROUP_WIDTH), lambda bi, i: (bi, i, 0)),
        out_shape=jax.ShapeDtypeStruct((b, s, GROUP_WIDTH), F32),
        scratch_shapes=scratch,
        compiler_params=pltpu.CompilerParams(dimension_semantics=("arbitrary", "arbitrary"),
                                             vmem_limit_bytes=VMEM_LIMIT),
        name=name,
    )(*args)


def _pad_lanes(v):
    v = v.astype(F32).reshape(1, -1)
    return jnp.pad(v, ((0, 0), (0, LANES - v.shape[1])))


def _row(v):
    return v.astype(F32).reshape(1, -1)


_MLSTM0, _MAMBA0, _HGRN0, _GDN0 = 0, 1032, 2060, 3084
_MAIN_COLS = ((_MLSTM0, _MLSTM0 + 1024), (_MAMBA0, _MAMBA0 + 1024), (_HGRN0, _HGRN0 + 1024),
              (_GDN0, _GDN0 + 1024))
_GATE_COLS = (_MLSTM0 + 1024, _MLSTM0 + 1028, _MAMBA0 + 1024, _GDN0 + 1024, _GDN0 + 1028)


def kernel(x, ffn1_norm, ffn1_w_gate, ffn1_w_up, ffn1_w_down, mix_norm, w_in, w_out, mlstm_i_bias, mlstm_f_bias, mlstm_norm, mamba_conv_w, mamba_conv_b, mamba_dt_bias, mamba_a_log, mamba_d, mamba_norm, hgrn_lb_logits, hgrn_norm, gdn_conv_w, gdn_a_log, gdn_dt_bias, gdn_norm, ffn2_norm, ffn2_w_gate, ffn2_w_up, ffn2_w_down, final_norm):
    b, s, d = x.shape
    depth = w_in.shape[0]
    x2d = x.reshape(b * s, d)
    lbl = hgrn_lb_logits.astype(F32)
    for l in range(depth):
        wi = w_in[l]
        w_main = jnp.concatenate([wi[:, a:e] for a, e in _MAIN_COLS], axis=1).astype(BF16)
        w_gate = jnp.concatenate(
            [jnp.pad(wi[:, c:c + N_HEADS], ((0, 0), (0, LANES - N_HEADS))) for c in _GATE_COLS], axis=1).astype(BF16)

        x2d = _ffn_call(x2d, _row(ffn1_norm[l]), ffn1_w_gate[l].astype(BF16), ffn1_w_up[l].astype(BF16),
                        ffn1_w_down[l].astype(BF16))
        um, ug = _inproj_call(x2d, _row(mix_norm[l]), w_main, w_gate)
        um = um.reshape(b, s, 4 * 4 * GROUP_WIDTH)
        ug = ug.reshape(b, s, N_GATES * LANES)

        y_a = _mixer_call(
            _mlstm_body, "mlstm", um, ug, 0, (0, 1),
            [_pad_lanes(mlstm_i_bias[l]), _pad_lanes(mlstm_f_bias[l]), _row(mlstm_norm[l])],
            [pltpu.VMEM((GROUP_WIDTH, GROUP_WIDTH), F32), pltpu.VMEM((8, GROUP_WIDTH), F32),
             pltpu.VMEM((8, LANES), F32)])
        y_b = _mixer_call(
            _mamba_body, "mamba", um, ug, 1, (2,),
            [mamba_conv_w[l].astype(F32), _row(mamba_conv_b[l]), _pad_lanes(mamba_dt_bias[l]),
             _pad_lanes(mamba_a_log[l]), _row(jnp.repeat(mamba_d[l].astype(F32), HEAD_DIM)), _row(mamba_norm[l])],
            [pltpu.VMEM((GROUP_WIDTH, 2 * SSM_STATE), F32), pltpu.VMEM((8, 3 * GROUP_WIDTH), F32)])
        y_c = _mixer_call(
            functools.partial(_hgrn_body, layer=l), "hgrn", um, ug, 2, (),
            [lbl, _row(hgrn_norm[l])],
            [pltpu.VMEM((GROUP_WIDTH, GROUP_WIDTH), F32)])
        y_d = _mixer_call(
            _gdn_body, "gdn", um, ug, 3, (3, 4),
            [gdn_conv_w[l].astype(F32), _pad_lanes(gdn_a_log[l]), _pad_lanes(gdn_dt_bias[l]), _row(gdn_norm[l])],
            [pltpu.VMEM((GROUP_WIDTH, GROUP_WIDTH), F32), pltpu.VMEM((8, 3 * GROUP_WIDTH), F32)])

        ys = [y.reshape(b * s, GROUP_WIDTH) for y in (y_a, y_b, y_c, y_d)]
        x2d = _ffn_call(x2d, _row(ffn2_norm[l]), ffn2_w_gate[l].astype(BF16), ffn2_w_up[l].astype(BF16),
                        ffn2_w_down[l].astype(BF16), mix=(ys, w_out[l].astype(BF16)),
                        final_w=_row(final_norm) if l == depth - 1 else None)
    return x2d.reshape(b, s, d)
```

```python
import functools

import numpy as np
import jax
import jax.numpy as jnp
from jax import lax
from jax.experimental import pallas as pl
from jax.experimental.pallas import tpu as pltpu

F32 = jnp.float32
BF16 = jnp.bfloat16

D_MODEL = 1024
CHUNK = 64
N_HEADS = 4
HEAD_DIM = 64
GROUP_WIDTH = N_HEADS * HEAD_DIM
SSM_STATE = 128
CONV_K = 4
D_FF = 2816
EPS = 1e-6
NEG_BIG = -1e30
LANES = 128
N_GATES = 5

FFN_TOKENS = 512
FF_TILE = 1408
MIX_TOKENS = 256
VMEM_LIMIT = 56 * 1024 * 1024


def _iota(shape, dim):
    return lax.broadcasted_iota(jnp.int32, shape, dim)


def _mm(a, b):
    return jnp.dot(a.astype(BF16), b.astype(BF16), preferred_element_type=F32)


def _mm_nt(a, b):
    return lax.dot_general(a.astype(BF16), b.astype(BF16), (((1,), (1,)), ((), ())),
                           preferred_element_type=F32)


def _mm_tn(a, b):
    return lax.dot_general(a.astype(BF16), b.astype(BF16), (((0,), (0,)), ((), ())),
                           preferred_element_type=F32)


def _split2(a):
    hi = a.astype(BF16)
    return hi, (a - hi.astype(F32)).astype(BF16)


def _mm_hilo(a, b_bf16):
    hi, lo = _split2(a)
    return (jnp.dot(hi, b_bf16, preferred_element_type=F32)
            + jnp.dot(lo, b_bf16, preferred_element_type=F32))


def _sigmoid(x):
    return jax.nn.sigmoid(x)


def _silu(x):
    return x * jax.nn.sigmoid(x)


def _softplus(x):
    return jnp.maximum(x, 0.0) + jnp.log1p(jnp.exp(-jnp.abs(x)))


class _Masks:
    def __init__(self):
        row = _iota((GROUP_WIDTH, 1), 0)
        lane = _iota((1, GROUP_WIDTH), 1)
        t = _iota((CHUNK, 1), 0)
        self.row, self.lane = row, lane
        self.lane_head = lane >> 6
        self.block_diag = (row >> 6) == self.lane_head
        self.ones_bd = self.block_diag.astype(BF16)
        s = lane & 63
        self.causal = t >= s
        self.strict = t > s
        self.diag = t == s


def _bd(x, mk):
    return jnp.where(mk.block_diag, jnp.concatenate([x, x, x, x], axis=0), 0.0)


def _unbd(y, mk):
    ym = jnp.where(mk.block_diag, y, 0.0)
    return ym[0:64] + ym[64:128] + ym[128:192] + ym[192:256]


def _expand4(g, mk):
    lh = mk.lane_head
    return jnp.where(lh == 0, g[:, 0:1], jnp.where(lh == 1, g[:, 1:2], jnp.where(lh == 2, g[:, 2:3], g[:, 3:4])))


def _rowvec(ge, mk):
    return jnp.sum(jnp.where(mk.diag, ge, 0.0), axis=0, keepdims=True)


def _chunk_cumsum(x):
    rows = x.shape[0]
    r, c = _iota((rows, rows), 0), _iota((rows, rows), 1)
    tril = (((r >> 6) == (c >> 6)) & (r >= c)).astype(BF16)
    x1 = x.astype(BF16)
    r1 = x - x1.astype(F32)
    x2 = r1.astype(BF16)
    x3 = (r1 - x2.astype(F32)).astype(BF16)
    return (jnp.dot(tril, x1, preferred_element_type=F32) + jnp.dot(tril, x2, preferred_element_type=F32)
            + jnp.dot(tril, x3, preferred_element_type=F32))


def _chunk_cummax(x):
    t = _iota((x.shape[0], 1), 0) & 63
    k = 1
    while k < CHUNK:
        x = jnp.maximum(x, jnp.where(t >= k, pltpu.roll(x, k, 0), NEG_BIG))
        k *= 2
    return x


def _causal_conv(x, prev8, w):
    acc = x * w[CONV_K - 1:CONV_K, :]
    r8 = _iota((8, 1), 0)
    for j in range(1, CONV_K):
        xr = pltpu.roll(x, j, 0)
        top = jnp.where(r8 < j, pltpu.roll(prev8, j, 0), xr[0:8])
        xs = jnp.concatenate([top, xr[8:]], axis=0)
        acc = acc + xs * w[CONV_K - 1 - j:CONV_K - j, :]
    return acc


def _head_rmsnorm(o, mk):
    return o * lax.rsqrt(_mm_hilo(o * o, mk.ones_bd) * (1.0 / HEAD_DIM) + EPS)


def _chunks(n_rows):
    return [slice(c * CHUNK, (c + 1) * CHUNK) for c in range(n_rows // CHUNK)]


def _rms(x, w):
    return x * lax.rsqrt(jnp.mean(x * x, axis=-1, keepdims=True) + EPS) * w


def _ffn_body(*refs, with_mix, with_final):
    it = iter(refs)
    x_ref = next(it)
    if with_mix:
        y_refs = [next(it) for _ in range(4)]
        wo_ref = next(it)
    nw_ref, wg_ref, wu_ref, wd_ref = next(it), next(it), next(it), next(it)
    if with_final:
        fw_ref = next(it)
    o_ref = next(it)

    x = x_ref[...]
    if with_mix:
        y = jnp.concatenate([r[...] for r in y_refs], axis=-1)
        x = x + jnp.dot(y.astype(BF16), wo_ref[...], preferred_element_type=F32)
    xb = _rms(x, nw_ref[...]).astype(BF16)
    acc = jnp.zeros(x.shape, F32)
    for j in range(D_FF // FF_TILE):
        sl = slice(j * FF_TILE, (j + 1) * FF_TILE)
        g = jnp.dot(xb, wg_ref[:, sl], preferred_element_type=F32)
        u = jnp.dot(xb, wu_ref[:, sl], preferred_element_type=F32)
        h = (_silu(g) * u).astype(BF16)
        acc = acc + jnp.dot(h, wd_ref[sl, :], preferred_element_type=F32)
    out = x + 0.5 * acc
    if with_final:
        out = _rms(out, fw_ref[...])
    o_ref[...] = out


def _const_spec(shape):
    return pl.BlockSpec(shape, lambda i: (0,) * len(shape), pipeline_mode=pl.Buffered(1))


def _ffn_call(x2d, nw, wg, wu, wd, mix=None, final_w=None):
    n = x2d.shape[0]
    tm = min(FFN_TOKENS, n)
    tok = lambda w: pl.BlockSpec((tm, w), lambda i: (i, 0))
    args, specs = [x2d], [tok(D_MODEL)]
    if mix is not None:
        ys, wo = mix
        args += list(ys) + [wo]
        specs += [tok(GROUP_WIDTH)] * 4 + [_const_spec((D_MODEL, D_MODEL))]
    args += [nw, wg, wu, wd]
    specs += [_const_spec((1, D_MODEL)), _const_spec((D_MODEL, D_FF)), _const_spec((D_MODEL, D_FF)),
              _const_spec((D_FF, D_MODEL))]
    if final_w is not None:
        args.append(final_w)
        specs.append(_const_spec((1, D_MODEL)))
    return pl.pallas_call(
        functools.partial(_ffn_body, with_mix=mix is not None, with_final=final_w is not None),
        grid=(n // tm,),
        in_specs=specs,
        out_specs=tok(D_MODEL),
        out_shape=jax.ShapeDtypeStruct((n, D_MODEL), F32),
        compiler_params=pltpu.CompilerParams(dimension_semantics=("arbitrary",),
                                             vmem_limit_bytes=VMEM_LIMIT),
        name="ffn",
    )(*args)


def _inproj_body(x_ref, nw_ref, wm_ref, wgate_ref, um_ref, ug_ref):
    xb = _rms(x_ref[...], nw_ref[...]).astype(BF16)
    um_ref[...] = jnp.dot(xb, wm_ref[...], preferred_element_type=F32)
    ug_ref[...] = jnp.dot(xb, wgate_ref[...], preferred_element_type=F32)


def _inproj_call(x2d, nw, w_main, w_gate):
    n = x2d.shape[0]
    tm = min(FFN_TOKENS, n)
    wm, wg = w_main.shape[1], w_gate.shape[1]
    return pl.pallas_call(
        _inproj_body,
        grid=(n // tm,),
        in_specs=[pl.BlockSpec((tm, D_MODEL), lambda i: (i, 0)), _const_spec((1, D_MODEL)),
                  _const_spec((D_MODEL, wm)), _const_spec((D_MODEL, wg))],
        out_specs=[pl.BlockSpec((tm, wm), lambda i: (i, 0)), pl.BlockSpec((tm, wg), lambda i: (i, 0))],
        out_shape=[jax.ShapeDtypeStruct((n, wm), F32), jax.ShapeDtypeStruct((n, wg), F32)],
        compiler_params=pltpu.CompilerParams(dimension_semantics=("arbitrary",),
                                             vmem_limit_bytes=VMEM_LIMIT),
        name="inproj",
    )(x2d, nw, w_main, w_gate)


def _mlstm_body(u_ref, gi_ref, gf_ref, ib_ref, fb_ref, nw_ref, y_ref, c_ref, n_ref, m_ref):
    @pl.when(pl.program_id(1) == 0)
    def _():
        c_ref[...] = jnp.zeros_like(c_ref)
        n_ref[...] = jnp.zeros_like(n_ref)
        m_ref[...] = jnp.zeros_like(m_ref)

    mk = _Masks()
    chunks = _chunks(u_ref.shape[0])
    li = gi_ref[...] + ib_ref[...]
    xf = gf_ref[...] + fb_ref[...]
    lf = jnp.minimum(xf, 0.0) - jnp.log1p(jnp.exp(-jnp.abs(xf)))
    bcum = _chunk_cumsum(lf)
    a = li - bcum
    cmax = _chunk_cummax(a)

    m_run = m_ref[0:1, :]
    m_start, m_tot = [], []
    for sl in chunks:
        last = slice(sl.stop - 1, sl.stop)
        m_start.append(m_run)
        m_tot.append(jnp.maximum(m_run, cmax[last]))
        m_run = bcum[last] + m_tot[-1]
    m_ref[0:1, :] = m_run

    qs = [u_ref[sl, 0:256] * (HEAD_DIM ** -0.5) for sl in chunks]
    ks = [u_ref[sl, 256:512] for sl in chunks]
    vs = [u_ref[sl, 512:768] for sl in chunks]
    mx = [jnp.maximum(m0, cmax[sl]) for m0, sl in zip(m_start, chunks)]
    arow = [_rowvec(_expand4(a[sl], mk), mk) for sl in chunks]
    dmat = [jnp.where(mk.causal, jnp.exp(jnp.where(mk.causal, ar - _expand4(m, mk), 0.0)), 0.0)
            for ar, m in zip(arow, mx)]
    p = [_mm_nt(q, _bd(k, mk)) * d for q, k, d in zip(qs, ks, dmat)]
    num_i = [_mm(pp, _bd(v, mk)) for pp, v in zip(p, vs)]
    den_i = [_mm_hilo(pp, mk.ones_bd) for pp in p]
    w_int = [_expand4(jnp.exp(m0 - m), mk) for m0, m in zip(m_start, mx)]
    bound = [_expand4(jnp.exp(-(bcum[sl] + m)), mk) for sl, m in zip(chunks, mx)]
    kw = [k * _expand4(jnp.exp(a[sl] - mt), mk) for k, sl, mt in zip(ks, chunks, m_tot)]
    decay = [_expand4(jnp.exp(m0 - mt), mk) for m0, mt in zip(m_start, m_tot)]
    d_c = [jnp.where(mk.block_diag, _mm_tn(kk, v), 0.0) for kk, v in zip(kw, vs)]
    d_n = [jnp.sum(kk, axis=0, keepdims=True) for kk in kw]

    nw = nw_ref[...]
    for i, sl in enumerate(chunks):
        c_bd = c_ref[...]
        n_row = n_ref[0:1, :]
        num = num_i[i] + w_int[i] * _mm(qs[i], c_bd)
        den = den_i[i] + w_int[i] * _mm_hilo(qs[i] * n_row, mk.ones_bd)
        h = num / jnp.maximum(jnp.abs(den), bound[i])
        y_ref[sl, :] = _head_rmsnorm(h, mk) * nw * _sigmoid(u_ref[sl, 768:1024])
        c_ref[...] = decay[i] * c_bd + d_c[i]
        n_ref[0:1, :] = decay[i] * n_row + d_n[i]


def _mamba_body(u_ref, gdt_ref, cw_ref, cb_ref, dtb_ref, alog_ref, dskip_ref, nw_ref, y_ref, s_ref, prev_ref):
    @pl.when(pl.program_id(1) == 0)
    def _():
        s_ref[...] = jnp.zeros_like(s_ref)
        prev_ref[...] = jnp.zeros_like(prev_ref)

    mk = _Masks()
    rows = u_ref.shape[0]
    chunks = _chunks(rows)
    raw = u_ref[:, 256:1024]
    xbc = _silu(_causal_conv(raw, prev_ref[...], cw_ref[...]) + cb_ref[...])
    prev_ref[...] = raw[rows - 8:rows, :]
    dt = _softplus(gdt_ref[...] + dtb_ref[...])
    cum = _chunk_cumsum(dt * (-jnp.exp(alog_ref[...])))
    grp_b = ((mk.row >> 6) >> 1) == (mk.lane >> 7)
    grp_s = (mk.row >> 7) == (mk.lane_head >> 1)

    xs = [xbc[sl, 0:256] for sl in chunks]
    bm = [xbc[sl, 256:512] for sl in chunks]
    cm = [xbc[sl, 512:768] for sl in chunks]
    cum_e = [_expand4(cum[sl], mk) for sl in chunks]
    seg = [jnp.where(mk.causal, jnp.exp(jnp.where(mk.causal, ce - _rowvec(ce, mk), 0.0)), 0.0) for ce in cum_e]
    dtrow = [_rowvec(_expand4(dt[sl], mk), mk) for sl in chunks]
    p = [_mm_nt(c, jnp.where(grp_b, jnp.concatenate([b] * 4, axis=0), 0.0)) * sg * dr
         for c, b, sg, dr in zip(cm, bm, seg, dtrow)]
    y_i = [_mm(pp, _bd(x, mk)) + x * dskip_ref[...] for pp, x in zip(p, xs)]
    last = [slice(sl.stop - 1, sl.stop) for sl in chunks]
    xw = [x * _expand4(jnp.exp(cum[ls] - cum[sl]) * dt[sl], mk) for x, sl, ls in zip(xs, chunks, last)]
    d_s = [jnp.where(grp_s, _mm_tn(b, w), 0.0) for b, w in zip(bm, xw)]
    decay = [_expand4(jnp.exp(cum[ls]), mk) for ls in last]
    e_cum = [jnp.exp(ce) for ce in cum_e]

    nw = nw_ref[...]
    for i, sl in enumerate(chunks):
        s_mat = s_ref[...]
        y = y_i[i] + e_cum[i] * _mm(cm[i], s_mat)
        yz = y * _silu(u_ref[sl, 0:256])
        halves = []
        for g in range(2):
            yg = yz[:, g * 128:(g + 1) * 128]
            halves.append(yg * lax.rsqrt(jnp.mean(yg * yg, axis=-1, keepdims=True) + EPS))
        y_ref[sl, :] = jnp.concatenate(halves, axis=-1) * nw
        s_ref[...] = decay[i] * s_mat + d_s[i]


HG_SUB = 16


def _stack_rows(x, mk):
    n = x.shape[0]
    r = _iota((4 * n, 1), 0)
    blk = (r >= n).astype(jnp.int32) + (r >= 2 * n).astype(jnp.int32) + (r >= 3 * n).astype(jnp.int32)
    return jnp.where(blk == mk.lane_head, jnp.concatenate([x, x, x, x], axis=0), 0.0)


def _hgrn_intra(qq, kk, gc, i_in, mk):
    t_sub = _iota((HG_SUB, 1), 0)
    outs = []
    for a in range(CHUNK // HG_SUB):
        lo = a * HG_SUB
        q_a, k_a, g_a, i_a = qq[lo:lo + HG_SUB], kk[lo:lo + HG_SUB], gc[lo:lo + HG_SUB], i_in[lo:lo + HG_SUB]
        prods = []
        for s in range(HG_SUB):
            msk = t_sub >= s
            e = jnp.exp(jnp.where(msk, g_a - g_a[s:s + 1, :], 0.0))
            prods.append(jnp.where(msk, q_a * k_a[s:s + 1, :] * e, 0.0))
        z = _mm_hilo(jnp.concatenate(prods, axis=0), mk.ones_bd)
        o = z[0:HG_SUB] * i_a[0:1, :]
        for s in range(1, HG_SUB):
            o = o + z[s * HG_SUB:(s + 1) * HG_SUB] * i_a[s:s + 1, :]
        if a > 0:
            r = gc[lo - 1:lo, :]
            sc = _mm_nt(q_a * jnp.exp(g_a - r), _stack_rows(kk[0:lo] * jnp.exp(r - gc[0:lo]), mk))
            o = o + _mm(sc, _stack_rows(i_in[0:lo], mk))
        outs.append(o)
    return jnp.concatenate(outs, axis=0)


def _hgrn_body(u_ref, lbl_ref, nw_ref, y_ref, st_ref, *, layer):
    @pl.when(pl.program_id(1) == 0)
    def _():
        st_ref[...] = jnp.zeros_like(st_ref)

    logits = lbl_ref[...]
    e = jnp.exp(logits - jnp.max(logits, axis=0, keepdims=True))
    prob = e / jnp.sum(e, axis=0, keepdims=True)
    lb = jnp.sum(prob[0:layer + 1], axis=0, keepdims=True) - prob[0:1]

    mk = _Masks()
    chunks = _chunks(u_ref.shape[0])
    f = lb + (1.0 - lb) * _sigmoid(u_ref[:, 256:512])
    kk_all = 1.0 - f
    qq_all = _silu(u_ref[:, 0:256])
    gc_all = _chunk_cumsum(jnp.log(f))

    qq = [qq_all[sl] for sl in chunks]
    kk = [kk_all[sl] for sl in chunks]
    gc = [gc_all[sl] for sl in chunks]
    ii = [u_ref[sl, 512:768] for sl in chunks]
    o_i = [_hgrn_intra(q, k, g, i, mk) for q, k, g, i in zip(qq, kk, gc, ii)]
    qe = [q * jnp.exp(g) for q, g in zip(qq, gc)]
    g_last = [g[CHUNK - 1:CHUNK, :] for g in gc]
    d_s = [jnp.where(mk.block_diag, _mm_tn(i, k * jnp.exp(gl - g)), 0.0)
           for i, k, g, gl in zip(ii, kk, gc, g_last)]
    decay = [jnp.exp(gl) for gl in g_last]

    nw = nw_ref[...]
    for i, sl in enumerate(chunks):
        st = st_ref[...]
        o = o_i[i] + _mm_nt(qe[i], st)
        y_ref[sl, :] = _head_rmsnorm(o, mk) * nw * _silu(u_ref[sl, 768:1024])
        st_ref[...] = decay[i] * st + d_s[i]


def _gdn_body(u_ref, gb_ref, ga_ref, cw_ref, alog_ref, dtb_ref, nw_ref, y_ref, s_ref, prev_ref):
    @pl.when(pl.program_id(1) == 0)
    def _():
        s_ref[...] = jnp.zeros_like(s_ref)
        prev_ref[...] = jnp.zeros_like(prev_ref)

    mk = _Masks()
    rows = u_ref.shape[0]
    chunks = _chunks(rows)
    raw = u_ref[:, 0:768]
    qkv = _silu(_causal_conv(raw, prev_ref[...], cw_ref[...]))
    prev_ref[...] = raw[rows - 8:rows, :]
    q, k, v = qkv[:, 0:256], qkv[:, 256:512], qkv[:, 512:768]
    qn = q * lax.rsqrt(_mm_hilo(q * q, mk.ones_bd) + EPS) * (HEAD_DIM ** -0.5)
    kn = k * lax.rsqrt(_mm_hilo(k * k, mk.ones_bd) + EPS)
    gcum = _chunk_cumsum(-jnp.exp(alog_ref[...]) * _softplus(ga_ref[...] + dtb_ref[...]))
    be = _expand4(_sigmoid(gb_ref[...]), mk)
    ge = _expand4(gcum, mk)
    e_g = jnp.exp(ge)
    kb = kn * be
    vb = v * be
    kbg = kb * e_g
    q_dec = qn * e_g
    eye = jnp.where(mk.diag, 1.0, 0.0)

    dec = [jnp.where(mk.causal, jnp.exp(jnp.where(mk.causal, ge[sl] - _rowvec(ge[sl], mk), 0.0)), 0.0)
           for sl in chunks]
    qk = [_mm_nt(jnp.concatenate([kb[sl], qn[sl]], axis=0), _bd(kn[sl], mk)) for sl in chunks]
    a_p = [jnp.where(mk.strict, x[0:CHUNK] * d, 0.0) for x, d in zip(qk, dec)]
    attn = [x[CHUNK:2 * CHUNK] * d for x, d in zip(qk, dec)]

    pw = [_mm(-x, _bd(-x, mk)) for x in a_p]
    t0 = [eye - x for x in a_p]
    for j in range(1, 6):
        r = [_mm(jnp.concatenate([t, m], axis=0), _bd(m, mk)) for t, m in zip(t0, pw)]
        t0 = [t + x[0:CHUNK] for t, x in zip(t0, r)]
        pw = [x[CHUNK:2 * CHUNK] for x in r]
    t_mat = []
    for x, t in zip(a_p, t0):
        a_hi, a_lo = _split2(x)
        t_hi, t_lo = _split2(t)
        r = jnp.dot(jnp.concatenate([a_hi, a_lo], axis=0), _bd(t_hi, mk).astype(BF16), preferred_element_type=F32)
        a_t0 = r[0:CHUNK] + r[CHUNK:2 * CHUNK] + jnp.dot(a_hi, _bd(t_lo, mk).astype(BF16), preferred_element_type=F32)
        t_mat.append(t + _mm(t, _bd(eye - t - a_t0, mk)))

    uw = [_mm(t, jnp.concatenate([_bd(vb[sl], mk), _bd(kbg[sl], mk)], axis=1)) for t, sl in zip(t_mat, chunks)]
    u = [x[:, 0:256] for x in uw]
    w = [x[:, 256:512] for x in uw]
    au = [_mm(at, jnp.concatenate([_bd(uu, mk), _bd(ww, mk)], axis=1)) for at, uu, ww in zip(attn, u, w)]
    o_i = [x[:, 0:256] for x in au]
    q2 = [q_dec[sl] - x[:, 256:512] for sl, x in zip(chunks, au)]
    last = [slice(sl.stop - 1, sl.stop) for sl in chunks]
    k_dec = [kn[sl] * _expand4(jnp.exp(gcum[ls] - gcum[sl]), mk) for sl, ls in zip(chunks, last)]
    fg = [_mm_tn(kd, jnp.concatenate([ww, uu], axis=1)) for kd, ww, uu in zip(k_dec, w, u)]
    f_p = [-_unbd(x[:, 0:256], mk) for x in fg]
    g_p = [_unbd(x[:, 256:512], mk) for x in fg]
    decay = [_expand4(jnp.exp(gcum[ls]), mk) for ls in last]

    nw = nw_ref[...]
    for i, sl in enumerate(chunks):
        s_p = s_ref[...]
        r = _mm(jnp.concatenate([q2[i], f_p[i]], axis=0), _bd(s_p, mk))
        o = o_i[i] + r[0:CHUNK]
        y_ref[sl, :] = _head_rmsnorm(o, mk) * nw * _silu(u_ref[sl, 768:1024])
        s_ref[...] = decay[i] * s_p + r[CHUNK:2 * CHUNK] + g_p[i]


def _mixer_call(body, name, u_main, u_gate, mix_idx, gate_ids, params, scratch):
    b, s, _ = u_main.shape
    tb = min(MIX_TOKENS, s)
    in_specs = [pl.BlockSpec((None, tb, 4 * GROUP_WIDTH), lambda bi, i: (bi, i, mix_idx))]
    args = [u_main]
    for gid in gate_ids:
        in_specs.append(pl.BlockSpec((None, tb, LANES), lambda bi, i, gid=gid: (bi, i, gid)))
        args.append(u_gate)
    for p in params:
        in_specs.append(pl.BlockSpec(p.shape, lambda bi, i, nd=p.ndim: (0,) * nd))
        args.append(p)
    return pl.pallas_call(
        body,
        grid=(b, s // tb),
        in_specs=in_specs,
        out_specs=pl.BlockSpec((None, tb, GROUP_WIDTH), lambda bi, i: (bi, i, 0)),
        out_shape=jax.ShapeDtypeStruct((b, s, GROUP_WIDTH), F32),
        scratch_shapes=scratch,
        compiler_params=pltpu.CompilerParams(dimension_semantics=("arbitrary", "arbitrary"),
                                             vmem_limit_bytes=VMEM_LIMIT),
        name=name,
    )(*args)


def _pad_lanes(v):
    v = v.astype(F32).reshape(1, -1)
    return jnp.pad(v, ((0, 0), (0, LANES - v.shape[1])))


def _row(v):
    return v.astype(F32).reshape(1, -1)


_MLSTM0, _MAMBA0, _HGRN0, _GDN0 = 0, 1032, 2060, 3084
_MAIN_COLS = ((_MLSTM0, _MLSTM0 + 1024), (_MAMBA0, _MAMBA0 + 1024), (_HGRN0, _HGRN0 + 1024),
              (_GDN0, _GDN0 + 1024))
_GATE_COLS = (_MLSTM0 + 1024, _MLSTM0 + 1028, _MAMBA0 + 1024, _GDN0 + 1024, _GDN0 + 1028)

_STATE = pltpu.VMEM((GROUP_WIDTH, GROUP_WIDTH), F32)
_CONV_TAIL = pltpu.VMEM((8, 3 * GROUP_WIDTH), F32)


def kernel(x, ffn1_norm, ffn1_w_gate, ffn1_w_up, ffn1_w_down, mix_norm, w_in, w_out, mlstm_i_bias, mlstm_f_bias, mlstm_norm, mamba_conv_w, mamba_conv_b, mamba_dt_bias, mamba_a_log, mamba_d, mamba_norm, hgrn_lb_logits, hgrn_norm, gdn_conv_w, gdn_a_log, gdn_dt_bias, gdn_norm, ffn2_norm, ffn2_w_gate, ffn2_w_up, ffn2_w_down, final_norm):
    b, s, d = x.shape
    depth = w_in.shape[0]
    x2d = x.reshape(b * s, d)
    lbl = hgrn_lb_logits.astype(F32)
    for l in range(depth):
        wi = w_in[l]
        w_main = jnp.concatenate([wi[:, a:e] for a, e in _MAIN_COLS], axis=1).astype(BF16)
        w_gate = jnp.concatenate(
            [jnp.pad(wi[:, c:c + N_HEADS], ((0, 0), (0, LANES - N_HEADS))) for c in _GATE_COLS], axis=1).astype(BF16)

        x2d = _ffn_call(x2d, _row(ffn1_norm[l]), ffn1_w_gate[l].astype(BF16), ffn1_w_up[l].astype(BF16),
                        ffn1_w_down[l].astype(BF16))
        um, ug = _inproj_call(x2d, _row(mix_norm[l]), w_main, w_gate)
        um = um.reshape(b, s, 4 * 4 * GROUP_WIDTH)
        ug = ug.reshape(b, s, N_GATES * LANES)

        y_a = _mixer_call(
            _mlstm_body, "mlstm", um, ug, 0, (0, 1),
            [_pad_lanes(mlstm_i_bias[l]), _pad_lanes(mlstm_f_bias[l]), _row(mlstm_norm[l])],
            [_STATE, pltpu.VMEM((8, GROUP_WIDTH), F32), pltpu.VMEM((8, LANES), F32)])
        y_b = _mixer_call(
            _mamba_body, "mamba", um, ug, 1, (2,),
            [mamba_conv_w[l].astype(F32), _row(mamba_conv_b[l]), _pad_lanes(mamba_dt_bias[l]),
             _pad_lanes(mamba_a_log[l]), _row(jnp.repeat(mamba_d[l].astype(F32), HEAD_DIM)), _row(mamba_norm[l])],
            [_STATE, _CONV_TAIL])
        y_c = _mixer_call(
            functools.partial(_hgrn_body, layer=l), "hgrn", um, ug, 2, (),
            [lbl, _row(hgrn_norm[l])], [_STATE])
        y_d = _mixer_call(
            _gdn_body, "gdn", um, ug, 3, (3, 4),
            [gdn_conv_w[l].astype(F32), _pad_lanes(gdn_a_log[l]), _pad_lanes(gdn_dt_bias[l]), _row(gdn_norm[l])],
            [pltpu.VMEM((HEAD_DIM, GROUP_WIDTH), F32), _CONV_TAIL])

        ys = [y.reshape(b * s, GROUP_WIDTH) for y in (y_a, y_b, y_c, y_d)]
        x2d = _ffn_call(x2d, _row(ffn2_norm[l]), ffn2_w_gate[l].astype(BF16), ffn2_w_up[l].astype(BF16),
                        ffn2_w_down[l].astype(BF16), mix=(ys, w_out[l].astype(BF16)),
                        final_w=_row(final_norm) if l == depth - 1 else None)
    return x2d.reshape(b, s, d)
```

```python
import functools

import numpy as np
import jax
import jax.numpy as jnp
from jax import lax
from jax.experimental import pallas as pl
from jax.experimental.pallas import tpu as pltpu

F32 = jnp.float32
BF16 = jnp.bfloat16

D_MODEL = 1024
CHUNK = 64
N_HEADS = 4
HEAD_DIM = 64
GROUP_WIDTH = N_HEADS * HEAD_DIM
SSM_STATE = 128
CONV_K = 4
D_FF = 2816
EPS = 1e-6
NEG_BIG = -1e30
LANES = 128
N_GATES = 5

FFN_TOKENS = 512
MXU_DIM = 256
FF_SPLIT = 6 * MXU_DIM
MIX_TOKENS = 512
HGRN_TOKENS = 256
VMEM_LIMIT = 56 * 1024 * 1024


def _iota(shape, dim):
    return lax.broadcasted_iota(jnp.int32, shape, dim)


def _mm(a, b):
    return jnp.dot(a.astype(BF16), b.astype(BF16), preferred_element_type=F32)


def _mm_nt(a, b):
    return lax.dot_general(a.astype(BF16), b.astype(BF16), (((1,), (1,)), ((), ())),
                           preferred_element_type=F32)


def _mm_tn(a, b):
    return lax.dot_general(a.astype(BF16), b.astype(BF16), (((0,), (0,)), ((), ())),
                           preferred_element_type=F32)


def _split2(a):
    hi = a.astype(BF16)
    return hi, (a - hi.astype(F32)).astype(BF16)


def _mm_hilo(a, b_bf16):
    hi, lo = _split2(a)
    return (jnp.dot(hi, b_bf16, preferred_element_type=F32)
            + jnp.dot(lo, b_bf16, preferred_element_type=F32))


def _sigmoid(x):
    return jax.nn.sigmoid(x)


def _silu(x):
    return x * jax.nn.sigmoid(x)


def _softplus(x):
    return jnp.maximum(x, 0.0) + jnp.log1p(jnp.exp(-jnp.abs(x)))


class _Masks:
    def __init__(self):
        row = _iota((GROUP_WIDTH, 1), 0)
        lane = _iota((1, GROUP_WIDTH), 1)
        t = _iota((CHUNK, 1), 0)
        self.row, self.lane = row, lane
        self.lane_head = lane >> 6
        self.block_diag = (row >> 6) == self.lane_head
        self.ones_bd = self.block_diag.astype(BF16)
        s = lane & 63
        self.causal = t >= s
        self.strict = t > s
        self.diag = t == s


def _bd(x, mk):
    x = x.astype(BF16)
    return jnp.where(mk.block_diag, jnp.concatenate([x, x, x, x], axis=0), jnp.zeros((), BF16))


def _unbd(y, mk):
    ym = jnp.where(mk.block_diag, y, 0.0)
    return ym[0:64] + ym[64:128] + ym[128:192] + ym[192:256]


def _expand4(g, mk):
    lh = mk.lane_head
    return jnp.where(lh == 0, g[:, 0:1], jnp.where(lh == 1, g[:, 1:2], jnp.where(lh == 2, g[:, 2:3], g[:, 3:4])))


def _rowvec(ge, mk):
    return jnp.sum(jnp.where(mk.diag, ge, 0.0), axis=0, keepdims=True)


def _chunk_cumsum(x):
    rows = x.shape[0]
    r, c = _iota((rows, rows), 0), _iota((rows, rows), 1)
    tril = (((r >> 6) == (c >> 6)) & (r >= c)).astype(BF16)
    x1 = x.astype(BF16)
    r1 = x - x1.astype(F32)
    x2 = r1.astype(BF16)
    x3 = (r1 - x2.astype(F32)).astype(BF16)
    return (jnp.dot(tril, x1, preferred_element_type=F32) + jnp.dot(tril, x2, preferred_element_type=F32)
            + jnp.dot(tril, x3, preferred_element_type=F32))


def _chunk_cummax(x):
    t = _iota((x.shape[0], 1), 0) & 63
    k = 1
    while k < CHUNK:
        x = jnp.maximum(x, jnp.where(t >= k, pltpu.roll(x, k, 0), NEG_BIG))
        k *= 2
    return x


def _causal_conv(x, prev8, w):
    acc = x * w[CONV_K - 1:CONV_K, :]
    r8 = _iota((8, 1), 0)
    for j in range(1, CONV_K):
        xr = pltpu.roll(x, j, 0)
        top = jnp.where(r8 < j, pltpu.roll(prev8, j, 0), xr[0:8])
        xs = jnp.concatenate([top, xr[8:]], axis=0)
        acc = acc + xs * w[CONV_K - 1 - j:CONV_K - j, :]
    return acc


def _head_rmsnorm(o, mk):
    return o * lax.rsqrt(_mm_hilo(o * o, mk.ones_bd) * (1.0 / HEAD_DIM) + EPS)


def _chunks(n_rows):
    return [slice(c * CHUNK, (c + 1) * CHUNK) for c in range(n_rows // CHUNK)]


def _rms(x, w):
    return x * lax.rsqrt(jnp.mean(x * x, axis=-1, keepdims=True) + EPS) * w


def _ffn_body(*refs, with_mix, with_final):
    it = iter(refs)
    x_ref = next(it)
    if with_mix:
        y_refs = [next(it) for _ in range(4)]
        wo_ref = next(it)
    nw_ref, wg_ref, wu_ref, wd_ref = next(it), next(it), next(it), next(it)
    if with_final:
        fw_ref = next(it)
    o_ref = next(it)

    x = x_ref[...]
    if with_mix:
        y = jnp.concatenate([r[...] for r in y_refs], axis=-1)
        x = x + jnp.dot(y.astype(BF16), wo_ref[...], preferred_element_type=F32)
    xb = _rms(x, nw_ref[...]).astype(BF16)
    acc = jnp.zeros(x.shape, F32)
    for sl in (slice(0, FF_SPLIT), slice(FF_SPLIT, D_FF)):
        g = jnp.dot(xb, wg_ref[:, sl], preferred_element_type=F32)
        u = jnp.dot(xb, wu_ref[:, sl], preferred_element_type=F32)
        h = (_silu(g) * u).astype(BF16)
        acc = acc + jnp.dot(h, wd_ref[sl, :], preferred_element_type=F32)
    out = x + 0.5 * acc
    if with_final:
        out = _rms(out, fw_ref[...])
    o_ref[...] = out


def _const_spec(shape):
    return pl.BlockSpec(shape, lambda i: (0,) * len(shape), pipeline_mode=pl.Buffered(1))


def _ffn_call(x2d, nw, wg, wu, wd, mix=None, final_w=None):
    n = x2d.shape[0]
    tm = min(FFN_TOKENS, n)
    tok = lambda w: pl.BlockSpec((tm, w), lambda i: (i, 0))
    args, specs = [x2d], [tok(D_MODEL)]
    if mix is not None:
        ys, wo = mix
        args += list(ys) + [wo]
        specs += [tok(GROUP_WIDTH)] * 4 + [_const_spec((D_MODEL, D_MODEL))]
    args += [nw, wg, wu, wd]
    specs += [_const_spec((1, D_MODEL)), _const_spec((D_MODEL, D_FF)), _const_spec((D_MODEL, D_FF)),
              _const_spec((D_FF, D_MODEL))]
    if final_w is not None:
        args.append(final_w)
        specs.append(_const_spec((1, D_MODEL)))
    return pl.pallas_call(
        functools.partial(_ffn_body, with_mix=mix is not None, with_final=final_w is not None),
        grid=(n // tm,),
        in_specs=specs,
        out_specs=tok(D_MODEL),
        out_shape=jax.ShapeDtypeStruct((n, D_MODEL), F32),
        compiler_params=pltpu.CompilerParams(dimension_semantics=("arbitrary",),
                                             vmem_limit_bytes=VMEM_LIMIT),
        name="ffn",
    )(*args)


def _inproj_body(x_ref, nw_ref, w_ref, um_ref, ug_ref):
    xb = _rms(x_ref[...], nw_ref[...]).astype(BF16)
    u = jnp.dot(xb, w_ref[...], preferred_element_type=F32)
    for j, start in enumerate(_MAIN_STARTS):
        lo = (start // LANES) * LANES
        width = 4 * GROUP_WIDTH
        if start == lo:
            blk = u[:, lo:lo + width]
        else:
            win = u[:, lo:lo + width + LANES]
            blk = pltpu.roll(win, width + LANES - (start - lo), 1)[:, 0:width]
        um_ref[:, j * width:(j + 1) * width] = blk
    head_lane = _iota((1, LANES), 1) < N_HEADS
    for j, start in enumerate(_GATE_COLS):
        lo = (start // LANES) * LANES
        tile = u[:, lo:lo + LANES]
        if start != lo:
            tile = pltpu.roll(tile, LANES - (start - lo), 1)
        ug_ref[:, j * LANES:(j + 1) * LANES] = jnp.where(head_lane, tile, 0.0)


def _inproj_call(x2d, nw, w_pad):
    n = x2d.shape[0]
    tm = min(FFN_TOKENS, n)
    wm, wg = 4 * 4 * GROUP_WIDTH, N_GATES * LANES
    return pl.pallas_call(
        _inproj_body,
        grid=(n // tm,),
        in_specs=[pl.BlockSpec((tm, D_MODEL), lambda i: (i, 0)), _const_spec((1, D_MODEL)),
                  _const_spec(w_pad.shape)],
        out_specs=[pl.BlockSpec((tm, wm), lambda i: (i, 0)), pl.BlockSpec((tm, wg), lambda i: (i, 0))],
        out_shape=[jax.ShapeDtypeStruct((n, wm), F32), jax.ShapeDtypeStruct((n, wg), F32)],
        compiler_params=pltpu.CompilerParams(dimension_semantics=("arbitrary",),
                                             vmem_limit_bytes=VMEM_LIMIT),
        name="inproj",
    )(x2d, nw, w_pad)


def _mlstm_body(u_ref, gi_ref, gf_ref, ib_ref, fb_ref, nw_ref, y_ref, c_ref, n_ref, m_ref):
    @pl.when(pl.program_id(1) == 0)
    def _():
        c_ref[...] = jnp.zeros_like(c_ref)
        n_ref[...] = jnp.zeros_like(n_ref)
        m_ref[...] = jnp.zeros_like(m_ref)

    mk = _Masks()
    chunks = _chunks(u_ref.shape[0])
    li = gi_ref[...] + ib_ref[...]
    xf = gf_ref[...] + fb_ref[...]
    lf = jnp.minimum(xf, 0.0) - jnp.log1p(jnp.exp(-jnp.abs(xf)))
    bcum = _chunk_cumsum(lf)
    a = li - bcum
    cmax = _chunk_cummax(a)

    m_run = m_ref[0:1, :]
    m_start, m_tot = [], []
    for sl in chunks:
        last = slice(sl.stop - 1, sl.stop)
        m_start.append(m_run)
        m_tot.append(jnp.maximum(m_run, cmax[last]))
        m_run = bcum[last] + m_tot[-1]
    m_ref[0:1, :] = m_run

    qs = [u_ref[sl, 0:256] * (HEAD_DIM ** -0.5) for sl in chunks]
    ks = [u_ref[sl, 256:512] for sl in chunks]
    vs = [u_ref[sl, 512:768] for sl in chunks]
    mx = [jnp.maximum(m0, cmax[sl]) for m0, sl in zip(m_start, chunks)]
    arow = [_rowvec(_expand4(a[sl], mk), mk) for sl in chunks]
    dmat = [jnp.where(mk.causal, jnp.exp(jnp.where(mk.causal, ar - _expand4(m, mk), 0.0)), 0.0)
            for ar, m in zip(arow, mx)]
    p = [_mm_nt(q, _bd(k, mk)) * d for q, k, d in zip(qs, ks, dmat)]
    num_i = [_mm(pp, _bd(v, mk)) for pp, v in zip(p, vs)]
    den_i = [_mm_hilo(pp, mk.ones_bd) for pp in p]
    w_int = [_expand4(jnp.exp(m0 - m), mk) for m0, m in zip(m_start, mx)]
    bound = [_expand4(jnp.exp(-(bcum[sl] + m)), mk) for sl, m in zip(chunks, mx)]
    kw = [k * _expand4(jnp.exp(a[sl] - mt), mk) for k, sl, mt in zip(ks, chunks, m_tot)]
    decay = [_expand4(jnp.exp(m0 - mt), mk) for m0, mt in zip(m_start, m_tot)]
    d_c = [jnp.where(mk.block_diag, _mm_tn(kk, v), 0.0) for kk, v in zip(kw, vs)]
    d_n = [jnp.sum(kk, axis=0, keepdims=True) for kk in kw]

    nw = nw_ref[...]
    for i, sl in enumerate(chunks):
        c_bd = c_ref[...]
        n_row = n_ref[0:1, :]
        num = num_i[i] + w_int[i] * _mm(qs[i], c_bd)
        den = den_i[i] + w_int[i] * _mm_hilo(qs[i] * n_row, mk.ones_bd)
        h = num / jnp.maximum(jnp.abs(den), bound[i])
        y_ref[sl, :] = _head_rmsnorm(h, mk) * nw * _sigmoid(u_ref[sl, 768:1024])
        c_ref[...] = decay[i] * c_bd + d_c[i]
        n_ref[0:1, :] = decay[i] * n_row + d_n[i]


def _mamba_body(u_ref, gdt_ref, cw_ref, cb_ref, dtb_ref, alog_ref, dskip_ref, nw_ref, y_ref, s_ref, prev_ref):
    @pl.when(pl.program_id(1) == 0)
    def _():
        s_ref[...] = jnp.zeros_like(s_ref)
        prev_ref[...] = jnp.zeros_like(prev_ref)

    mk = _Masks()
    rows = u_ref.shape[0]
    chunks = _chunks(rows)
    raw = u_ref[:, 256:1024]
    xbc = _silu(_causal_conv(raw, prev_ref[...], cw_ref[...]) + cb_ref[...])
    prev_ref[...] = raw[rows - 8:rows, :]
    dt = _softplus(gdt_ref[...] + dtb_ref[...])
    cum = _chunk_cumsum(dt * (-jnp.exp(alog_ref[...])))
    grp_b = ((mk.row >> 6) >> 1) == (mk.lane >> 7)
    grp_s = (mk.row >> 7) == (mk.lane_head >> 1)

    xs = [xbc[sl, 0:256] for sl in chunks]
    bm = [xbc[sl, 256:512] for sl in chunks]
    cm = [xbc[sl, 512:768] for sl in chunks]
    cum_e = [_expand4(cum[sl], mk) for sl in chunks]
    seg = [jnp.where(mk.causal, jnp.exp(jnp.where(mk.causal, ce - _rowvec(ce, mk), 0.0)), 0.0) for ce in cum_e]
    dtrow = [_rowvec(_expand4(dt[sl], mk), mk) for sl in chunks]
    p = [_mm_nt(c, jnp.where(grp_b, jnp.concatenate([b.astype(BF16)] * 4, axis=0), jnp.zeros((), BF16))) * sg * dr
         for c, b, sg, dr in zip(cm, bm, seg, dtrow)]
    y_i = [_mm(pp, _bd(x, mk)) + x * dskip_ref[...] for pp, x in zip(p, xs)]
    last = [slice(sl.stop - 1, sl.stop) for sl in chunks]
    xw = [x * _expand4(jnp.exp(cum[ls] - cum[sl]) * dt[sl], mk) for x, sl, ls in zip(xs, chunks, last)]
    d_s = [jnp.where(grp_s, _mm_tn(b, w), 0.0) for b, w in zip(bm, xw)]
    decay = [_expand4(jnp.exp(cum[ls]), mk) for ls in last]
    e_cum = [jnp.exp(ce) for ce in cum_e]

    nw = nw_ref[...]
    for i, sl in enumerate(chunks):
        s_mat = s_ref[...]
        y = y_i[i] + e_cum[i] * _mm(cm[i], s_mat)
        yz = y * _silu(u_ref[sl, 0:256])
        halves = []
        for g in range(2):
            yg = yz[:, g * 128:(g + 1) * 128]
            halves.append(yg * lax.rsqrt(jnp.mean(yg * yg, axis=-1, keepdims=True) + EPS))
        y_ref[sl, :] = jnp.concatenate(halves, axis=-1) * nw
        s_ref[...] = decay[i] * s_mat + d_s[i]


HG_SUB = 16


def _stack_rows(x, mk):
    n = x.shape[0]
    r = _iota((4 * n, 1), 0)
    blk = (r >= n).astype(jnp.int32) + (r >= 2 * n).astype(jnp.int32) + (r >= 3 * n).astype(jnp.int32)
    x = x.astype(BF16)
    return jnp.where(blk == mk.lane_head, jnp.concatenate([x, x, x, x], axis=0), jnp.zeros((), BF16))


def _hgrn_intra(qq, kk, gc, i_in, mk):
    t_sub = _iota((HG_SUB, 1), 0)
    outs = []
    for a in range(CHUNK // HG_SUB):
        lo = a * HG_SUB
        q_a, k_a, g_a, i_a = qq[lo:lo + HG_SUB], kk[lo:lo + HG_SUB], gc[lo:lo + HG_SUB], i_in[lo:lo + HG_SUB]
        prods = []
        for s in range(HG_SUB):
            msk = t_sub >= s
            e = jnp.exp(jnp.where(msk, g_a - g_a[s:s + 1, :], 0.0))
            prods.append(jnp.where(msk, q_a * k_a[s:s + 1, :] * e, 0.0))
        z = _mm_hilo(jnp.concatenate(prods, axis=0), mk.ones_bd)
        o = z[0:HG_SUB] * i_a[0:1, :]
        for s in range(1, HG_SUB):
            o = o + z[s * HG_SUB:(s + 1) * HG_SUB] * i_a[s:s + 1, :]
        if a > 0:
            r = gc[lo - 1:lo, :]
            sc = _mm_nt(q_a * jnp.exp(g_a - r), _stack_rows(kk[0:lo] * jnp.exp(r - gc[0:lo]), mk))
            o = o + _mm(sc, _stack_rows(i_in[0:lo], mk))
        outs.append(o)
    return jnp.concatenate(outs, axis=0)


def _hgrn_body(u_ref, lbl_ref, nw_ref, y_ref, st_ref, *, layer):
    @pl.when(pl.program_id(1) == 0)
    def _():
        st_ref[...] = jnp.zeros_like(st_ref)

    logits = lbl_ref[...]
    e = jnp.exp(logits - jnp.max(logits, axis=0, keepdims=True))
    prob = e / jnp.sum(e, axis=0, keepdims=True)
    lb = jnp.sum(prob[0:layer + 1], axis=0, keepdims=True) - prob[0:1]

    mk = _Masks()
    chunks = _chunks(u_ref.shape[0])
    f = lb + (1.0 - lb) * _sigmoid(u_ref[:, 256:512])
    kk_all = 1.0 - f
    qq_all = _silu(u_ref[:, 0:256])
    gc_all = _chunk_cumsum(jnp.log(f))

    qq = [qq_all[sl] for sl in chunks]
    kk = [kk_all[sl] for sl in chunks]
    gc = [gc_all[sl] for sl in chunks]
    ii = [u_ref[sl, 512:768] for sl in chunks]
    o_i = [_hgrn_intra(q, k, g, i, mk) for q, k, g, i in zip(qq, kk, gc, ii)]
    qe = [q * jnp.exp(g) for q, g in zip(qq, gc)]
    g_last = [g[CHUNK - 1:CHUNK, :] for g in gc]
    d_s = [jnp.where(mk.block_diag, _mm_tn(i, k * jnp.exp(gl - g)), 0.0)
           for i, k, g, gl in zip(ii, kk, gc, g_last)]
    decay = [jnp.exp(gl) for gl in g_last]

    nw = nw_ref[...]
    for i, sl in enumerate(chunks):
        st = st_ref[...]
        o = o_i[i] + _mm_nt(qe[i], st)
        y_ref[sl, :] = _head_rmsnorm(o, mk) * nw * _silu(u_ref[sl, 768:1024])
        st_ref[...] = decay[i] * st + d_s[i]


def _gdn_body(u_ref, gb_ref, ga_ref, cw_ref, alog_ref, dtb_ref, nw_ref, y_ref, s_ref, prev_ref):
    @pl.when(pl.program_id(1) == 0)
    def _():
        s_ref[...] = jnp.zeros_like(s_ref)
        prev_ref[...] = jnp.zeros_like(prev_ref)

    mk = _Masks()
    rows = u_ref.shape[0]
    chunks = _chunks(rows)
    raw = u_ref[:, 0:768]
    qkv = _silu(_causal_conv(raw, prev_ref[...], cw_ref[...]))
    prev_ref[...] = raw[rows - 8:rows, :]
    q, k, v = qkv[:, 0:256], qkv[:, 256:512], qkv[:, 512:768]
    qn = q * lax.rsqrt(_mm_hilo(q * q, mk.ones_bd) + EPS) * (HEAD_DIM ** -0.5)
    kn = k * lax.rsqrt(_mm_hilo(k * k, mk.ones_bd) + EPS)
    gcum = _chunk_cumsum(-jnp.exp(alog_ref[...]) * _softplus(ga_ref[...] + dtb_ref[...]))
    be = _expand4(_sigmoid(gb_ref[...]), mk)
    ge = _expand4(gcum, mk)
    e_g = jnp.exp(ge)
    kb = kn * be
    vb = v * be
    kbg = kb * e_g
    q_dec = qn * e_g
    eye = jnp.where(mk.diag, 1.0, 0.0)

    dec = [jnp.where(mk.causal, jnp.exp(jnp.where(mk.causal, ge[sl] - _rowvec(ge[sl], mk), 0.0)), 0.0)
           for sl in chunks]
    qk = [_mm_nt(jnp.concatenate([kb[sl], qn[sl]], axis=0), _bd(kn[sl], mk)) for sl in chunks]
    a_p = [jnp.where(mk.strict, x[0:CHUNK] * d, 0.0) for x, d in zip(qk, dec)]
    attn = [x[CHUNK:2 * CHUNK] * d for x, d in zip(qk, dec)]

    pw = [_mm(-x, _bd(-x, mk)) for x in a_p]
    t0 = [eye - x for x in a_p]
    for j in range(1, 6):
        r = [_mm(jnp.concatenate([t, m], axis=0), _bd(m, mk)) for t, m in zip(t0, pw)]
        t0 = [t + x[0:CHUNK] for t, x in zip(t0, r)]
        pw = [x[CHUNK:2 * CHUNK] for x in r]
    t_mat = []
    for x, t in zip(a_p, t0):
        a_hi, a_lo = _split2(x)
        t_hi, t_lo = _split2(t)
        r = jnp.dot(jnp.concatenate([a_hi, a_lo], axis=0), _bd(t_hi, mk).astype(BF16), preferred_element_type=F32)
        a_t0 = r[0:CHUNK] + r[CHUNK:2 * CHUNK] + jnp.dot(a_hi, _bd(t_lo, mk).astype(BF16), preferred_element_type=F32)
        t_mat.append(t + _mm(t, _bd(eye - t - a_t0, mk)))

    uw = [_mm(t, jnp.concatenate([_bd(vb[sl], mk), _bd(kbg[sl], mk)], axis=1)) for t, sl in zip(t_mat, chunks)]
    u = [x[:, 0:256] for x in uw]
    w = [x[:, 256:512] for x in uw]
    au = [_mm(at, jnp.concatenate([_bd(uu, mk), _bd(ww, mk)], axis=1)) for at, uu, ww in zip(attn, u, w)]
    o_i = [x[:, 0:256] for x in au]
    q2 = [q_dec[sl] - x[:, 256:512] for sl, x in zip(chunks, au)]
    last = [slice(sl.stop - 1, sl.stop) for sl in chunks]
    k_dec = [kn[sl] * _expand4(jnp.exp(gcum[ls] - gcum[sl]), mk) for sl, ls in zip(chunks, last)]
    fg = [_mm_tn(kd, jnp.concatenate([ww, uu], axis=1)) for kd, ww, uu in zip(k_dec, w, u)]
    f_p = [-_unbd(x[:, 0:256], mk) for x in fg]
    g_p = [_unbd(x[:, 256:512], mk) for x in fg]
    decay = [_expand4(jnp.exp(gcum[ls]), mk) for ls in last]

    nw = nw_ref[...]
    for i, sl in enumerate(chunks):
        s_p = s_ref[...]
        r = _mm(jnp.concatenate([q2[i], f_p[i]], axis=0), _bd(s_p, mk))
        o = o_i[i] + r[0:CHUNK]
        y_ref[sl, :] = _head_rmsnorm(o, mk) * nw * _silu(u_ref[sl, 768:1024])
        s_ref[...] = decay[i] * s_p + r[CHUNK:2 * CHUNK] + g_p[i]


def _mixer_call(body, name, u_main, u_gate, mix_idx, gate_ids, params, scratch, tokens=MIX_TOKENS):
    b, s, _ = u_main.shape
    tb = min(tokens, s)
    in_specs = [pl.BlockSpec((None, tb, 4 * GROUP_WIDTH), lambda bi, i: (bi, i, mix_idx))]
    args = [u_main]
    for gid in gate_ids:
        in_specs.append(pl.BlockSpec((None, tb, LANES), lambda bi, i, gid=gid: (bi, i, gid)))
        args.append(u_gate)
    for p in params:
        in_specs.append(pl.BlockSpec(p.shape, lambda bi, i, nd=p.ndim: (0,) * nd))
        args.append(p)
    return pl.pallas_call(
        body,
        grid=(b, s // tb),
        in_specs=in_specs,
        out_specs=pl.BlockSpec((None, tb, GROUP_WIDTH), lambda bi, i: (bi, i, 0)),
        out_shape=jax.ShapeDtypeStruct((b, s, GROUP_WIDTH), F32),
        scratch_shapes=scratch,
        compiler_params=pltpu.CompilerParams(dimension_semantics=("arbitrary", "arbitrary"),
                                             vmem_limit_bytes=VMEM_LIMIT),
        name=name,
    )(*args)


def _pad_lanes(v):
    v = v.astype(F32).reshape(1, -1)
    return jnp.pad(v, ((0, 0), (0, LANES - v.shape[1])))


def _row(v):
    return v.astype(F32).reshape(1, -1)


_MLSTM0, _MAMBA0, _HGRN0, _GDN0 = 0, 1032, 2060, 3084
_MAIN_COLS = ((_MLSTM0, _MLSTM0 + 1024), (_MAMBA0, _MAMBA0 + 1024), (_HGRN0, _HGRN0 + 1024),
              (_GDN0, _GDN0 + 1024))
_GATE_COLS = (_MLSTM0 + 1024, _MLSTM0 + 1028, _MAMBA0 + 1024, _GDN0 + 1024, _GDN0 + 1028)
_MAIN_STARTS = tuple(a for a, _ in _MAIN_COLS)
_D_IN_PAD = 33 * LANES

_STATE = pltpu.VMEM((GROUP_WIDTH, GROUP_WIDTH), F32)
_CONV_TAIL = pltpu.VMEM((8, 3 * GROUP_WIDTH), F32)


def kernel(x, ffn1_norm, ffn1_w_gate, ffn1_w_up, ffn1_w_down, mix_norm, w_in, w_out, mlstm_i_bias, mlstm_f_bias, mlstm_norm, mamba_conv_w, mamba_conv_b, mamba_dt_bias, mamba_a_log, mamba_d, mamba_norm, hgrn_lb_logits, hgrn_norm, gdn_conv_w, gdn_a_log, gdn_dt_bias, gdn_norm, ffn2_norm, ffn2_w_gate, ffn2_w_up, ffn2_w_down, final_norm):
    b, s, d = x.shape
    depth = w_in.shape[0]
    x2d = x.reshape(b * s, d)
    lbl = hgrn_lb_logits.astype(F32)
    for l in range(depth):
        w_pad = jnp.pad(w_in[l], ((0, 0), (0, _D_IN_PAD - w_in.shape[2]))).astype(BF16)

        x2d = _ffn_call(x2d, _row(ffn1_norm[l]), ffn1_w_gate[l].astype(BF16), ffn1_w_up[l].astype(BF16),
                        ffn1_w_down[l].astype(BF16))
        um, ug = _inproj_call(x2d, _row(mix_norm[l]), w_pad)
        um = um.reshape(b, s, 4 * 4 * GROUP_WIDTH)
        ug = ug.reshape(b, s, N_GATES * LANES)

        y_a = _mixer_call(
            _mlstm_body, "mlstm", um, ug, 0, (0, 1),
            [_pad_lanes(mlstm_i_bias[l]), _pad_lanes(mlstm_f_bias[l]), _row(mlstm_norm[l])],
            [_STATE, pltpu.VMEM((8, GROUP_WIDTH), F32), pltpu.VMEM((8, LANES), F32)])
        y_b = _mixer_call(
            _mamba_body, "mamba", um, ug, 1, (2,),
            [mamba_conv_w[l].astype(F32), _row(mamba_conv_b[l]), _pad_lanes(mamba_dt_bias[l]),
             _pad_lanes(mamba_a_log[l]), _row(jnp.repeat(mamba_d[l].astype(F32), HEAD_DIM)), _row(mamba_norm[l])],
            [_STATE, _CONV_TAIL])
        y_c = _mixer_call(
            functools.partial(_hgrn_body, layer=l), "hgrn", um, ug, 2, (),
            [lbl, _row(hgrn_norm[l])], [_STATE], tokens=HGRN_TOKENS)
        y_d = _mixer_call(
            _gdn_body, "gdn", um, ug, 3, (3, 4),
            [gdn_conv_w[l].astype(F32), _pad_lanes(gdn_a_log[l]), _pad_lanes(gdn_dt_bias[l]), _row(gdn_norm[l])],
            [pltpu.VMEM((HEAD_DIM, GROUP_WIDTH), F32), _CONV_TAIL])

        ys = [y.reshape(b * s, GROUP_WIDTH) for y in (y_a, y_b, y_c, y_d)]
        x2d = _ffn_call(x2d, _row(ffn2_norm[l]), ffn2_w_gate[l].astype(BF16), ffn2_w_up[l].astype(BF16),
                        ffn2_w_down[l].astype(BF16), mix=(ys, w_out[l].astype(BF16)),
                        final_w=_row(final_norm) if l == depth - 1 else None)
    return x2d.reshape(b, s, d)
```

```python
import functools

import numpy as np
import jax
import jax.numpy as jnp
from jax import lax
from jax.experimental import pallas as pl
from jax.experimental.pallas import tpu as pltpu

F32 = jnp.float32
BF16 = jnp.bfloat16

D_MODEL = 1024
CHUNK = 64
N_HEADS = 4
HEAD_DIM = 64
GROUP_WIDTH = N_HEADS * HEAD_DIM
SSM_STATE = 128
CONV_K = 4
D_FF = 2816
EPS = 1e-6
NEG_BIG = -1e30
LANES = 128
N_GATES = 5

FFN_TOKENS = 512
MXU_DIM = 256
FF_SPLIT = 6 * MXU_DIM
N_MIXERS = 4
MIX_TOKENS = 256
VMEM_LIMIT = 56 * 1024 * 1024


def _iota(shape, dim):
    return lax.broadcasted_iota(jnp.int32, shape, dim)


def _mm(a, b):
    return jnp.dot(a.astype(BF16), b.astype(BF16), preferred_element_type=F32)


def _mm_nt(a, b):
    return lax.dot_general(a.astype(BF16), b.astype(BF16), (((1,), (1,)), ((), ())),
                           preferred_element_type=F32)


def _mm_tn(a, b):
    return lax.dot_general(a.astype(BF16), b.astype(BF16), (((0,), (0,)), ((), ())),
                           preferred_element_type=F32)


def _split2(a):
    hi = a.astype(BF16)
    return hi, (a - hi.astype(F32)).astype(BF16)


def _mm_hilo(a, b_bf16):
    hi, lo = _split2(a)
    return (jnp.dot(hi, b_bf16, preferred_element_type=F32)
            + jnp.dot(lo, b_bf16, preferred_element_type=F32))


def _sigmoid(x):
    return jax.nn.sigmoid(x)


def _silu(x):
    return x * jax.nn.sigmoid(x)


def _softplus(x):
    return jnp.maximum(x, 0.0) + jnp.log1p(jnp.exp(-jnp.abs(x)))


class _Masks:
    def __init__(self):
        row = _iota((GROUP_WIDTH, 1), 0)
        lane = _iota((1, GROUP_WIDTH), 1)
        t = _iota((CHUNK, 1), 0)
        self.row, self.lane = row, lane
        self.lane_head = lane >> 6
        self.block_diag = (row >> 6) == self.lane_head
        self.ones_bd = self.block_diag.astype(BF16)
        s = lane & 63
        self.causal = t >= s
        self.strict = t > s
        self.diag = t == s


def _bd(x, mk):
    x = x.astype(BF16)
    return jnp.where(mk.block_diag, jnp.concatenate([x, x, x, x], axis=0), jnp.zeros((), BF16))


def _unbd(y, mk):
    ym = jnp.where(mk.block_diag, y, 0.0)
    return ym[0:64] + ym[64:128] + ym[128:192] + ym[192:256]


def _expand4(g, mk):
    lh = mk.lane_head
    return jnp.where(lh == 0, g[:, 0:1], jnp.where(lh == 1, g[:, 1:2], jnp.where(lh == 2, g[:, 2:3], g[:, 3:4])))


def _rowvec(ge, mk):
    return jnp.sum(jnp.where(mk.diag, ge, 0.0), axis=0, keepdims=True)


def _chunk_cumsum(x):
    rows = x.shape[0]
    r, c = _iota((rows, rows), 0), _iota((rows, rows), 1)
    tril = (((r >> 6) == (c >> 6)) & (r >= c)).astype(BF16)
    x1 = x.astype(BF16)
    r1 = x - x1.astype(F32)
    x2 = r1.astype(BF16)
    x3 = (r1 - x2.astype(F32)).astype(BF16)
    return (jnp.dot(tril, x1, preferred_element_type=F32) + jnp.dot(tril, x2, preferred_element_type=F32)
            + jnp.dot(tril, x3, preferred_element_type=F32))


def _chunk_cummax(x):
    t = _iota((x.shape[0], 1), 0) & 63
    k = 1
    while k < CHUNK:
        x = jnp.maximum(x, jnp.where(t >= k, pltpu.roll(x, k, 0), NEG_BIG))
        k *= 2
    return x


def _causal_conv(x, prev8, w):
    acc = x * w[CONV_K - 1:CONV_K, :]
    r8 = _iota((8, 1), 0)
    for j in range(1, CONV_K):
        xr = pltpu.roll(x, j, 0)
        top = jnp.where(r8 < j, pltpu.roll(prev8, j, 0), xr[0:8])
        xs = jnp.concatenate([top, xr[8:]], axis=0)
        acc = acc + xs * w[CONV_K - 1 - j:CONV_K - j, :]
    return acc


def _head_rmsnorm(o, mk):
    return o * lax.rsqrt(_mm_hilo(o * o, mk.ones_bd) * (1.0 / HEAD_DIM) + EPS)


def _chunks(n_rows):
    return [slice(c * CHUNK, (c + 1) * CHUNK) for c in range(n_rows // CHUNK)]


def _rms(x, w):
    return x * lax.rsqrt(jnp.mean(x * x, axis=-1, keepdims=True) + EPS) * w


def _ffn_body(*refs, with_mix, with_final):
    it = iter(refs)
    x_ref = next(it)
    if with_mix:
        y_ref, wo_ref = next(it), next(it)
    nw_ref, wg_ref, wu_ref, wd_ref = next(it), next(it), next(it), next(it)
    if with_final:
        fw_ref = next(it)
    o_ref = next(it)

    x = x_ref[...]
    if with_mix:
        x = x + jnp.dot(y_ref[...].astype(BF16), wo_ref[...], preferred_element_type=F32)
    xb = _rms(x, nw_ref[...]).astype(BF16)
    acc = jnp.zeros(x.shape, F32)
    for sl in (slice(0, FF_SPLIT), slice(FF_SPLIT, D_FF)):
        g = jnp.dot(xb, wg_ref[:, sl], preferred_element_type=F32)
        u = jnp.dot(xb, wu_ref[:, sl], preferred_element_type=F32)
        h = (_silu(g) * u).astype(BF16)
        acc = acc + jnp.dot(h, wd_ref[sl, :], preferred_element_type=F32)
    out = x + 0.5 * acc
    if with_final:
        out = _rms(out, fw_ref[...])
    o_ref[...] = out


def _const_spec(shape):
    return pl.BlockSpec(shape, lambda i: (0,) * len(shape), pipeline_mode=pl.Buffered(1))


def _ffn_call(x2d, nw, wg, wu, wd, mix=None, final_w=None):
    n = x2d.shape[0]
    tm = min(FFN_TOKENS, n)
    tok = lambda w: pl.BlockSpec((tm, w), lambda i: (i, 0))
    args, specs = [x2d], [tok(D_MODEL)]
    if mix is not None:
        y, wo = mix
        args += [y, wo]
        specs += [tok(y.shape[1]), _const_spec(wo.shape)]
    args += [nw, wg, wu, wd]
    specs += [_const_spec((1, D_MODEL)), _const_spec((D_MODEL, D_FF)), _const_spec((D_MODEL, D_FF)),
              _const_spec((D_FF, D_MODEL))]
    if final_w is not None:
        args.append(final_w)
        specs.append(_const_spec((1, D_MODEL)))
    return pl.pallas_call(
        functools.partial(_ffn_body, with_mix=mix is not None, with_final=final_w is not None),
        grid=(n // tm,),
        in_specs=specs,
        out_specs=tok(D_MODEL),
        out_shape=jax.ShapeDtypeStruct((n, D_MODEL), F32),
        compiler_params=pltpu.CompilerParams(dimension_semantics=("arbitrary",),
                                             vmem_limit_bytes=VMEM_LIMIT),
        name="ffn",
    )(*args)


def _inproj_body(x_ref, nw_ref, w_ref, um_ref, ug_ref):
    xb = _rms(x_ref[...], nw_ref[...]).astype(BF16)
    u = jnp.dot(xb, w_ref[...], preferred_element_type=F32)
    for j, start in enumerate(_MAIN_STARTS):
        lo = (start // LANES) * LANES
        width = 4 * GROUP_WIDTH
        if start == lo:
            blk = u[:, lo:lo + width]
        else:
            win = u[:, lo:lo + width + LANES]
            blk = pltpu.roll(win, width + LANES - (start - lo), 1)[:, 0:width]
        um_ref[:, j * width:(j + 1) * width] = blk
    head_lane = _iota((1, LANES), 1) < N_HEADS
    for j, start in enumerate(_GATE_COLS):
        lo = (start // LANES) * LANES
        tile = u[:, lo:lo + LANES]
        if start != lo:
            tile = pltpu.roll(tile, LANES - (start - lo), 1)
        ug_ref[:, j * LANES:(j + 1) * LANES] = jnp.where(head_lane, tile, 0.0)


def _inproj_call(x2d, nw, w_pad):
    n = x2d.shape[0]
    tm = min(FFN_TOKENS, n)
    wm, wg = 4 * 4 * GROUP_WIDTH, N_GATES * LANES
    return pl.pallas_call(
        _inproj_body,
        grid=(n // tm,),
        in_specs=[pl.BlockSpec((tm, D_MODEL), lambda i: (i, 0)), _const_spec((1, D_MODEL)),
                  _const_spec(w_pad.shape)],
        out_specs=[pl.BlockSpec((tm, wm), lambda i: (i, 0)), pl.BlockSpec((tm, wg), lambda i: (i, 0))],
        out_shape=[jax.ShapeDtypeStruct((n, wm), F32), jax.ShapeDtypeStruct((n, wg), F32)],
        compiler_params=pltpu.CompilerParams(dimension_semantics=("arbitrary",),
                                             vmem_limit_bytes=VMEM_LIMIT),
        name="inproj",
    )(x2d, nw, w_pad)


def _mlstm_steps(mk, u_ref, gi_ref, gf_ref, ib_ref, fb_ref, nw_ref, y_ref, c_ref, n_ref, m_ref):
    chunks = _chunks(u_ref.shape[0])
    li = gi_ref[...] + ib_ref[...]
    xf = gf_ref[...] + fb_ref[...]
    lf = jnp.minimum(xf, 0.0) - jnp.log1p(jnp.exp(-jnp.abs(xf)))
    bcum = _chunk_cumsum(lf)
    a = li - bcum
    yield
    cmax = _chunk_cummax(a)
    yield

    m_run = m_ref[0:1, :]
    m_start, m_tot = [], []
    for sl in chunks:
        last = slice(sl.stop - 1, sl.stop)
        m_start.append(m_run)
        m_tot.append(jnp.maximum(m_run, cmax[last]))
        m_run = bcum[last] + m_tot[-1]
    m_ref[0:1, :] = m_run
    yield

    qs = [u_ref[sl, 0:256] * (HEAD_DIM ** -0.5) for sl in chunks]
    ks = [u_ref[sl, 256:512] for sl in chunks]
    vs = [u_ref[sl, 512:768] for sl in chunks]
    mx = [jnp.maximum(m0, cmax[sl]) for m0, sl in zip(m_start, chunks)]
    arow = [_rowvec(_expand4(a[sl], mk), mk) for sl in chunks]
    yield
    dmat = [jnp.where(mk.causal, jnp.exp(jnp.where(mk.causal, ar - _expand4(m, mk), 0.0)), 0.0)
            for ar, m in zip(arow, mx)]
    yield
    p = [_mm_nt(q, _bd(k, mk)) * d for q, k, d in zip(qs, ks, dmat)]
    yield
    num_i = [_mm(pp, _bd(v, mk)) for pp, v in zip(p, vs)]
    yield
    den_i = [_mm_hilo(pp, mk.ones_bd) for pp in p]
    yield
    w_int = [_expand4(jnp.exp(m0 - m), mk) for m0, m in zip(m_start, mx)]
    bound = [_expand4(jnp.exp(-(bcum[sl] + m)), mk) for sl, m in zip(chunks, mx)]
    yield
    kw = [k * _expand4(jnp.exp(a[sl] - mt), mk) for k, sl, mt in zip(ks, chunks, m_tot)]
    decay = [_expand4(jnp.exp(m0 - mt), mk) for m0, mt in zip(m_start, m_tot)]
    yield
    d_c = [jnp.where(mk.block_diag, _mm_tn(kk, v), 0.0) for kk, v in zip(kw, vs)]
    d_n = [jnp.sum(kk, axis=0, keepdims=True) for kk in kw]
    yield

    nw = nw_ref[...]
    for i, sl in enumerate(chunks):
        c_bd = c_ref[...]
        n_row = n_ref[0:1, :]
        num = num_i[i] + w_int[i] * _mm(qs[i], c_bd)
        den = den_i[i] + w_int[i] * _mm_hilo(qs[i] * n_row, mk.ones_bd)
        h = num / jnp.maximum(jnp.abs(den), bound[i])
        y_ref[sl, :] = _head_rmsnorm(h, mk) * nw * _sigmoid(u_ref[sl, 768:1024])
        c_ref[...] = decay[i] * c_bd + d_c[i]
        n_ref[0:1, :] = decay[i] * n_row + d_n[i]
        yield


def _mamba_steps(mk, u_ref, gdt_ref, cw_ref, cb_ref, dtb_ref, alog_ref, dskip_ref, nw_ref, y_ref, s_ref, prev_ref):
    rows = u_ref.shape[0]
    chunks = _chunks(rows)
    raw = u_ref[:, 256:1024]
    xbc = _silu(_causal_conv(raw, prev_ref[...], cw_ref[...]) + cb_ref[...])
    prev_ref[...] = raw[rows - 8:rows, :]
    yield
    dt = _softplus(gdt_ref[...] + dtb_ref[...])
    cum = _chunk_cumsum(dt * (-jnp.exp(alog_ref[...])))
    grp_b = ((mk.row >> 6) >> 1) == (mk.lane >> 7)
    grp_s = (mk.row >> 7) == (mk.lane_head >> 1)
    yield

    xs = [xbc[sl, 0:256] for sl in chunks]
    bm = [xbc[sl, 256:512] for sl in chunks]
    cm = [xbc[sl, 512:768] for sl in chunks]
    cum_e = [_expand4(cum[sl], mk) for sl in chunks]
    yield
    seg = [jnp.where(mk.causal, jnp.exp(jnp.where(mk.causal, ce - _rowvec(ce, mk), 0.0)), 0.0) for ce in cum_e]
    yield
    dtrow = [_rowvec(_expand4(dt[sl], mk), mk) for sl in chunks]
    yield
    p = [_mm_nt(c, jnp.where(grp_b, jnp.concatenate([b.astype(BF16)] * 4, axis=0), jnp.zeros((), BF16))) * sg * dr
         for c, b, sg, dr in zip(cm, bm, seg, dtrow)]
    yield
    y_i = [_mm(pp, _bd(x, mk)) + x * dskip_ref[...] for pp, x in zip(p, xs)]
    yield
    last = [slice(sl.stop - 1, sl.stop) for sl in chunks]
    xw = [x * _expand4(jnp.exp(cum[ls] - cum[sl]) * dt[sl], mk) for x, sl, ls in zip(xs, chunks, last)]
    yield
    d_s = [jnp.where(grp_s, _mm_tn(b, w), 0.0) for b, w in zip(bm, xw)]
    yield
    decay = [_expand4(jnp.exp(cum[ls]), mk) for ls in last]
    e_cum = [jnp.exp(ce) for ce in cum_e]
    yield

    nw = nw_ref[...]
    for i, sl in enumerate(chunks):
        s_mat = s_ref[...]
        y = y_i[i] + e_cum[i] * _mm(cm[i], s_mat)
        yz = y * _silu(u_ref[sl, 0:256])
        halves = []
        for g in range(2):
            yg = yz[:, g * 128:(g + 1) * 128]
            halves.append(yg * lax.rsqrt(jnp.mean(yg * yg, axis=-1, keepdims=True) + EPS))
        y_ref[sl, :] = jnp.concatenate(halves, axis=-1) * nw
        s_ref[...] = decay[i] * s_mat + d_s[i]
        yield


HG_SUB = 16


def _stack_rows(x, mk):
    n = x.shape[0]
    r = _iota((4 * n, 1), 0)
    blk = (r >= n).astype(jnp.int32) + (r >= 2 * n).astype(jnp.int32) + (r >= 3 * n).astype(jnp.int32)
    x = x.astype(BF16)
    return jnp.where(blk == mk.lane_head, jnp.concatenate([x, x, x, x], axis=0), jnp.zeros((), BF16))


def _hgrn_intra_sub(qq, kk, gc, i_in, a, mk):
    t_sub = _iota((HG_SUB, 1), 0)
    lo = a * HG_SUB
    q_a, k_a, g_a, i_a = qq[lo:lo + HG_SUB], kk[lo:lo + HG_SUB], gc[lo:lo + HG_SUB], i_in[lo:lo + HG_SUB]
    prods = []
    for s in range(HG_SUB):
        msk = t_sub >= s
        e = jnp.exp(jnp.where(msk, g_a - g_a[s:s + 1, :], 0.0))
        prods.append(jnp.where(msk, q_a * k_a[s:s + 1, :] * e, 0.0))
    z = _mm_hilo(jnp.concatenate(prods, axis=0), mk.ones_bd)
    o = z[0:HG_SUB] * i_a[0:1, :]
    for s in range(1, HG_SUB):
        o = o + z[s * HG_SUB:(s + 1) * HG_SUB] * i_a[s:s + 1, :]
    if a > 0:
        r = gc[lo - 1:lo, :]
        sc = _mm_nt(q_a * jnp.exp(g_a - r), _stack_rows(kk[0:lo] * jnp.exp(r - gc[0:lo]), mk))
        o = o + _mm(sc, _stack_rows(i_in[0:lo], mk))
    return o


def _hgrn_steps(mk, u_ref, lbl_ref, nw_ref, y_ref, st_ref, *, layer):
    logits = lbl_ref[...]
    e = jnp.exp(logits - jnp.max(logits, axis=0, keepdims=True))
    prob = e / jnp.sum(e, axis=0, keepdims=True)
    lb = jnp.sum(prob[0:layer + 1], axis=0, keepdims=True) - prob[0:1]

    chunks = _chunks(u_ref.shape[0])
    f = lb + (1.0 - lb) * _sigmoid(u_ref[:, 256:512])
    kk_all = 1.0 - f
    qq_all = _silu(u_ref[:, 0:256])
    yield
    gc_all = _chunk_cumsum(jnp.log(f))
    yield

    qq = [qq_all[sl] for sl in chunks]
    kk = [kk_all[sl] for sl in chunks]
    gc = [gc_all[sl] for sl in chunks]
    ii = [u_ref[sl, 512:768] for sl in chunks]
    subs = [[] for _ in chunks]
    for a in range(CHUNK // HG_SUB):
        for c in range(len(chunks)):
            subs[c].append(_hgrn_intra_sub(qq[c], kk[c], gc[c], ii[c], a, mk))
            yield
    o_i = [jnp.concatenate(s, axis=0) for s in subs]
    qe = [q * jnp.exp(g) for q, g in zip(qq, gc)]
    g_last = [g[CHUNK - 1:CHUNK, :] for g in gc]
    yield
    d_s = [jnp.where(mk.block_diag, _mm_tn(i, k * jnp.exp(gl - g)), 0.0)
           for i, k, g, gl in zip(ii, kk, gc, g_last)]
    decay = [jnp.exp(gl) for gl in g_last]
    yield

    nw = nw_ref[...]
    for i, sl in enumerate(chunks):
        st = st_ref[...]
        o = o_i[i] + _mm_nt(qe[i], st)
        y_ref[sl, :] = _head_rmsnorm(o, mk) * nw * _silu(u_ref[sl, 768:1024])
        st_ref[...] = decay[i] * st + d_s[i]
        yield


def _gdn_steps(mk, u_ref, gb_ref, ga_ref, cw_ref, alog_ref, dtb_ref, nw_ref, y_ref, s_ref, prev_ref):
    rows = u_ref.shape[0]
    chunks = _chunks(rows)
    raw = u_ref[:, 0:768]
    qkv = _silu(_causal_conv(raw, prev_ref[...], cw_ref[...]))
    prev_ref[...] = raw[rows - 8:rows, :]
    yield
    q, k, v = qkv[:, 0:256], qkv[:, 256:512], qkv[:, 512:768]
    qn = q * lax.rsqrt(_mm_hilo(q * q, mk.ones_bd) + EPS) * (HEAD_DIM ** -0.5)
    kn = k * lax.rsqrt(_mm_hilo(k * k, mk.ones_bd) + EPS)
    yield
    gcum = _chunk_cumsum(-jnp.exp(alog_ref[...]) * _softplus(ga_ref[...] + dtb_ref[...]))
    be = _expand4(_sigmoid(gb_ref[...]), mk)
    ge = _expand4(gcum, mk)
    yield
    e_g = jnp.exp(ge)
    kb = kn * be
    vb = v * be
    kbg = kb * e_g
    q_dec = qn * e_g
    eye = jnp.where(mk.diag, 1.0, 0.0)
    yield

    dec = [jnp.where(mk.causal, jnp.exp(jnp.where(mk.causal, ge[sl] - _rowvec(ge[sl], mk), 0.0)), 0.0)
           for sl in chunks]
    yield
    qk = [_mm_nt(jnp.concatenate([kb[sl], qn[sl]], axis=0), _bd(kn[sl], mk)) for sl in chunks]
    yield
    a_p = [jnp.where(mk.strict, x[0:CHUNK] * d, 0.0) for x, d in zip(qk, dec)]
    attn = [x[CHUNK:2 * CHUNK] * d for x, d in zip(qk, dec)]

    pw = [_mm(-x, _bd(-x, mk)) for x in a_p]
    t0 = [eye - x for x in a_p]
    yield
    for j in range(1, 6):
        r = [_mm(jnp.concatenate([t, m], axis=0), _bd(m, mk)) for t, m in zip(t0, pw)]
        t0 = [t + x[0:CHUNK] for t, x in zip(t0, r)]
        pw = [x[CHUNK:2 * CHUNK] for x in r]
        yield
    a_t0 = []
    for x, t in zip(a_p, t0):
        a_hi, a_lo = _split2(x)
        t_hi, t_lo = _split2(t)
        r = jnp.dot(jnp.concatenate([a_hi, a_lo], axis=0), _bd(t_hi, mk).astype(BF16), preferred_element_type=F32)
        a_t0.append(r[0:CHUNK] + r[CHUNK:2 * CHUNK]
                    + jnp.dot(a_hi, _bd(t_lo, mk).astype(BF16), preferred_element_type=F32))
    yield
    t_mat = [t + _mm(t, _bd(eye - t - at, mk)) for t, at in zip(t0, a_t0)]
    yield

    uw = [_mm(t, jnp.concatenate([_bd(vb[sl], mk), _bd(kbg[sl], mk)], axis=1)) for t, sl in zip(t_mat, chunks)]
    u = [x[:, 0:256] for x in uw]
    w = [x[:, 256:512] for x in uw]
    yield
    au = [_mm(at, jnp.concatenate([_bd(uu, mk), _bd(ww, mk)], axis=1)) for at, uu, ww in zip(attn, u, w)]
    o_i = [x[:, 0:256] for x in au]
    q2 = [q_dec[sl] - x[:, 256:512] for sl, x in zip(chunks, au)]
    yield
    last = [slice(sl.stop - 1, sl.stop) for sl in chunks]
    k_dec = [kn[sl] * _expand4(jnp.exp(gcum[ls] - gcum[sl]), mk) for sl, ls in zip(chunks, last)]
    fg = [_mm_tn(kd, jnp.concatenate([ww, uu], axis=1)) for kd, ww, uu in zip(k_dec, w, u)]
    yield
    f_p = [-_unbd(x[:, 0:256], mk) for x in fg]
    g_p = [_unbd(x[:, 256:512], mk) for x in fg]
    decay = [_expand4(jnp.exp(gcum[ls]), mk) for ls in last]
    yield

    nw = nw_ref[...]
    for i, sl in enumerate(chunks):
        s_p = s_ref[...]
        r = _mm(jnp.concatenate([q2[i], f_p[i]], axis=0), _bd(s_p, mk))
        o = o_i[i] + r[0:CHUNK]
        y_ref[sl, :] = _head_rmsnorm(o, mk) * nw * _silu(u_ref[sl, 768:1024])
        s_ref[...] = decay[i] * s_p + r[CHUNK:2 * CHUNK] + g_p[i]
        yield


_DONE = object()


def _mixers_body(um_ref, ug_ref, ib, fb, nw_a, cw_b, cb_b, dtb_b, alog_b, dskip_b, nw_b, lbl, nw_c,
                 cw_d, alog_d, dtb_d, nw_d, y_ref,
                 c_ref, n_ref, m_ref, sb_ref, prevb_ref, st_ref, sd_ref, prevd_ref, *, layer):
    states = (c_ref, n_ref, m_ref, sb_ref, prevb_ref, st_ref, sd_ref, prevd_ref)

    @pl.when(pl.program_id(1) == 0)
    def _():
        for r in states:
            r[...] = jnp.zeros_like(r)

    mk = _Masks()
    win = lambda j: um_ref.at[:, pl.ds(j * 4 * GROUP_WIDTH, 4 * GROUP_WIDTH)]
    gate = lambda j: ug_ref.at[:, pl.ds(j * LANES, LANES)]
    out = lambda j: y_ref.at[:, pl.ds(j * GROUP_WIDTH, GROUP_WIDTH)]
    steps = [
        _gdn_steps(mk, win(3), gate(3), gate(4), cw_d, alog_d, dtb_d, nw_d, out(3), sd_ref, prevd_ref),
        _hgrn_steps(mk, win(2), lbl, nw_c, out(2), st_ref, layer=layer),
        _mlstm_steps(mk, win(0), gate(0), gate(1), ib, fb, nw_a, out(0), c_ref, n_ref, m_ref),
        _mamba_steps(mk, win(1), gate(2), cw_b, cb_b, dtb_b, alog_b, dskip_b, nw_b, out(1), sb_ref, prevb_ref),
    ]
    while steps:
        steps = [g for g in steps if next(g, _DONE) is not _DONE]


def _mixers_call(u_main, u_gate, params, layer):
    b, s, _ = u_main.shape
    tb = min(MIX_TOKENS, s)
    blk = lambda w: pl.BlockSpec((None, tb, w), lambda bi, i: (bi, i, 0))
    in_specs = [blk(u_main.shape[2]), blk(u_gate.shape[2])]
    in_specs += [pl.BlockSpec(p.shape, lambda bi, i, nd=p.ndim: (0,) * nd) for p in params]
    state = pltpu.VMEM((GROUP_WIDTH, GROUP_WIDTH), F32)
    conv_tail = pltpu.VMEM((8, 3 * GROUP_WIDTH), F32)
    scratch = [state, pltpu.VMEM((8, GROUP_WIDTH), F32), pltpu.VMEM((8, LANES), F32),
               state, conv_tail,
               state,
               pltpu.VMEM((HEAD_DIM, GROUP_WIDTH), F32), conv_tail]
    return pl.pallas_call(
        functools.partial(_mixers_body, layer=layer),
        grid=(b, s // tb),
        in_specs=in_specs,
        out_specs=blk(N_MIXERS * GROUP_WIDTH),
        out_shape=jax.ShapeDtypeStruct((b, s, N_MIXERS * GROUP_WIDTH), F32),
        scratch_shapes=scratch,
        compiler_params=pltpu.CompilerParams(dimension_semantics=("arbitrary", "arbitrary"),
                                             vmem_limit_bytes=VMEM_LIMIT),
        name="mixers",
    )(u_main, u_gate, *params)


def _pad_lanes(v):
    v = v.astype(F32).reshape(1, -1)
    return jnp.pad(v, ((0, 0), (0, LANES - v.shape[1])))


def _row(v):
    return v.astype(F32).reshape(1, -1)


_MLSTM0, _MAMBA0, _HGRN0, _GDN0 = 0, 1032, 2060, 3084
_MAIN_COLS = ((_MLSTM0, _MLSTM0 + 1024), (_MAMBA0, _MAMBA0 + 1024), (_HGRN0, _HGRN0 + 1024),
              (_GDN0, _GDN0 + 1024))
_GATE_COLS = (_MLSTM0 + 1024, _MLSTM0 + 1028, _MAMBA0 + 1024, _GDN0 + 1024, _GDN0 + 1028)
_MAIN_STARTS = tuple(a for a, _ in _MAIN_COLS)
_D_IN_PAD = 33 * LANES


def kernel(x, ffn1_norm, ffn1_w_gate, ffn1_w_up, ffn1_w_down, mix_norm, w_in, w_out, mlstm_i_bias, mlstm_f_bias, mlstm_norm, mamba_conv_w, mamba_conv_b, mamba_dt_bias, mamba_a_log, mamba_d, mamba_norm, hgrn_lb_logits, hgrn_norm, gdn_conv_w, gdn_a_log, gdn_dt_bias, gdn_norm, ffn2_norm, ffn2_w_gate, ffn2_w_up, ffn2_w_down, final_norm):
    b, s, d = x.shape
    depth = w_in.shape[0]
    x2d = x.reshape(b * s, d)
    lbl = hgrn_lb_logits.astype(F32)
    for l in range(depth):
        w_pad = jnp.pad(w_in[l], ((0, 0), (0, _D_IN_PAD - w_in.shape[2]))).astype(BF16)

        x2d = _ffn_call(x2d, _row(ffn1_norm[l]), ffn1_w_gate[l].astype(BF16), ffn1_w_up[l].astype(BF16),
                        ffn1_w_down[l].astype(BF16))
        um, ug = _inproj_call(x2d, _row(mix_norm[l]), w_pad)
        um = um.reshape(b, s, 4 * 4 * GROUP_WIDTH)
        ug = ug.reshape(b, s, N_GATES * LANES)

        params = [
            _pad_lanes(mlstm_i_bias[l]), _pad_lanes(mlstm_f_bias[l]), _row(mlstm_norm[l]),
            mamba_conv_w[l].astype(F32), _row(mamba_conv_b[l]), _pad_lanes(mamba_dt_bias[l]),
            _pad_lanes(mamba_a_log[l]), _row(jnp.repeat(mamba_d[l].astype(F32), HEAD_DIM)), _row(mamba_norm[l]),
            lbl, _row(hgrn_norm[l]),
            gdn_conv_w[l].astype(F32), _pad_lanes(gdn_a_log[l]), _pad_lanes(gdn_dt_bias[l]), _row(gdn_norm[l])]
        y = _mixers_call(um, ug, params, l).reshape(b * s, N_MIXERS * GROUP_WIDTH)
        x2d = _ffn_call(x2d, _row(ffn2_norm[l]), ffn2_w_gate[l].astype(BF16), ffn2_w_up[l].astype(BF16),
                        ffn2_w_down[l].astype(BF16), mix=(y, w_out[l].astype(BF16)),
                        final_w=_row(final_norm) if l == depth - 1 else None)
    return x2d.reshape(b, s, d)
```

```python
import functools

import numpy as np
import jax
import jax.numpy as jnp
from jax import lax
from jax.experimental import pallas as pl
from jax.experimental.pallas import tpu as pltpu

F32 = jnp.float32
BF16 = jnp.bfloat16

D_MODEL = 1024
CHUNK = 64
N_HEADS = 4
HEAD_DIM = 64
GROUP_WIDTH = N_HEADS * HEAD_DIM
SSM_STATE = 128
CONV_K = 4
D_FF = 2816
EPS = 1e-6
NEG_BIG = -1e30
LANES = 128
N_GATES = 5

FFN_TOKENS = 512
CAST_ROWS = 256
MXU_DIM = 256
FF_SPLIT = 6 * MXU_DIM
N_MIXERS = 4
MIX_TOKENS = 256
VMEM_LIMIT = 56 * 1024 * 1024


def _iota(shape, dim):
    return lax.broadcasted_iota(jnp.int32, shape, dim)


def _mm(a, b):
    return jnp.dot(a.astype(BF16), b.astype(BF16), preferred_element_type=F32)


def _mm_nt(a, b):
    return lax.dot_general(a.astype(BF16), b.astype(BF16), (((1,), (1,)), ((), ())),
                           preferred_element_type=F32)


def _mm_tn(a, b):
    return lax.dot_general(a.astype(BF16), b.astype(BF16), (((0,), (0,)), ((), ())),
                           preferred_element_type=F32)


def _split2(a):
    hi = a.astype(BF16)
    return hi, (a - hi.astype(F32)).astype(BF16)


def _mm_hilo(a, b_bf16):
    hi, lo = _split2(a)
    return (jnp.dot(hi, b_bf16, preferred_element_type=F32)
            + jnp.dot(lo, b_bf16, preferred_element_type=F32))


def _sigmoid(x):
    return jax.nn.sigmoid(x)


def _silu(x):
    return x * jax.nn.sigmoid(x)


def _softplus(x):
    return jnp.maximum(x, 0.0) + jnp.log1p(jnp.exp(-jnp.abs(x)))


class _Masks:
    def __init__(self):
        row = _iota((GROUP_WIDTH, 1), 0)
        lane = _iota((1, GROUP_WIDTH), 1)
        t = _iota((CHUNK, 1), 0)
        self.row, self.lane = row, lane
        self.lane_head = lane >> 6
        block_diag = (row >> 6) == self.lane_head
        self.ones_bd = block_diag.astype(BF16)
        self.bd_f32 = block_diag.astype(F32)
        s = lane & 63
        self.causal_f = (t >= s).astype(F32)
        self.strict_f = (t > s).astype(F32)
        self.diag = t == s


def _bd(x, mk):
    x = x.astype(BF16)
    return jnp.concatenate([x, x, x, x], axis=0) * mk.ones_bd


def _unbd(y, mk):
    ym = y * mk.bd_f32
    return ym[0:64] + ym[64:128] + ym[128:192] + ym[192:256]


def _decay(arg, mask_f):
    return jnp.exp(jnp.minimum(arg, 0.0)) * mask_f


def _expand4(g, mk):
    first = _iota((1, LANES), 1) < HEAD_DIM
    return jnp.concatenate([jnp.where(first, g[:, 0:1], g[:, 1:2]), jnp.where(first, g[:, 2:3], g[:, 3:4])], axis=1)


def _rowvec(ge, mk):
    return jnp.sum(jnp.where(mk.diag, ge, 0.0), axis=0, keepdims=True)


def _chunk_cumsum(x):
    rows = x.shape[0]
    r, c = _iota((rows, rows), 0), _iota((rows, rows), 1)
    tril = (((r >> 6) == (c >> 6)) & (r >= c)).astype(BF16)
    x1 = x.astype(BF16)
    r1 = x - x1.astype(F32)
    x2 = r1.astype(BF16)
    x3 = (r1 - x2.astype(F32)).astype(BF16)
    return (jnp.dot(tril, x1, preferred_element_type=F32) + jnp.dot(tril, x2, preferred_element_type=F32)
            + jnp.dot(tril, x3, preferred_element_type=F32))


def _chunk_cummax(x):
    t = _iota((x.shape[0], 1), 0) & 63
    k = 1
    while k < CHUNK:
        x = jnp.maximum(x, jnp.where(t >= k, pltpu.roll(x, k, 0), NEG_BIG))
        k *= 2
    return x


def _causal_conv(x, prev8, w):
    acc = x * w[CONV_K - 1:CONV_K, :]
    r8 = _iota((8, 1), 0)
    for j in range(1, CONV_K):
        xr = pltpu.roll(x, j, 0)
        top = jnp.where(r8 < j, pltpu.roll(prev8, j, 0), xr[0:8])
        xs = jnp.concatenate([top, xr[8:]], axis=0)
        acc = acc + xs * w[CONV_K - 1 - j:CONV_K - j, :]
    return acc


def _head_rmsnorm(o, mk):
    return o * lax.rsqrt(_mm_hilo(o * o, mk.ones_bd) * (1.0 / HEAD_DIM) + EPS)


def _chunks(n_rows):
    return [slice(c * CHUNK, (c + 1) * CHUNK) for c in range(n_rows // CHUNK)]


def _rms(x, w):
    return x * lax.rsqrt(jnp.mean(x * x, axis=-1, keepdims=True) + EPS) * w


def _ffn_body(*refs, with_mix, with_final):
    it = iter(refs)
    x_ref = next(it)
    if with_mix:
        y_ref, wo_ref = next(it), next(it)
    nw_ref, wg_ref, wu_ref, wd_ref = next(it), next(it), next(it), next(it)
    if with_final:
        fw_ref = next(it)
    o_ref = next(it)

    x = x_ref[...]
    if with_mix:
        x = x + jnp.dot(y_ref[...].astype(BF16), wo_ref[...], preferred_element_type=F32)
    xb = _rms(x, nw_ref[...]).astype(BF16)
    acc = jnp.zeros(x.shape, F32)
    for sl in (slice(0, FF_SPLIT), slice(FF_SPLIT, D_FF)):
        g = jnp.dot(xb, wg_ref[:, sl], preferred_element_type=F32)
        u = jnp.dot(xb, wu_ref[:, sl], preferred_element_type=F32)
        h = (_silu(g) * u).astype(BF16)
        acc = acc + jnp.dot(h, wd_ref[sl, :], preferred_element_type=F32)
    out = x + 0.5 * acc
    if with_final:
        out = _rms(out, fw_ref[...])
    o_ref[...] = out


def _const_spec(shape):
    return pl.BlockSpec(shape, lambda *_: (0,) * len(shape), pipeline_mode=pl.Buffered(1))


def _layer_spec(a, layer):
    return pl.BlockSpec((None,) + a.shape[1:], lambda *_: (layer, 0, 0), pipeline_mode=pl.Buffered(1))


def _cast_body(w_ref, o_ref):
    cols = w_ref.shape[1]
    if o_ref.shape[1] != cols:
        o_ref[...] = jnp.zeros(o_ref.shape, BF16)
    o_ref[:, 0:cols] = w_ref[...].astype(BF16)


def _to_bf16(w, cols_out=None):
    depth, rows, cols = w.shape
    cols_out = cols_out or cols
    return pl.pallas_call(
        _cast_body,
        grid=(depth, rows // CAST_ROWS),
        in_specs=[pl.BlockSpec((None, CAST_ROWS, cols), lambda l, i: (l, i, 0))],
        out_specs=pl.BlockSpec((None, CAST_ROWS, cols_out), lambda l, i: (l, i, 0)),
        out_shape=jax.ShapeDtypeStruct((depth, rows, cols_out), BF16),
        compiler_params=pltpu.CompilerParams(dimension_semantics=("arbitrary", "arbitrary")),
        name="cast",
    )(w)


def _ffn_call(x2d, nw, wg, wu, wd, layer, mix=None, final_w=None):
    n = x2d.shape[0]
    tm = min(FFN_TOKENS, n)
    tok = lambda w: pl.BlockSpec((tm, w), lambda i: (i, 0))
    args, specs = [x2d], [tok(D_MODEL)]
    if mix is not None:
        y, wo = mix
        args += [y, wo]
        specs += [tok(y.shape[1]), _layer_spec(wo, layer)]
    args += [nw, wg, wu, wd]
    specs += [_layer_spec(a, layer) for a in (nw, wg, wu, wd)]
    if final_w is not None:
        args.append(final_w)
        specs.append(_const_spec((1, D_MODEL)))
    return pl.pallas_call(
        functools.partial(_ffn_body, with_mix=mix is not None, with_final=final_w is not None),
        grid=(n // tm,),
        in_specs=specs,
        out_specs=tok(D_MODEL),
        out_shape=jax.ShapeDtypeStruct((n, D_MODEL), F32),
        compiler_params=pltpu.CompilerParams(dimension_semantics=("arbitrary",),
                                             vmem_limit_bytes=VMEM_LIMIT),
        name="ffn",
    )(*args)


def _inproj_body(x_ref, nw_ref, w_ref, um_ref, ug_ref):
    xb = _rms(x_ref[...], nw_ref[...]).astype(BF16)
    u = jnp.dot(xb, w_ref[...], preferred_element_type=F32)
    for j, start in enumerate(_MAIN_STARTS):
        lo = (start // LANES) * LANES
        width = 4 * GROUP_WIDTH
        if start == lo:
            blk = u[:, lo:lo + width]
        else:
            win = u[:, lo:lo + width + LANES]
            blk = pltpu.roll(win, width + LANES - (start - lo), 1)[:, 0:width]
        um_ref[:, j * width:(j + 1) * width] = blk
    head_lane = _iota((1, LANES), 1) < N_HEADS
    for j, start in enumerate(_GATE_COLS):
        lo = (start // LANES) * LANES
        tile = u[:, lo:lo + LANES]
        if start != lo:
            tile = pltpu.roll(tile, LANES - (start - lo), 1)
        ug_ref[:, j * LANES:(j + 1) * LANES] = jnp.where(head_lane, tile, 0.0)


def _inproj_call(x2d, nw, w_pad, layer):
    n = x2d.shape[0]
    tm = min(FFN_TOKENS, n)
    wm, wg = 4 * 4 * GROUP_WIDTH, N_GATES * LANES
    return pl.pallas_call(
        _inproj_body,
        grid=(n // tm,),
        in_specs=[pl.BlockSpec((tm, D_MODEL), lambda i: (i, 0)), _layer_spec(nw, layer), _layer_spec(w_pad, layer)],
        out_specs=[pl.BlockSpec((tm, wm), lambda i: (i, 0)), pl.BlockSpec((tm, wg), lambda i: (i, 0))],
        out_shape=[jax.ShapeDtypeStruct((n, wm), F32), jax.ShapeDtypeStruct((n, wg), F32)],
        compiler_params=pltpu.CompilerParams(dimension_semantics=("arbitrary",),
                                             vmem_limit_bytes=VMEM_LIMIT),
        name="inproj",
    )(x2d, nw, w_pad)


def _mlstm_steps(mk, u_ref, gi_ref, gf_ref, ib_ref, fb_ref, nw_ref, y_ref, c_ref, n_ref, m_ref):
    chunks = _chunks(u_ref.shape[0])
    li = gi_ref[...] + ib_ref[...]
    xf = gf_ref[...] + fb_ref[...]
    lf = jnp.minimum(xf, 0.0) - jnp.log1p(jnp.exp(-jnp.abs(xf)))
    bcum = _chunk_cumsum(lf)
    a = li - bcum
    yield
    cmax = _chunk_cummax(a)
    yield

    m_run = m_ref[0:1, :]
    m_start, m_tot = [], []
    for sl in chunks:
        last = slice(sl.stop - 1, sl.stop)
        m_start.append(m_run)
        m_tot.append(jnp.maximum(m_run, cmax[last]))
        m_run = bcum[last] + m_tot[-1]
    m_ref[0:1, :] = m_run
    yield

    qs = [u_ref[sl, 0:256] * (HEAD_DIM ** -0.5) for sl in chunks]
    ks = [u_ref[sl, 256:512] for sl in chunks]
    vs = [u_ref[sl, 512:768] for sl in chunks]
    mx = [jnp.maximum(m0, cmax[sl]) for m0, sl in zip(m_start, chunks)]
    arow = [_rowvec(_expand4(a[sl], mk), mk) for sl in chunks]
    yield
    dmat = [_decay(ar - _expand4(m, mk), mk.causal_f) for ar, m in zip(arow, mx)]
    yield
    p = [_mm_nt(q, _bd(k, mk)) * d for q, k, d in zip(qs, ks, dmat)]
    yield
    num_i = [_mm(pp, _bd(v, mk)) for pp, v in zip(p, vs)]
    yield
    den_i = [_mm_hilo(pp, mk.ones_bd) for pp in p]
    yield
    w_int = [_expand4(jnp.exp(m0 - m), mk) for m0, m in zip(m_start, mx)]
    bound = [_expand4(jnp.exp(-(bcum[sl] + m)), mk) for sl, m in zip(chunks, mx)]
    yield
    kw = [k * _expand4(jnp.exp(a[sl] - mt), mk) for k, sl, mt in zip(ks, chunks, m_tot)]
    decay = [_expand4(jnp.exp(m0 - mt), mk) for m0, mt in zip(m_start, m_tot)]
    yield
    d_c = [_mm_tn(kk, v) * mk.bd_f32 for kk, v in zip(kw, vs)]
    d_n = [jnp.sum(kk, axis=0, keepdims=True) for kk in kw]
    yield

    nw = nw_ref[...]
    for i, sl in enumerate(chunks):
        c_bd = c_ref[...]
        n_row = n_ref[0:1, :]
        num = num_i[i] + w_int[i] * _mm(qs[i], c_bd)
        den = den_i[i] + w_int[i] * _mm_hilo(qs[i] * n_row, mk.ones_bd)
        h = num / jnp.maximum(jnp.abs(den), bound[i])
        y_ref[sl, :] = _head_rmsnorm(h, mk) * nw * _sigmoid(u_ref[sl, 768:1024])
        c_ref[...] = decay[i] * c_bd + d_c[i]
        n_ref[0:1, :] = decay[i] * n_row + d_n[i]
        yield


def _mamba_steps(mk, u_ref, gdt_ref, cw_ref, cb_ref, dtb_ref, alog_ref, dskip_ref, nw_ref, y_ref, s_ref, prev_ref):
    rows = u_ref.shape[0]
    chunks = _chunks(rows)
    raw = u_ref[:, 256:1024]
    xbc = _silu(_causal_conv(raw, prev_ref[...], cw_ref[...]) + cb_ref[...])
    prev_ref[...] = raw[rows - 8:rows, :]
    yield
    dt = _softplus(gdt_ref[...] + dtb_ref[...])
    cum = _chunk_cumsum(dt * (-jnp.exp(alog_ref[...])))
    grp_b = (((mk.row >> 6) >> 1) == (mk.lane >> 7)).astype(BF16)
    grp_s = ((mk.row >> 7) == (mk.lane_head >> 1)).astype(F32)
    yield

    xs = [xbc[sl, 0:256] for sl in chunks]
    bm = [xbc[sl, 256:512] for sl in chunks]
    cm = [xbc[sl, 512:768] for sl in chunks]
    cum_e = [_expand4(cum[sl], mk) for sl in chunks]
    yield
    seg = [_decay(ce - _rowvec(ce, mk), mk.causal_f) for ce in cum_e]
    yield
    dtrow = [_rowvec(_expand4(dt[sl], mk), mk) for sl in chunks]
    yield
    p = [_mm_nt(c, jnp.concatenate([b.astype(BF16)] * 4, axis=0) * grp_b) * sg * dr
         for c, b, sg, dr in zip(cm, bm, seg, dtrow)]
    yield
    y_i = [_mm(pp, _bd(x, mk)) + x * dskip_ref[...] for pp, x in zip(p, xs)]
    yield
    last = [slice(sl.stop - 1, sl.stop) for sl in chunks]
    xw = [x * _expand4(jnp.exp(cum[ls] - cum[sl]) * dt[sl], mk) for x, sl, ls in zip(xs, chunks, last)]
    yield
    d_s = [_mm_tn(b, w) * grp_s for b, w in zip(bm, xw)]
    yield
    decay = [_expand4(jnp.exp(cum[ls]), mk) for ls in last]
    e_cum = [jnp.exp(ce) for ce in cum_e]
    yield

    nw = nw_ref[...]
    for i, sl in enumerate(chunks):
        s_mat = s_ref[...]
        y = y_i[i] + e_cum[i] * _mm(cm[i], s_mat)
        yz = y * _silu(u_ref[sl, 0:256])
        halves = []
        for g in range(2):
            yg = yz[:, g * 128:(g + 1) * 128]
            halves.append(yg * lax.rsqrt(jnp.mean(yg * yg, axis=-1, keepdims=True) + EPS))
        y_ref[sl, :] = jnp.concatenate(halves, axis=-1) * nw
        s_ref[...] = decay[i] * s_mat + d_s[i]
        yield


HG_SUB = 16


def _stack_rows(x, mk):
    n = x.shape[0]
    r = _iota((4 * n, 1), 0)
    blk = (r >= n).astype(jnp.int32) + (r >= 2 * n).astype(jnp.int32) + (r >= 3 * n).astype(jnp.int32)
    x = x.astype(BF16)
    return jnp.where(blk == mk.lane_head, jnp.concatenate([x, x, x, x], axis=0), jnp.zeros((), BF16))


def _hgrn_intra_sub(qq, kk, gc, i_in, a, mk):
    half = HG_SUB // 2
    t_half = _iota((half, 1), 0)
    lo = a * HG_SUB
    q_a, k_a, g_a, i_a = qq[lo:lo + HG_SUB], kk[lo:lo + HG_SUB], gc[lo:lo + HG_SUB], i_in[lo:lo + HG_SUB]
    q_t, q_b, g_t, g_b = q_a[0:half], q_a[half:], g_a[0:half], g_a[half:]
    tops, bots = [], []
    for s in range(HG_SUB):
        k_s, g_s = k_a[s:s + 1, :], g_a[s:s + 1, :]
        if s < half:
            tops.append(q_t * k_s * _decay(g_t - g_s, (t_half >= s).astype(F32)))
            bots.append(q_b * k_s * jnp.exp(g_b - g_s))
        else:
            bots.append(q_b * k_s * _decay(g_b - g_s, (t_half >= s - half).astype(F32)))
    z = _mm_hilo(jnp.concatenate(tops + bots, axis=0), mk.ones_bd)
    z_t, z_b = z[0:half * half], z[half * half:]
    o_t = z_t[0:half] * i_a[0:1, :]
    for s in range(1, half):
        o_t = o_t + z_t[s * half:(s + 1) * half] * i_a[s:s + 1, :]
    o_b = z_b[0:half] * i_a[0:1, :]
    for s in range(1, HG_SUB):
        o_b = o_b + z_b[s * half:(s + 1) * half] * i_a[s:s + 1, :]
    o = jnp.concatenate([o_t, o_b], axis=0)
    if a > 0:
        r = gc[lo - 1:lo, :]
        sc = _mm_nt(q_a * jnp.exp(g_a - r), _stack_rows(kk[0:lo] * jnp.exp(r - gc[0:lo]), mk))
        o = o + _mm(sc, _stack_rows(i_in[0:lo], mk))
    return o


def _hgrn_steps(mk, u_ref, lbl_ref, nw_ref, y_ref, st_ref, *, layer):
    logits = lbl_ref[...]
    e = jnp.exp(logits - jnp.max(logits, axis=0, keepdims=True))
    prob = e / jnp.sum(e, axis=0, keepdims=True)
    lb = jnp.sum(prob[0:layer + 1], axis=0, keepdims=True) - prob[0:1]

    chunks = _chunks(u_ref.shape[0])
    f = lb + (1.0 - lb) * _sigmoid(u_ref[:, 256:512])
    kk_all = 1.0 - f
    qq_all = _silu(u_ref[:, 0:256])
    yield
    gc_all = _chunk_cumsum(jnp.log(f))
    yield

    qq = [qq_all[sl] for sl in chunks]
    kk = [kk_all[sl] for sl in chunks]
    gc = [gc_all[sl] for sl in chunks]
    ii = [u_ref[sl, 512:768] for sl in chunks]
    subs = [[] for _ in chunks]
    for a in range(CHUNK // HG_SUB):
        for c in range(len(chunks)):
            subs[c].append(_hgrn_intra_sub(qq[c], kk[c], gc[c], ii[c], a, mk))
            yield
    o_i = [jnp.concatenate(s, axis=0) for s in subs]
    qe = [q * jnp.exp(g) for q, g in zip(qq, gc)]
    g_last = [g[CHUNK - 1:CHUNK, :] for g in gc]
    yield
    d_s = [_mm_tn(i, k * jnp.exp(gl - g)) * mk.bd_f32
           for i, k, g, gl in zip(ii, kk, gc, g_last)]
    decay = [jnp.exp(gl) for gl in g_last]
    yield

    nw = nw_ref[...]
    for i, sl in enumerate(chunks):
        st = st_ref[...]
        o = o_i[i] + _mm_nt(qe[i], st)
        y_ref[sl, :] = _head_rmsnorm(o, mk) * nw * _silu(u_ref[sl, 768:1024])
        st_ref[...] = decay[i] * st + d_s[i]
        yield


def _gdn_steps(mk, u_ref, gb_ref, ga_ref, cw_ref, alog_ref, dtb_ref, nw_ref, y_ref, s_ref, prev_ref):
    rows = u_ref.shape[0]
    chunks = _chunks(rows)
    raw = u_ref[:, 0:768]
    qkv = _silu(_causal_conv(raw, prev_ref[...], cw_ref[...]))
    prev_ref[...] = raw[rows - 8:rows, :]
    yield
    q, k, v = qkv[:, 0:256], qkv[:, 256:512], qkv[:, 512:768]
    qn = q * lax.rsqrt(_mm_hilo(q * q, mk.ones_bd) + EPS) * (HEAD_DIM ** -0.5)
    kn = k * lax.rsqrt(_mm_hilo(k * k, mk.ones_bd) + EPS)
    yield
    gcum = _chunk_cumsum(-jnp.exp(alog_ref[...]) * _softplus(ga_ref[...] + dtb_ref[...]))
    be = _expand4(_sigmoid(gb_ref[...]), mk)
    ge = _expand4(gcum, mk)
    yield
    e_g = jnp.exp(ge)
    kb = kn * be
    vb = v * be
    kbg = kb * e_g
    q_dec = qn * e_g
    eye = jnp.where(mk.diag, 1.0, 0.0)
    yield

    e_dec = [jnp.exp(jnp.minimum(ge[sl] - _rowvec(ge[sl], mk), 0.0)) for sl in chunks]
    yield
    qk = [_mm_nt(jnp.concatenate([kb[sl], qn[sl]], axis=0), _bd(kn[sl], mk)) for sl in chunks]
    yield
    a_p = [x[0:CHUNK] * (e * mk.strict_f) for x, e in zip(qk, e_dec)]
    attn = [x[CHUNK:2 * CHUNK] * (e * mk.causal_f) for x, e in zip(qk, e_dec)]

    pw = [_mm(-x, _bd(-x, mk)) for x in a_p]
    t0 = [eye - x for x in a_p]
    yield
    for j in range(1, 6):
        r = [_mm(jnp.concatenate([t, m], axis=0), _bd(m, mk)) for t, m in zip(t0, pw)]
        t0 = [t + x[0:CHUNK] for t, x in zip(t0, r)]
        pw = [x[CHUNK:2 * CHUNK] for x in r]
        yield
    a_t0 = []
    for x, t in zip(a_p, t0):
        a_hi, a_lo = _split2(x)
        t_hi, t_lo = _split2(t)
        r = jnp.dot(jnp.concatenate([a_hi, a_lo], axis=0), _bd(t_hi, mk).astype(BF16), preferred_element_type=F32)
        a_t0.append(r[0:CHUNK] + r[CHUNK:2 * CHUNK]
                    + jnp.dot(a_hi, _bd(t_lo, mk).astype(BF16), preferred_element_type=F32))
    yield
    t_mat = [t + _mm(t, _bd(eye - t - at, mk)) for t, at in zip(t0, a_t0)]
    yield

    uw = [_mm(t, jnp.concatenate([_bd(vb[sl], mk), _bd(kbg[sl], mk)], axis=1)) for t, sl in zip(t_mat, chunks)]
    u = [x[:, 0:256] for x in uw]
    w = [x[:, 256:512] for x in uw]
    yield
    au = [_mm(at, jnp.concatenate([_bd(uu, mk), _bd(ww, mk)], axis=1)) for at, uu, ww in zip(attn, u, w)]
    o_i = [x[:, 0:256] for x in au]
    q2 = [q_dec[sl] - x[:, 256:512] for sl, x in zip(chunks, au)]
    yield
    last = [slice(sl.stop - 1, sl.stop) for sl in chunks]
    k_dec = [kn[sl] * _expand4(jnp.exp(gcum[ls] - gcum[sl]), mk) for sl, ls in zip(chunks, last)]
    fg = [_mm_tn(kd, jnp.concatenate([ww, uu], axis=1)) for kd, ww, uu in zip(k_dec, w, u)]
    yield
    f_p = [-_unbd(x[:, 0:256], mk) for x in fg]
    g_p = [_unbd(x[:, 256:512], mk) for x in fg]
    decay = [_expand4(jnp.exp(gcum[ls]), mk) for ls in last]
    yield

    nw = nw_ref[...]
    for i, sl in enumerate(chunks):
        s_p = s_ref[...]
        r = _mm(jnp.concatenate([q2[i], f_p[i]], axis=0), _bd(s_p, mk))
        o = o_i[i] + r[0:CHUNK]
        y_ref[sl, :] = _head_rmsnorm(o, mk) * nw * _silu(u_ref[sl, 768:1024])
        s_ref[...] = decay[i] * s_p + r[CHUNK:2 * CHUNK] + g_p[i]
        yield


_DONE = object()


def _mixers_body(um_ref, ug_ref, ib, fb, nw_a, cw_b, cb_b, dtb_b, alog_b, dskip_b, nw_b, lbl, nw_c,
                 cw_d, alog_d, dtb_d, nw_d, y_ref,
                 c_ref, n_ref, m_ref, sb_ref, prevb_ref, st_ref, sd_ref, prevd_ref, *, layer):
    states = (c_ref, n_ref, m_ref, sb_ref, prevb_ref, st_ref, sd_ref, prevd_ref)

    @pl.when(pl.program_id(1) == 0)
    def _():
        for r in states:
            r[...] = jnp.zeros_like(r)

    mk = _Masks()
    win = lambda j: um_ref.at[:, pl.ds(j * 4 * GROUP_WIDTH, 4 * GROUP_WIDTH)]
    gate = lambda j: ug_ref.at[:, pl.ds(j * LANES, LANES)]
    out = lambda j: y_ref.at[:, pl.ds(j * GROUP_WIDTH, GROUP_WIDTH)]
    steps = [
        _gdn_steps(mk, win(3), gate(3), gate(4), cw_d, alog_d, dtb_d, nw_d, out(3), sd_ref, prevd_ref),
        _hgrn_steps(mk, win(2), lbl, nw_c, out(2), st_ref, layer=layer),
        _mlstm_steps(mk, win(0), gate(0), gate(1), ib, fb, nw_a, out(0), c_ref, n_ref, m_ref),
        _mamba_steps(mk, win(1), gate(2), cw_b, cb_b, dtb_b, alog_b, dskip_b, nw_b, out(1), sb_ref, prevb_ref),
    ]
    while steps:
        steps = [g for g in steps if next(g, _DONE) is not _DONE]


def _mixers_call(u_main, u_gate, params, layer):
    b, s, _ = u_main.shape
    tb = min(MIX_TOKENS, s)
    blk = lambda w: pl.BlockSpec((None, tb, w), lambda bi, i: (bi, i, 0))
    in_specs = [blk(u_main.shape[2]), blk(u_gate.shape[2])]
    in_specs += [pl.BlockSpec((None,) + p.shape[1:], lambda bi, i: (layer, 0, 0)) if per_layer
                 else pl.BlockSpec(p.shape, lambda bi, i, nd=p.ndim: (0,) * nd) for p, per_layer in params]
    state = pltpu.VMEM((GROUP_WIDTH, GROUP_WIDTH), F32)
    conv_tail = pltpu.VMEM((8, 3 * GROUP_WIDTH), F32)
    scratch = [state, pltpu.VMEM((8, GROUP_WIDTH), F32), pltpu.VMEM((8, LANES), F32),
               state, conv_tail,
               state,
               pltpu.VMEM((HEAD_DIM, GROUP_WIDTH), F32), conv_tail]
    return pl.pallas_call(
        functools.partial(_mixers_body, layer=layer),
        grid=(b, s // tb),
        in_specs=in_specs,
        out_specs=blk(N_MIXERS * GROUP_WIDTH),
        out_shape=jax.ShapeDtypeStruct((b, s, N_MIXERS * GROUP_WIDTH), F32),
        scratch_shapes=scratch,
        compiler_params=pltpu.CompilerParams(dimension_semantics=("arbitrary", "arbitrary"),
                                             vmem_limit_bytes=VMEM_LIMIT),
        name="mixers",
    )(u_main, u_gate, *[p for p, _ in params])


def _rows(v):
    return v.astype(F32)[:, None, :]


def _head_rows(v):
    return jnp.pad(_rows(v), ((0, 0), (0, 0), (0, LANES - v.shape[1])))


def _mixer_params(mlstm_i_bias, mlstm_f_bias, mlstm_norm, mamba_conv_w, mamba_conv_b, mamba_dt_bias, mamba_a_log,
                  mamba_d, mamba_norm, hgrn_lb_logits, hgrn_norm, gdn_conv_w, gdn_a_log, gdn_dt_bias, gdn_norm):
    per_layer = [
        _head_rows(mlstm_i_bias), _head_rows(mlstm_f_bias), _rows(mlstm_norm),
        mamba_conv_w.astype(F32), _rows(mamba_conv_b), _head_rows(mamba_dt_bias), _head_rows(mamba_a_log),
        _rows(jnp.repeat(mamba_d, HEAD_DIM, axis=1)), _rows(mamba_norm)]
    hgrn = [(hgrn_lb_logits.astype(F32), False), (_rows(hgrn_norm), True)]
    gdn = [gdn_conv_w.astype(F32), _head_rows(gdn_a_log), _head_rows(gdn_dt_bias), _rows(gdn_norm)]
    return [(p, True) for p in per_layer] + hgrn + [(p, True) for p in gdn]


_MLSTM0, _MAMBA0, _HGRN0, _GDN0 = 0, 1032, 2060, 3084
_MAIN_COLS = ((_MLSTM0, _MLSTM0 + 1024), (_MAMBA0, _MAMBA0 + 1024), (_HGRN0, _HGRN0 + 1024),
              (_GDN0, _GDN0 + 1024))
_GATE_COLS = (_MLSTM0 + 1024, _MLSTM0 + 1028, _MAMBA0 + 1024, _GDN0 + 1024, _GDN0 + 1028)
_MAIN_STARTS = tuple(a for a, _ in _MAIN_COLS)
_D_IN_PAD = 33 * LANES


def kernel(x, ffn1_norm, ffn1_w_gate, ffn1_w_up, ffn1_w_down, mix_norm, w_in, w_out, mlstm_i_bias, mlstm_f_bias, mlstm_norm, mamba_conv_w, mamba_conv_b, mamba_dt_bias, mamba_a_log, mamba_d, mamba_norm, hgrn_lb_logits, hgrn_norm, gdn_conv_w, gdn_a_log, gdn_dt_bias, gdn_norm, ffn2_norm, ffn2_w_gate, ffn2_w_up, ffn2_w_down, final_norm):
    b, s, d = x.shape
    depth = w_in.shape[0]
    x2d = x.reshape(b * s, d)
    ffn1 = (_rows(ffn1_norm), _to_bf16(ffn1_w_gate), _to_bf16(ffn1_w_up), _to_bf16(ffn1_w_down))
    ffn2 = (_rows(ffn2_norm), _to_bf16(ffn2_w_gate), _to_bf16(ffn2_w_up), _to_bf16(ffn2_w_down))
    w_pad = _to_bf16(w_in, _D_IN_PAD)
    w_o = _to_bf16(w_out)
    mix_nw = _rows(mix_norm)
    params = _mixer_params(mlstm_i_bias, mlstm_f_bias, mlstm_norm, mamba_conv_w, mamba_conv_b, mamba_dt_bias,
                           mamba_a_log, mamba_d, mamba_norm, hgrn_lb_logits, hgrn_norm, gdn_conv_w, gdn_a_log,
                           gdn_dt_bias, gdn_norm)
    for l in range(depth):
        x2d = _ffn_call(x2d, *ffn1, l)
        um, ug = _inproj_call(x2d, mix_nw, w_pad, l)
        um = um.reshape(b, s, 4 * 4 * GROUP_WIDTH)
        ug = ug.reshape(b, s, N_GATES * LANES)
        y = _mixers_call(um, ug, params, l).reshape(b * s, N_MIXERS * GROUP_WIDTH)
        x2d = _ffn_call(x2d, *ffn2, l, mix=(y, w_o),
                        final_w=final_norm.astype(F32).reshape(1, d) if l == depth - 1 else None)
    return x2d.reshape(b, s, d)
```

```python
import functools

import numpy as np
import jax
import jax.numpy as jnp
from jax import lax
from jax.experimental import pallas as pl
from jax.experimental.pallas import tpu as pltpu

F32 = jnp.float32
BF16 = jnp.bfloat16

D_MODEL = 1024
CHUNK = 64
N_HEADS = 4
HEAD_DIM = 64
GROUP_WIDTH = N_HEADS * HEAD_DIM
SSM_STATE = 128
CONV_K = 4
D_FF = 2816
EPS = 1e-6
NEG_BIG = -1e30
LOG2E = 1.4426950408889634
LANES = 128
N_GATES = 5

FFN_TOKENS = 512
CAST_STEPS = 2
MXU_DIM = 256
FF_SPLIT = 6 * MXU_DIM
FF_STEP = 2 * MXU_DIM
N_MIXERS = 4
MIX_TOKENS = 256
VMEM_LIMIT = 56 * 1024 * 1024


def _iota(shape, dim):
    return lax.broadcasted_iota(jnp.int32, shape, dim)


def _mm(a, b):
    return jnp.dot(a.astype(BF16), b.astype(BF16), preferred_element_type=F32)


def _mm_nt(a, b):
    return lax.dot_general(a.astype(BF16), b.astype(BF16), (((1,), (1,)), ((), ())),
                           preferred_element_type=F32)


def _mm_tn(a, b):
    return lax.dot_general(a.astype(BF16), b.astype(BF16), (((0,), (0,)), ((), ())),
                           preferred_element_type=F32)


def _split2(a):
    hi = a.astype(BF16)
    return hi, (a - hi.astype(F32)).astype(BF16)


def _mm_hilo(a, b_bf16):
    hi, lo = _split2(a)
    return (jnp.dot(hi, b_bf16, preferred_element_type=F32)
            + jnp.dot(lo, b_bf16, preferred_element_type=F32))


def _sigmoid(x):
    return 0.5 * jnp.tanh(0.5 * x) + 0.5


def _silu(x):
    h = 0.5 * x
    return h * jnp.tanh(h) + h


def _softplus(x):
    return jnp.maximum(x, 0.0) + jnp.log1p(jnp.exp(-jnp.abs(x)))


class _Masks:
    def __init__(self):
        row = _iota((GROUP_WIDTH, 1), 0)
        lane = _iota((1, GROUP_WIDTH), 1)
        t = _iota((CHUNK, 1), 0)
        self.row, self.lane = row, lane
        self.lane_head = lane >> 6
        block_diag = (row >> 6) == self.lane_head
        self.ones_bd = block_diag.astype(BF16)
        self.bd_f32 = block_diag.astype(F32)
        s = lane & 63
        self.causal_f = (t >= s).astype(F32)
        self.strict_f = (t > s).astype(F32)
        self.diag = t == s


def _bd(x, mk):
    x = x.astype(BF16)
    return jnp.concatenate([x, x, x, x], axis=0) * mk.ones_bd


def _unbd(y, mk):
    ym = y * mk.bd_f32
    return ym[0:64] + ym[64:128] + ym[128:192] + ym[192:256]


def _decay(arg, mask_f, exp=jnp.exp):
    return exp(jnp.minimum(arg, 0.0)) * mask_f


def _expand4(g, mk):
    first = _iota((1, LANES), 1) < HEAD_DIM
    return jnp.concatenate([jnp.where(first, g[:, 0:1], g[:, 1:2]), jnp.where(first, g[:, 2:3], g[:, 3:4])], axis=1)


def _rowvec(ge, mk):
    return jnp.sum(jnp.where(mk.diag, ge, 0.0), axis=0, keepdims=True)


def _chunk_cumsum(x):
    rows = x.shape[0]
    r, c = _iota((rows, rows), 0), _iota((rows, rows), 1)
    tril = (((r >> 6) == (c >> 6)) & (r >= c)).astype(BF16)
    x1 = x.astype(BF16)
    r1 = x - x1.astype(F32)
    x2 = r1.astype(BF16)
    x3 = (r1 - x2.astype(F32)).astype(BF16)
    return (jnp.dot(tril, x1, preferred_element_type=F32) + jnp.dot(tril, x2, preferred_element_type=F32)
            + jnp.dot(tril, x3, preferred_element_type=F32))


def _chunk_cummax(x):
    t = _iota((x.shape[0], 1), 0) & 63
    k = 1
    while k < CHUNK:
        x = jnp.maximum(x, jnp.where(t >= k, pltpu.roll(x, k, 0), NEG_BIG))
        k *= 2
    return x


def _causal_conv(x, prev8, w):
    acc = x * w[CONV_K - 1:CONV_K, :]
    r8 = _iota((8, 1), 0)
    for j in range(1, CONV_K):
        xr = pltpu.roll(x, j, 0)
        top = jnp.where(r8 < j, pltpu.roll(prev8, j, 0), xr[0:8])
        xs = jnp.concatenate([top, xr[8:]], axis=0)
        acc = acc + xs * w[CONV_K - 1 - j:CONV_K - j, :]
    return acc


def _head_rmsnorm(o, mk):
    return o * lax.rsqrt(_mm_hilo(o * o, mk.ones_bd) * (1.0 / HEAD_DIM) + EPS)


def _chunks(n_rows):
    return [slice(c * CHUNK, (c + 1) * CHUNK) for c in range(n_rows // CHUNK)]


def _rms(x, w):
    return x * lax.rsqrt(jnp.mean(x * x, axis=-1, keepdims=True) + EPS) * w


def _ffn_body(x_ref, nw_ref, wg_ref, wu_ref, wd_ref, o_ref):
    x = x_ref[...]
    xb = _rms(x, nw_ref[...]).astype(BF16)
    acc = jnp.zeros(x.shape, F32)
    for sl in (slice(0, FF_SPLIT), slice(FF_SPLIT, D_FF)):
        g = jnp.dot(xb, wg_ref[:, sl], preferred_element_type=F32)
        u = jnp.dot(xb, wu_ref[:, sl], preferred_element_type=F32)
        h = (_silu(g) * u).astype(BF16)
        acc = acc + jnp.dot(h, wd_ref[sl, :], preferred_element_type=F32)
    o_ref[...] = x + 0.5 * acc


def _const_spec(shape):
    return pl.BlockSpec(shape, lambda *_: (0,) * len(shape), pipeline_mode=pl.Buffered(1))


def _layer_spec(a, layer):
    return pl.BlockSpec((None,) + a.shape[1:], lambda *_: (layer, 0, 0), pipeline_mode=pl.Buffered(1))


def _cast_body(w_ref, o_ref):
    cols = w_ref.shape[1]
    if o_ref.shape[1] != cols:
        o_ref[...] = jnp.zeros(o_ref.shape, BF16)
    o_ref[:, 0:cols] = w_ref[...].astype(BF16)


def _to_bf16(w, cols_out=None):
    depth, rows, cols = w.shape
    cols_out = cols_out or cols
    tr = rows // CAST_STEPS
    return pl.pallas_call(
        _cast_body,
        grid=(depth, CAST_STEPS),
        in_specs=[pl.BlockSpec((None, tr, cols), lambda l, i: (l, i, 0))],
        out_specs=pl.BlockSpec((None, tr, cols_out), lambda l, i: (l, i, 0)),
        out_shape=jax.ShapeDtypeStruct((depth, rows, cols_out), BF16),
        compiler_params=pltpu.CompilerParams(dimension_semantics=("arbitrary", "arbitrary"),
                                             vmem_limit_bytes=VMEM_LIMIT),
        name="cast",
    )(w)


def _ffn_call(x2d, nw, wg, wu, wd, layer):
    n = x2d.shape[0]
    tm = min(FFN_TOKENS, n)
    tok = pl.BlockSpec((tm, D_MODEL), lambda i: (i, 0))
    return pl.pallas_call(
        _ffn_body,
        grid=(n // tm,),
        in_specs=[tok] + [_layer_spec(a, layer) for a in (nw, wg, wu, wd)],
        out_specs=tok,
        out_shape=jax.ShapeDtypeStruct((n, D_MODEL), F32),
        compiler_params=pltpu.CompilerParams(dimension_semantics=("arbitrary",),
                                             vmem_limit_bytes=VMEM_LIMIT),
        name="ffn",
    )(x2d, nw, wg, wu, wd)


def _inproj_body(x_ref, nw_ref, w_ref, um_ref, ug_ref):
    xb = _rms(x_ref[...], nw_ref[...]).astype(BF16)
    u = jnp.dot(xb, w_ref[...], preferred_element_type=F32)
    for j, start in enumerate(_MAIN_STARTS):
        lo = (start // LANES) * LANES
        width = 4 * GROUP_WIDTH
        if start == lo:
            blk = u[:, lo:lo + width]
        else:
            win = u[:, lo:lo + width + LANES]
            blk = pltpu.roll(win, width + LANES - (start - lo), 1)[:, 0:width]
        um_ref[:, j * width:(j + 1) * width] = blk
    head_lane = _iota((1, LANES), 1) < N_HEADS
    for j, start in enumerate(_GATE_COLS):
        lo = (start // LANES) * LANES
        tile = u[:, lo:lo + LANES]
        if start != lo:
            tile = pltpu.roll(tile, LANES - (start - lo), 1)
        ug_ref[:, j * LANES:(j + 1) * LANES] = jnp.where(head_lane, tile, 0.0)


def _inproj_call(x2d, nw, w_pad, layer):
    n = x2d.shape[0]
    tm = min(FFN_TOKENS, n)
    wm, wg = 4 * 4 * GROUP_WIDTH, N_GATES * LANES
    return pl.pallas_call(
        _inproj_body,
        grid=(n // tm,),
        in_specs=[pl.BlockSpec((tm, D_MODEL), lambda i: (i, 0)), _layer_spec(nw, layer), _layer_spec(w_pad, layer)],
        out_specs=[pl.BlockSpec((tm, wm), lambda i: (i, 0)), pl.BlockSpec((tm, wg), lambda i: (i, 0))],
        out_shape=[jax.ShapeDtypeStruct((n, wm), F32), jax.ShapeDtypeStruct((n, wg), F32)],
        compiler_params=pltpu.CompilerParams(dimension_semantics=("arbitrary",),
                                             vmem_limit_bytes=VMEM_LIMIT),
        name="inproj",
    )(x2d, nw, w_pad)


def _mlstm_steps(mk, u_ref, gi_ref, gf_ref, ib_ref, fb_ref, nw_ref, y_ref, c_ref, n_ref, m_ref):
    chunks = _chunks(u_ref.shape[0])
    li = gi_ref[...] + ib_ref[...]
    xf = gf_ref[...] + fb_ref[...]
    lf = jnp.minimum(xf, 0.0) - jnp.log1p(jnp.exp(-jnp.abs(xf)))
    bcum = _chunk_cumsum(lf)
    a = li - bcum
    yield
    cmax = _chunk_cummax(a)
    yield

    m_run = m_ref[0:1, :]
    m_start, m_tot = [], []
    for sl in chunks:
        last = slice(sl.stop - 1, sl.stop)
        m_start.append(m_run)
        m_tot.append(jnp.maximum(m_run, cmax[last]))
        m_run = bcum[last] + m_tot[-1]
    m_ref[0:1, :] = m_run
    yield

    qs = [u_ref[sl, 0:256] * (HEAD_DIM ** -0.5) for sl in chunks]
    ks = [u_ref[sl, 256:512] for sl in chunks]
    vs = [u_ref[sl, 512:768] for sl in chunks]
    mx = [jnp.maximum(m0, cmax[sl]) for m0, sl in zip(m_start, chunks)]
    arow = [_rowvec(_expand4(a[sl], mk), mk) for sl in chunks]
    yield
    dmat = [_decay(ar - _expand4(m, mk), mk.causal_f) for ar, m in zip(arow, mx)]
    yield
    p = [_mm_nt(q, _bd(k, mk)) * d for q, k, d in zip(qs, ks, dmat)]
    yield
    num_i = [_mm(pp, _bd(v, mk)) for pp, v in zip(p, vs)]
    yield
    den_i = [_mm_hilo(pp, mk.ones_bd) for pp in p]
    yield
    w_int = [_expand4(jnp.exp(m0 - m), mk) for m0, m in zip(m_start, mx)]
    bound = [_expand4(jnp.exp(-(bcum[sl] + m)), mk) for sl, m in zip(chunks, mx)]
    yield
    kw = [k * _expand4(jnp.exp(a[sl] - mt), mk) for k, sl, mt in zip(ks, chunks, m_tot)]
    decay = [_expand4(jnp.exp(m0 - mt), mk) for m0, mt in zip(m_start, m_tot)]
    yield
    d_c = [_mm_tn(kk, v) * mk.bd_f32 for kk, v in zip(kw, vs)]
    d_n = [jnp.sum(kk, axis=0, keepdims=True) for kk in kw]
    yield

    nw = nw_ref[...]
    for i, sl in enumerate(chunks):
        c_bd = c_ref[...]
        n_row = n_ref[0:1, :]
        num = num_i[i] + w_int[i] * _mm(qs[i], c_bd)
        den = den_i[i] + w_int[i] * _mm_hilo(qs[i] * n_row, mk.ones_bd)
        h = num / jnp.maximum(jnp.abs(den), bound[i])
        y_ref[sl, :] = _head_rmsnorm(h, mk) * nw * _sigmoid(u_ref[sl, 768:1024])
        c_ref[...] = decay[i] * c_bd + d_c[i]
        n_ref[0:1, :] = decay[i] * n_row + d_n[i]
        yield


def _mamba_steps(mk, u_ref, gdt_ref, cw_ref, cb_ref, dtb_ref, alog_ref, dskip_ref, nw_ref, y_ref, s_ref, prev_ref):
    rows = u_ref.shape[0]
    chunks = _chunks(rows)
    raw = u_ref[:, 256:1024]
    xbc = _silu(_causal_conv(raw, prev_ref[...], cw_ref[...]) + cb_ref[...])
    prev_ref[...] = raw[rows - 8:rows, :]
    yield
    dt = _softplus(gdt_ref[...] + dtb_ref[...])
    cum = _chunk_cumsum(dt * (-jnp.exp(alog_ref[...])))
    grp_b = (((mk.row >> 6) >> 1) == (mk.lane >> 7)).astype(BF16)
    grp_s = ((mk.row >> 7) == (mk.lane_head >> 1)).astype(F32)
    yield

    xs = [xbc[sl, 0:256] for sl in chunks]
    bm = [xbc[sl, 256:512] for sl in chunks]
    cm = [xbc[sl, 512:768] for sl in chunks]
    cum_e = [_expand4(cum[sl], mk) for sl in chunks]
    yield
    seg = [_decay(ce - _rowvec(ce, mk), mk.causal_f) for ce in cum_e]
    yield
    dtrow = [_rowvec(_expand4(dt[sl], mk), mk) for sl in chunks]
    yield
    p = [_mm_nt(c, jnp.concatenate([b.astype(BF16)] * 4, axis=0) * grp_b) * sg * dr
         for c, b, sg, dr in zip(cm, bm, seg, dtrow)]
    yield
    y_i = [_mm(pp, _bd(x, mk)) + x * dskip_ref[...] for pp, x in zip(p, xs)]
    yield
    last = [slice(sl.stop - 1, sl.stop) for sl in chunks]
    xw = [x * _expand4(jnp.exp(cum[ls] - cum[sl]) * dt[sl], mk) for x, sl, ls in zip(xs, chunks, last)]
    yield
    d_s = [_mm_tn(b, w) * grp_s for b, w in zip(bm, xw)]
    yield
    decay = [_expand4(jnp.exp(cum[ls]), mk) for ls in last]
    e_cum = [jnp.exp(ce) for ce in cum_e]
    yield

    nw = nw_ref[...]
    for i, sl in enumerate(chunks):
        s_mat = s_ref[...]
        y = y_i[i] + e_cum[i] * _mm(cm[i], s_mat)
        yz = y * _silu(u_ref[sl, 0:256])
        halves = []
        for g in range(2):
            yg = yz[:, g * 128:(g + 1) * 128]
            halves.append(yg * lax.rsqrt(jnp.mean(yg * yg, axis=-1, keepdims=True) + EPS))
        y_ref[sl, :] = jnp.concatenate(halves, axis=-1) * nw
        s_ref[...] = decay[i] * s_mat + d_s[i]
        yield


HG_SUB = 16


def _stack_rows(x, mk):
    n = x.shape[0]
    r = _iota((4 * n, 1), 0)
    blk = (r >= n).astype(jnp.int32) + (r >= 2 * n).astype(jnp.int32) + (r >= 3 * n).astype(jnp.int32)
    x = x.astype(BF16)
    return jnp.where(blk == mk.lane_head, jnp.concatenate([x, x, x, x], axis=0), jnp.zeros((), BF16))


def _hgrn_intra_sub(qq, kk, gc, i_in, a, mk):
    half = HG_SUB // 2
    t_half = _iota((half, 1), 0)
    lo = a * HG_SUB
    q_a, k_a, g_a, i_a = qq[lo:lo + HG_SUB], kk[lo:lo + HG_SUB], gc[lo:lo + HG_SUB], i_in[lo:lo + HG_SUB]
    q_t, q_b, g_t, g_b = q_a[0:half], q_a[half:], g_a[0:half], g_a[half:]
    tops, bots = [], []
    for s in range(HG_SUB):
        k_s, g_s = k_a[s:s + 1, :], g_a[s:s + 1, :]
        if s < half:
            tops.append(q_t * k_s * _decay(g_t - g_s, (t_half >= s).astype(F32), jnp.exp2))
            bots.append(q_b * k_s * jnp.exp2(g_b - g_s))
        else:
            bots.append(q_b * k_s * _decay(g_b - g_s, (t_half >= s - half).astype(F32), jnp.exp2))
    z = _mm(jnp.concatenate(tops + bots, axis=0), mk.ones_bd)
    z_t, z_b = z[0:half * half], z[half * half:]
    o_t = z_t[0:half] * i_a[0:1, :]
    for s in range(1, half):
        o_t = o_t + z_t[s * half:(s + 1) * half] * i_a[s:s + 1, :]
    o_b = z_b[0:half] * i_a[0:1, :]
    for s in range(1, HG_SUB):
        o_b = o_b + z_b[s * half:(s + 1) * half] * i_a[s:s + 1, :]
    o = jnp.concatenate([o_t, o_b], axis=0)
    if a > 0:
        r = gc[lo - 1:lo, :]
        sc = _mm_nt(q_a * jnp.exp2(g_a - r), _stack_rows(kk[0:lo] * jnp.exp2(r - gc[0:lo]), mk))
        o = o + _mm(sc, _stack_rows(i_in[0:lo], mk))
    return o


def _hgrn_steps(mk, u_ref, lbl_ref, nw_ref, y_ref, st_ref, *, layer):
    logits = lbl_ref[...]
    e = jnp.exp(logits - jnp.max(logits, axis=0, keepdims=True))
    prob = e / jnp.sum(e, axis=0, keepdims=True)
    lb = jnp.sum(prob[0:layer + 1], axis=0, keepdims=True) - prob[0:1]

    chunks = _chunks(u_ref.shape[0])
    f = lb + (1.0 - lb) * _sigmoid(u_ref[:, 256:512])
    kk_all = 1.0 - f
    qq_all = _silu(u_ref[:, 0:256])
    yield
    gc_all = _chunk_cumsum(jnp.log(f)) * LOG2E
    yield

    qq = [qq_all[sl] for sl in chunks]
    kk = [kk_all[sl] for sl in chunks]
    gc = [gc_all[sl] for sl in chunks]
    ii = [u_ref[sl, 512:768] for sl in chunks]
    subs = [[] for _ in chunks]
    for a in range(CHUNK // HG_SUB):
        for c in range(len(chunks)):
            subs[c].append(_hgrn_intra_sub(qq[c], kk[c], gc[c], ii[c], a, mk))
            yield
    o_i = [jnp.concatenate(s, axis=0) for s in subs]
    qe = [q * jnp.exp2(g) for q, g in zip(qq, gc)]
    g_last = [g[CHUNK - 1:CHUNK, :] for g in gc]
    yield
    d_s = [_mm_tn(i, k * jnp.exp2(gl - g)) * mk.bd_f32
           for i, k, g, gl in zip(ii, kk, gc, g_last)]
    decay = [jnp.exp2(gl) for gl in g_last]
    yield

    nw = nw_ref[...]
    for i, sl in enumerate(chunks):
        st = st_ref[...]
        o = o_i[i] + _mm_nt(qe[i], st)
        y_ref[sl, :] = _head_rmsnorm(o, mk) * nw * _silu(u_ref[sl, 768:1024])
        st_ref[...] = decay[i] * st + d_s[i]
        yield


def _gdn_steps(mk, u_ref, gb_ref, ga_ref, cw_ref, alog_ref, dtb_ref, nw_ref, y_ref, s_ref, prev_ref):
    rows = u_ref.shape[0]
    chunks = _chunks(rows)
    raw = u_ref[:, 0:768]
    qkv = _silu(_causal_conv(raw, prev_ref[...], cw_ref[...]))
    prev_ref[...] = raw[rows - 8:rows, :]
    yield
    q, k, v = qkv[:, 0:256], qkv[:, 256:512], qkv[:, 512:768]
    qn = q * lax.rsqrt(_mm_hilo(q * q, mk.ones_bd) + EPS) * (HEAD_DIM ** -0.5)
    kn = k * lax.rsqrt(_mm_hilo(k * k, mk.ones_bd) + EPS)
    yield
    gcum = _chunk_cumsum(-jnp.exp(alog_ref[...]) * _softplus(ga_ref[...] + dtb_ref[...]))
    be = _expand4(_sigmoid(gb_ref[...]), mk)
    ge = _expand4(gcum, mk)
    yield
    e_g = jnp.exp(ge)
    kb = kn * be
    vb = v * be
    kbg = kb * e_g
    q_dec = qn * e_g
    eye = jnp.where(mk.diag, 1.0, 0.0)
    yield

    e_dec = [jnp.exp(jnp.minimum(ge[sl] - _rowvec(ge[sl], mk), 0.0)) for sl in chunks]
    yield
    qk = [_mm_nt(jnp.concatenate([kb[sl], qn[sl]], axis=0), _bd(kn[sl], mk)) for sl in chunks]
    yield
    a_p = [x[0:CHUNK] * (e * mk.strict_f) for x, e in zip(qk, e_dec)]
    attn = [x[CHUNK:2 * CHUNK] * (e * mk.causal_f) for x, e in zip(qk, e_dec)]

    pw = [_mm(-x, _bd(-x, mk)) for x in a_p]
    t0 = [eye - x for x in a_p]
    yield
    for j in range(1, 6):
        r = [_mm(jnp.concatenate([t, m], axis=0), _bd(m, mk)) for t, m in zip(t0, pw)]
        t0 = [t + x[0:CHUNK] for t, x in zip(t0, r)]
        pw = [x[CHUNK:2 * CHUNK] for x in r]
        yield
    a_t0 = []
    for x, t in zip(a_p, t0):
        a_hi, a_lo = _split2(x)
        t_hi, t_lo = _split2(t)
        r = jnp.dot(jnp.concatenate([a_hi, a_lo], axis=0), _bd(t_hi, mk).astype(BF16), preferred_element_type=F32)
        a_t0.append(r[0:CHUNK] + r[CHUNK:2 * CHUNK]
                    + jnp.dot(a_hi, _bd(t_lo, mk).astype(BF16), preferred_element_type=F32))
    yield
    t_mat = [t + _mm(t, _bd(eye - t - at, mk)) for t, at in zip(t0, a_t0)]
    yield

    uw = [_mm(t, jnp.concatenate([_bd(vb[sl], mk), _bd(kbg[sl], mk)], axis=1)) for t, sl in zip(t_mat, chunks)]
    u = [x[:, 0:256] for x in uw]
    w = [x[:, 256:512] for x in uw]
    yield
    au = [_mm(at, jnp.concatenate([_bd(uu, mk), _bd(ww, mk)], axis=1)) for at, uu, ww in zip(attn, u, w)]
    o_i = [x[:, 0:256] for x in au]
    q2 = [q_dec[sl] - x[:, 256:512] for sl, x in zip(chunks, au)]
    yield
    last = [slice(sl.stop - 1, sl.stop) for sl in chunks]
    k_dec = [kn[sl] * _expand4(jnp.exp(gcum[ls] - gcum[sl]), mk) for sl, ls in zip(chunks, last)]
    fg = [_mm_tn(kd, jnp.concatenate([ww, uu], axis=1)) for kd, ww, uu in zip(k_dec, w, u)]
    yield
    f_p = [-_unbd(x[:, 0:256], mk) for x in fg]
    g_p = [_unbd(x[:, 256:512], mk) for x in fg]
    decay = [_expand4(jnp.exp(gcum[ls]), mk) for ls in last]
    yield

    nw = nw_ref[...]
    for i, sl in enumerate(chunks):
        s_p = s_ref[...]
        r = _mm(jnp.concatenate([q2[i], f_p[i]], axis=0), _bd(s_p, mk))
        o = o_i[i] + r[0:CHUNK]
        y_ref[sl, :] = _head_rmsnorm(o, mk) * nw * _silu(u_ref[sl, 768:1024])
        s_ref[...] = decay[i] * s_p + r[CHUNK:2 * CHUNK] + g_p[i]
        yield


_DONE = object()


N_MIXER_PARAMS = 15
N_MIXER_STATES = 8


def _mixer_steps(um_ref, ug_ref, params, y_ref, states, layer):
    ib, fb, nw_a, cw_b, cb_b, dtb_b, alog_b, dskip_b, nw_b, lbl, nw_c, cw_d, alog_d, dtb_d, nw_d = params
    c_ref, n_ref, m_ref, sb_ref, prevb_ref, st_ref, sd_ref, prevd_ref = states
    mk = _Masks()
    win = lambda j: um_ref.at[:, pl.ds(j * 4 * GROUP_WIDTH, 4 * GROUP_WIDTH)]
    gate = lambda j: ug_ref.at[:, pl.ds(j * LANES, LANES)]
    out = lambda j: y_ref.at[:, pl.ds(j * GROUP_WIDTH, GROUP_WIDTH)]
    return [
        _gdn_steps(mk, win(3), gate(3), gate(4), cw_d, alog_d, dtb_d, nw_d, out(3), sd_ref, prevd_ref),
        _hgrn_steps(mk, win(2), lbl, nw_c, out(2), st_ref, layer=layer),
        _mlstm_steps(mk, win(0), gate(0), gate(1), ib, fb, nw_a, out(0), c_ref, n_ref, m_ref),
        _mamba_steps(mk, win(1), gate(2), cw_b, cb_b, dtb_b, alog_b, dskip_b, nw_b, out(1), sb_ref, prevb_ref),
    ]


def _run_round_robin(steps):
    while steps:
        steps = [g for g in steps if next(g, _DONE) is not _DONE]


def _ffn2_steps(x_ref, y_ref, wo_ref, nw_ref, wg_ref, wu_ref, wd_ref, fw_ref, o_ref):
    x = x_ref[...] + jnp.dot(y_ref[...].astype(BF16), wo_ref[...], preferred_element_type=F32)
    yield
    xb = _rms(x, nw_ref[...]).astype(BF16)
    yield
    acc = jnp.zeros(x.shape, F32)
    for c0 in range(0, D_FF, FF_STEP):
        sl = slice(c0, min(c0 + FF_STEP, D_FF))
        g = jnp.dot(xb, wg_ref[:, sl], preferred_element_type=F32)
        yield
        u = jnp.dot(xb, wu_ref[:, sl], preferred_element_type=F32)
        yield
        acc = acc + jnp.dot((_silu(g) * u).astype(BF16), wd_ref[sl, :], preferred_element_type=F32)
        yield
    out = x + 0.5 * acc
    if fw_ref is not None:
        out = _rms(out, fw_ref[...])
    o_ref[...] = out
    yield


def _mix_ffn_body(*refs, layer, tiles_per_seq, with_final):
    um_ref, ug_ref, x_ref = refs[0:3]
    params = refs[3:3 + N_MIXER_PARAMS]
    rest = list(refs[3 + N_MIXER_PARAMS:])
    wo_ref, nw_ref, wg_ref, wu_ref, wd_ref = rest[0:5]
    fw_ref = rest[5] if with_final else None
    o_ref = rest[5 + with_final]
    scratch = rest[6 + with_final:]
    states, y_ref = scratch[0:N_MIXER_STATES], scratch[N_MIXER_STATES]
    j = pl.program_id(0)

    @pl.when(j == 0)
    def _():
        y_ref[...] = jnp.zeros_like(y_ref)

    @pl.when(j % tiles_per_seq == 0)
    def _():
        for r in states:
            r[...] = jnp.zeros_like(r)

    _run_round_robin([_ffn2_steps(x_ref, y_ref, wo_ref, nw_ref, wg_ref, wu_ref, wd_ref, fw_ref, o_ref)]
                     + _mixer_steps(um_ref, ug_ref, params, y_ref, states, layer))


def _mixer_param_specs(params, layer):
    return [pl.BlockSpec((None,) + p.shape[1:], lambda *_: (layer, 0, 0)) if per_layer
            else pl.BlockSpec(p.shape, lambda *_, nd=p.ndim: (0,) * nd) for p, per_layer in params]


def _mixer_state_shapes():
    state = pltpu.VMEM((GROUP_WIDTH, GROUP_WIDTH), F32)
    conv_tail = pltpu.VMEM((8, 3 * GROUP_WIDTH), F32)
    return [state, pltpu.VMEM((8, GROUP_WIDTH), F32), pltpu.VMEM((8, LANES), F32),
            state, conv_tail,
            state,
            pltpu.VMEM((HEAD_DIM, GROUP_WIDTH), F32), conv_tail]


def _mix_ffn_call(u_main, u_gate, x2d, params, wo, nw, wg, wu, wd, layer, seq_len, final_w=None):
    n = x2d.shape[0]
    tb = min(MIX_TOKENS, seq_len)
    tiles = n // tb
    cur = lambda w: pl.BlockSpec((tb, w), lambda j: (jnp.minimum(j, tiles - 1), 0))
    prev = lambda w: pl.BlockSpec((tb, w), lambda j: (jnp.maximum(j - 1, 0), 0))
    args = [u_main, u_gate, x2d] + [p for p, _ in params] + [wo, nw, wg, wu, wd]
    in_specs = ([cur(u_main.shape[1]), cur(u_gate.shape[1]), prev(D_MODEL)] + _mixer_param_specs(params, layer)
                + [_layer_spec(a, layer) for a in (wo, nw, wg, wu, wd)])
    if final_w is not None:
        args.append(final_w)
        in_specs.append(_const_spec(final_w.shape))
    return pl.pallas_call(
        functools.partial(_mix_ffn_body, layer=layer, tiles_per_seq=seq_len // tb, with_final=final_w is not None),
        grid=(tiles + 1,),
        in_specs=in_specs,
        out_specs=prev(D_MODEL),
        out_shape=jax.ShapeDtypeStruct((n, D_MODEL), F32),
        scratch_shapes=_mixer_state_shapes() + [pltpu.VMEM((tb, N_MIXERS * GROUP_WIDTH), F32)],
        compiler_params=pltpu.CompilerParams(dimension_semantics=("arbitrary",),
                                             vmem_limit_bytes=VMEM_LIMIT),
        name="mix_ffn",
    )(*args)


def _rows(v):
    return v.astype(F32)[:, None, :]


def _head_rows(v):
    return jnp.pad(_rows(v), ((0, 0), (0, 0), (0, LANES - v.shape[1])))


def _mixer_params(mlstm_i_bias, mlstm_f_bias, mlstm_norm, mamba_conv_w, mamba_conv_b, mamba_dt_bias, mamba_a_log,
                  mamba_d, mamba_norm, hgrn_lb_logits, hgrn_norm, gdn_conv_w, gdn_a_log, gdn_dt_bias, gdn_norm):
    per_layer = [
        _head_rows(mlstm_i_bias), _head_rows(mlstm_f_bias), _rows(mlstm_norm),
        mamba_conv_w.astype(F32), _rows(mamba_conv_b), _head_rows(mamba_dt_bias), _head_rows(mamba_a_log),
        _rows(jnp.repeat(mamba_d, HEAD_DIM, axis=1)), _rows(mamba_norm)]
    hgrn = [(hgrn_lb_logits.astype(F32), False), (_rows(hgrn_norm), True)]
    gdn = [gdn_conv_w.astype(F32), _head_rows(gdn_a_log), _head_rows(gdn_dt_bias), _rows(gdn_norm)]
    return [(p, True) for p in per_layer] + hgrn + [(p, True) for p in gdn]


_MLSTM0, _MAMBA0, _HGRN0, _GDN0 = 0, 1032, 2060, 3084
_MAIN_COLS = ((_MLSTM0, _MLSTM0 + 1024), (_MAMBA0, _MAMBA0 + 1024), (_HGRN0, _HGRN0 + 1024),
              (_GDN0, _GDN0 + 1024))
_GATE_COLS = (_MLSTM0 + 1024, _MLSTM0 + 1028, _MAMBA0 + 1024, _GDN0 + 1024, _GDN0 + 1028)
_MAIN_STARTS = tuple(a for a, _ in _MAIN_COLS)
_D_IN_PAD = 33 * LANES


def kernel(x, ffn1_norm, ffn1_w_gate, ffn1_w_up, ffn1_w_down, mix_norm, w_in, w_out, mlstm_i_bias, mlstm_f_bias, mlstm_norm, mamba_conv_w, mamba_conv_b, mamba_dt_bias, mamba_a_log, mamba_d, mamba_norm, hgrn_lb_logits, hgrn_norm, gdn_conv_w, gdn_a_log, gdn_dt_bias, gdn_norm, ffn2_norm, ffn2_w_gate, ffn2_w_up, ffn2_w_down, final_norm):
    b, s, d = x.shape
    depth = w_in.shape[0]
    x2d = x.reshape(b * s, d)
    ffn1 = (_rows(ffn1_norm), _to_bf16(ffn1_w_gate), _to_bf16(ffn1_w_up), _to_bf16(ffn1_w_down))
    ffn2 = (_rows(ffn2_norm), _to_bf16(ffn2_w_gate), _to_bf16(ffn2_w_up), _to_bf16(ffn2_w_down))
    w_pad = _to_bf16(w_in, _D_IN_PAD)
    w_o = _to_bf16(w_out)
    mix_nw = _rows(mix_norm)
    params = _mixer_params(mlstm_i_bias, mlstm_f_bias, mlstm_norm, mamba_conv_w, mamba_conv_b, mamba_dt_bias,
                           mamba_a_log, mamba_d, mamba_norm, hgrn_lb_logits, hgrn_norm, gdn_conv_w, gdn_a_log,
                           gdn_dt_bias, gdn_norm)
    for l in range(depth):
        x2d = _ffn_call(x2d, *ffn1, l)
        um, ug = _inproj_call(x2d, mix_nw, w_pad, l)
        x2d = _mix_ffn_call(um, ug, x2d, params, w_o, *ffn2, l, s,
                            final_w=final_norm.astype(F32).reshape(1, d) if l == depth - 1 else None)
    return x2d.reshape(b, s, d)
```

```python
import functools

import numpy as np
import jax
import jax.numpy as jnp
from jax import lax
from jax.experimental import pallas as pl
from jax.experimental.pallas import tpu as pltpu

F32 = jnp.float32
BF16 = jnp.bfloat16

D_MODEL = 1024
CHUNK = 64
N_HEADS = 4
HEAD_DIM = 64
GROUP_WIDTH = N_HEADS * HEAD_DIM
SSM_STATE = 128
CONV_K = 4
D_FF = 2816
EPS = 1e-6
NEG_BIG = -1e30
LOG2E = 1.4426950408889634
LANES = 128
N_GATES = 5

FFN_TOKENS = 512
CAST_STEPS = 2
MXU_DIM = 256
FF_SPLIT = 6 * MXU_DIM
N_MIXERS = 4
MLSTM_PACE = 0.7
MAMBA_PACE = 0.55
MIX_TOKENS = 256
VMEM_LIMIT = 56 * 1024 * 1024


def _iota(shape, dim):
    return lax.broadcasted_iota(jnp.int32, shape, dim)


def _mm(a, b):
    return jnp.dot(a.astype(BF16), b.astype(BF16), preferred_element_type=F32)


def _mm_nt(a, b):
    return lax.dot_general(a.astype(BF16), b.astype(BF16), (((1,), (1,)), ((), ())),
                           preferred_element_type=F32)


def _mm_tn(a, b):
    return lax.dot_general(a.astype(BF16), b.astype(BF16), (((0,), (0,)), ((), ())),
                           preferred_element_type=F32)


def _split2(a):
    hi = a.astype(BF16)
    return hi, (a - hi.astype(F32)).astype(BF16)


def _mm_hilo(a, b_bf16):
    hi, lo = _split2(a)
    return (jnp.dot(hi, b_bf16, preferred_element_type=F32)
            + jnp.dot(lo, b_bf16, preferred_element_type=F32))


def _sigmoid(x):
    return 0.5 * jnp.tanh(0.5 * x) + 0.5


def _silu(x):
    h = 0.5 * x
    return h * jnp.tanh(h) + h


def _softplus(x):
    return jnp.maximum(x, 0.0) + jnp.log1p(jnp.exp(-jnp.abs(x)))


class _Masks:
    def __init__(self):
        row = _iota((GROUP_WIDTH, 1), 0)
        lane = _iota((1, GROUP_WIDTH), 1)
        t = _iota((CHUNK, 1), 0)
        self.row, self.lane = row, lane
        self.lane_head = lane >> 6
        block_diag = (row >> 6) == self.lane_head
        self.ones_bd = block_diag.astype(BF16)
        self.bd_f32 = block_diag.astype(F32)
        s = lane & 63
        self.causal_f = (t >= s).astype(F32)
        self.strict_f = (t > s).astype(F32)
        self.diag = t == s


def _bd(x, mk):
    x = x.astype(BF16)
    return jnp.concatenate([x, x, x, x], axis=0) * mk.ones_bd


def _unbd(y, mk):
    ym = y * mk.bd_f32
    return ym[0:64] + ym[64:128] + ym[128:192] + ym[192:256]


def _decay(arg, mask_f, exp=jnp.exp):
    return exp(jnp.minimum(arg, 0.0)) * mask_f


def _expand4(g, mk):
    first = _iota((1, LANES), 1) < HEAD_DIM
    return jnp.concatenate([jnp.where(first, g[:, 0:1], g[:, 1:2]), jnp.where(first, g[:, 2:3], g[:, 3:4])], axis=1)


def _rowvec(ge, mk):
    return jnp.sum(jnp.where(mk.diag, ge, 0.0), axis=0, keepdims=True)


def _chunk_cumsum(x):
    rows = x.shape[0]
    r, c = _iota((rows, rows), 0), _iota((rows, rows), 1)
    tril = (((r >> 6) == (c >> 6)) & (r >= c)).astype(BF16)
    x1 = x.astype(BF16)
    r1 = x - x1.astype(F32)
    x2 = r1.astype(BF16)
    x3 = (r1 - x2.astype(F32)).astype(BF16)
    return (jnp.dot(tril, x1, preferred_element_type=F32) + jnp.dot(tril, x2, preferred_element_type=F32)
            + jnp.dot(tril, x3, preferred_element_type=F32))


def _chunk_cummax(x):
    t = _iota((x.shape[0], 1), 0) & 63
    k = 1
    while k < CHUNK:
        x = jnp.maximum(x, jnp.where(t >= k, pltpu.roll(x, k, 0), NEG_BIG))
        k *= 2
    return x


def _causal_conv(x, prev8, w):
    acc = x * w[CONV_K - 1:CONV_K, :]
    r8 = _iota((8, 1), 0)
    for j in range(1, CONV_K):
        xr = pltpu.roll(x, j, 0)
        top = jnp.where(r8 < j, pltpu.roll(prev8, j, 0), xr[0:8])
        xs = jnp.concatenate([top, xr[8:]], axis=0)
        acc = acc + xs * w[CONV_K - 1 - j:CONV_K - j, :]
    return acc


def _head_rmsnorm(o, mk):
    return o * lax.rsqrt(_mm_hilo(o * o, mk.ones_bd) * (1.0 / HEAD_DIM) + EPS)


def _chunks(n_rows):
    return [slice(c * CHUNK, (c + 1) * CHUNK) for c in range(n_rows // CHUNK)]


def _rms(x, w):
    return x * lax.rsqrt(jnp.mean(x * x, axis=-1, keepdims=True) + EPS) * w


def _ffn_body(*refs, with_mix, with_final):
    it = iter(refs)
    x_ref = next(it)
    if with_mix:
        y_ref, wo_ref = next(it), next(it)
    nw_ref, wg_ref, wu_ref, wd_ref = next(it), next(it), next(it), next(it)
    if with_final:
        fw_ref = next(it)
    o_ref = next(it)

    x = x_ref[...]
    if with_mix:
        x = x + jnp.dot(y_ref[...].astype(BF16), wo_ref[...], preferred_element_type=F32)
    xb = _rms(x, nw_ref[...]).astype(BF16)
    acc = jnp.zeros(x.shape, F32)
    for sl in (slice(0, FF_SPLIT), slice(FF_SPLIT, D_FF)):
        g = jnp.dot(xb, wg_ref[:, sl], preferred_element_type=F32)
        u = jnp.dot(xb, wu_ref[:, sl], preferred_element_type=F32)
        h = (_silu(g) * u).astype(BF16)
        acc = acc + jnp.dot(h, wd_ref[sl, :], preferred_element_type=F32)
    out = x + 0.5 * acc
    if with_final:
        out = _rms(out, fw_ref[...])
    o_ref[...] = out


def _const_spec(shape):
    return pl.BlockSpec(shape, lambda *_: (0,) * len(shape), pipeline_mode=pl.Buffered(1))


def _layer_spec(a, layer):
    return pl.BlockSpec((None,) + a.shape[1:], lambda *_: (layer, 0, 0), pipeline_mode=pl.Buffered(1))


def _cast_body(w_ref, o_ref):
    cols = w_ref.shape[1]
    if o_ref.shape[1] != cols:
        o_ref[...] = jnp.zeros(o_ref.shape, BF16)
    o_ref[:, 0:cols] = w_ref[...].astype(BF16)


def _to_bf16(w, cols_out=None):
    depth, rows, cols = w.shape
    cols_out = cols_out or cols
    tr = rows // CAST_STEPS
    return pl.pallas_call(
        _cast_body,
        grid=(depth, CAST_STEPS),
        in_specs=[pl.BlockSpec((None, tr, cols), lambda l, i: (l, i, 0))],
        out_specs=pl.BlockSpec((None, tr, cols_out), lambda l, i: (l, i, 0)),
        out_shape=jax.ShapeDtypeStruct((depth, rows, cols_out), BF16),
        compiler_params=pltpu.CompilerParams(dimension_semantics=("arbitrary", "arbitrary"),
                                             vmem_limit_bytes=VMEM_LIMIT),
        name="cast",
    )(w)


def _ffn_call(x2d, nw, wg, wu, wd, layer, mix=None, final_w=None):
    n = x2d.shape[0]
    tm = min(FFN_TOKENS, n)
    tok = lambda w: pl.BlockSpec((tm, w), lambda i: (i, 0))
    args, specs = [x2d], [tok(D_MODEL)]
    if mix is not None:
        y, wo = mix
        args += [y, wo]
        specs += [tok(y.shape[1]), _layer_spec(wo, layer)]
    args += [nw, wg, wu, wd]
    specs += [_layer_spec(a, layer) for a in (nw, wg, wu, wd)]
    if final_w is not None:
        args.append(final_w)
        specs.append(_const_spec(final_w.shape))
    return pl.pallas_call(
        functools.partial(_ffn_body, with_mix=mix is not None, with_final=final_w is not None),
        grid=(n // tm,),
        in_specs=specs,
        out_specs=tok(D_MODEL),
        out_shape=jax.ShapeDtypeStruct((n, D_MODEL), F32),
        compiler_params=pltpu.CompilerParams(dimension_semantics=("arbitrary",),
                                             vmem_limit_bytes=VMEM_LIMIT),
        name="ffn",
    )(*args)


def _inproj_body(x_ref, nw_ref, w_ref, um_ref, ug_ref):
    xb = _rms(x_ref[...], nw_ref[...]).astype(BF16)
    u = jnp.dot(xb, w_ref[...], preferred_element_type=F32)
    for j, start in enumerate(_MAIN_STARTS):
        lo = (start // LANES) * LANES
        width = 4 * GROUP_WIDTH
        if start == lo:
            blk = u[:, lo:lo + width]
        else:
            win = u[:, lo:lo + width + LANES]
            blk = pltpu.roll(win, width + LANES - (start - lo), 1)[:, 0:width]
        um_ref[:, j * width:(j + 1) * width] = blk
    head_lane = _iota((1, LANES), 1) < N_HEADS
    for j, start in enumerate(_GATE_COLS):
        lo = (start // LANES) * LANES
        tile = u[:, lo:lo + LANES]
        if start != lo:
            tile = pltpu.roll(tile, LANES - (start - lo), 1)
        ug_ref[:, j * LANES:(j + 1) * LANES] = jnp.where(head_lane, tile, 0.0)


def _inproj_call(x2d, nw, w_pad, layer):
    n = x2d.shape[0]
    tm = min(FFN_TOKENS, n)
    wm, wg = 4 * 4 * GROUP_WIDTH, N_GATES * LANES
    return pl.pallas_call(
        _inproj_body,
        grid=(n // tm,),
        in_specs=[pl.BlockSpec((tm, D_MODEL), lambda i: (i, 0)), _layer_spec(nw, layer), _layer_spec(w_pad, layer)],
        out_specs=[pl.BlockSpec((tm, wm), lambda i: (i, 0)), pl.BlockSpec((tm, wg), lambda i: (i, 0))],
        out_shape=[jax.ShapeDtypeStruct((n, wm), F32), jax.ShapeDtypeStruct((n, wg), F32)],
        compiler_params=pltpu.CompilerParams(dimension_semantics=("arbitrary",),
                                             vmem_limit_bytes=VMEM_LIMIT),
        name="inproj",
    )(x2d, nw, w_pad)


def _mlstm_steps(mk, u_ref, gi_ref, gf_ref, ib_ref, fb_ref, nw_ref, y_ref, c_ref, n_ref, m_ref):
    chunks = _chunks(u_ref.shape[0])
    li = gi_ref[...] + ib_ref[...]
    xf = gf_ref[...] + fb_ref[...]
    lf = jnp.minimum(xf, 0.0) - jnp.log1p(jnp.exp(-jnp.abs(xf)))
    bcum = _chunk_cumsum(lf)
    a = li - bcum
    yield
    cmax = _chunk_cummax(a)
    yield

    m_run = m_ref[0:1, :]
    m_start, m_tot = [], []
    for sl in chunks:
        last = slice(sl.stop - 1, sl.stop)
        m_start.append(m_run)
        m_tot.append(jnp.maximum(m_run, cmax[last]))
        m_run = bcum[last] + m_tot[-1]
    m_ref[0:1, :] = m_run
    yield

    qs = [u_ref[sl, 0:256] * (HEAD_DIM ** -0.5) for sl in chunks]
    ks = [u_ref[sl, 256:512] for sl in chunks]
    vs = [u_ref[sl, 512:768] for sl in chunks]
    mx = [jnp.maximum(m0, cmax[sl]) for m0, sl in zip(m_start, chunks)]
    arow = [_rowvec(_expand4(a[sl], mk), mk) for sl in chunks]
    yield
    dmat = [_decay(ar - _expand4(m, mk), mk.causal_f) for ar, m in zip(arow, mx)]
    yield
    p = [_mm_nt(q, _bd(k, mk)) * d for q, k, d in zip(qs, ks, dmat)]
    yield
    num_i = [_mm(pp, _bd(v, mk)) for pp, v in zip(p, vs)]
    yield
    den_i = [_mm_hilo(pp, mk.ones_bd) for pp in p]
    yield
    w_int = [_expand4(jnp.exp(m0 - m), mk) for m0, m in zip(m_start, mx)]
    bound = [_expand4(jnp.exp(-(bcum[sl] + m)), mk) for sl, m in zip(chunks, mx)]
    yield
    kw = [k * _expand4(jnp.exp(a[sl] - mt), mk) for k, sl, mt in zip(ks, chunks, m_tot)]
    decay = [_expand4(jnp.exp(m0 - mt), mk) for m0, mt in zip(m_start, m_tot)]
    yield
    d_c = [_mm_tn(kk, v) * mk.bd_f32 for kk, v in zip(kw, vs)]
    d_n = [jnp.sum(kk, axis=0, keepdims=True) for kk in kw]
    yield

    c_bd, n_row = c_ref[...], n_ref[0:1, :]
    c_at, n_at = [], []
    for i in range(len(chunks)):
        c_at.append(c_bd.astype(BF16))
        n_at.append(n_row)
        c_bd = decay[i] * c_bd + d_c[i]
        n_row = decay[i] * n_row + d_n[i]
    c_ref[...] = c_bd
    n_ref[0:1, :] = n_row
    yield
    q_c = [_mm(q, c) for q, c in zip(qs, c_at)]
    yield
    q_n = [_mm_hilo(q * n, mk.ones_bd) for q, n in zip(qs, n_at)]
    yield
    h = [(ni + w * qc) / jnp.maximum(jnp.abs(di + w * qn), b)
         for ni, di, w, qc, qn, b in zip(num_i, den_i, w_int, q_c, q_n, bound)]
    yield
    hn = [_head_rmsnorm(x, mk) for x in h]
    yield
    nw = nw_ref[...]
    for sl, x in zip(chunks, hn):
        y_ref[sl, :] = x * nw * _sigmoid(u_ref[sl, 768:1024])
    yield


def _mamba_steps(mk, u_ref, gdt_ref, cw_ref, cb_ref, dtb_ref, alog_ref, dskip_ref, nw_ref, y_ref, s_ref, prev_ref):
    rows = u_ref.shape[0]
    chunks = _chunks(rows)
    raw = u_ref[:, 256:1024]
    xbc = _silu(_causal_conv(raw, prev_ref[...], cw_ref[...]) + cb_ref[...])
    prev_ref[...] = raw[rows - 8:rows, :]
    yield
    dt = _softplus(gdt_ref[...] + dtb_ref[...])
    cum = _chunk_cumsum(dt * (-jnp.exp(alog_ref[...])))
    grp_b = (((mk.row >> 6) >> 1) == (mk.lane >> 7)).astype(BF16)
    grp_s = ((mk.row >> 7) == (mk.lane_head >> 1)).astype(F32)
    yield

    xs = [xbc[sl, 0:256] for sl in chunks]
    bm = [xbc[sl, 256:512] for sl in chunks]
    cm = [xbc[sl, 512:768] for sl in chunks]
    cum_e = [_expand4(cum[sl], mk) for sl in chunks]
    yield
    seg = [_decay(ce - _rowvec(ce, mk), mk.causal_f) for ce in cum_e]
    yield
    dtrow = [_rowvec(_expand4(dt[sl], mk), mk) for sl in chunks]
    yield
    p = [_mm_nt(c, jnp.concatenate([b.astype(BF16)] * 4, axis=0) * grp_b) * sg * dr
         for c, b, sg, dr in zip(cm, bm, seg, dtrow)]
    yield
    y_i = [_mm(pp, _bd(x, mk)) + x * dskip_ref[...] for pp, x in zip(p, xs)]
    yield
    last = [slice(sl.stop - 1, sl.stop) for sl in chunks]
    xw = [x * _expand4(jnp.exp(cum[ls] - cum[sl]) * dt[sl], mk) for x, sl, ls in zip(xs, chunks, last)]
    yield
    d_s = [_mm_tn(b, w) * grp_s for b, w in zip(bm, xw)]
    yield
    decay = [_expand4(jnp.exp(cum[ls]), mk) for ls in last]
    e_cum = [jnp.exp(ce) for ce in cum_e]
    yield

    s_mat = s_ref[...]
    s_at = []
    for i in range(len(chunks)):
        s_at.append(s_mat.astype(BF16))
        s_mat = decay[i] * s_mat + d_s[i]
    s_ref[...] = s_mat
    yield
    y = [yi + e * _mm(c, s) for yi, e, c, s in zip(y_i, e_cum, cm, s_at)]
    yield
    nw = nw_ref[...]
    for sl, yc in zip(chunks, y):
        yz = yc * _silu(u_ref[sl, 0:256])
        halves = []
        for g in range(2):
            yg = yz[:, g * 128:(g + 1) * 128]
            halves.append(yg * lax.rsqrt(jnp.mean(yg * yg, axis=-1, keepdims=True) + EPS))
        y_ref[sl, :] = jnp.concatenate(halves, axis=-1) * nw
    yield


HG_SUB = 16


def _stack_rows(x, mk):
    n = x.shape[0]
    r = _iota((4 * n, 1), 0)
    blk = (r >= n).astype(jnp.int32) + (r >= 2 * n).astype(jnp.int32) + (r >= 3 * n).astype(jnp.int32)
    x = x.astype(BF16)
    return jnp.where(blk == mk.lane_head, jnp.concatenate([x, x, x, x], axis=0), jnp.zeros((), BF16))


def _hgrn_intra_sub(qq, kk, gc, i_in, a, mk):
    half = HG_SUB // 2
    t_half = _iota((half, 1), 0)
    lo = a * HG_SUB
    q_a, k_a, g_a, i_a = qq[lo:lo + HG_SUB], kk[lo:lo + HG_SUB], gc[lo:lo + HG_SUB], i_in[lo:lo + HG_SUB]
    q_t, q_b, g_t, g_b = q_a[0:half], q_a[half:], g_a[0:half], g_a[half:]
    tops, bots = [], []
    for s in range(HG_SUB):
        k_s, g_s = k_a[s:s + 1, :], g_a[s:s + 1, :]
        if s < half:
            tops.append(q_t * k_s * _decay(g_t - g_s, (t_half >= s).astype(F32), jnp.exp2))
            bots.append(q_b * k_s * jnp.exp2(g_b - g_s))
        else:
            bots.append(q_b * k_s * _decay(g_b - g_s, (t_half >= s - half).astype(F32), jnp.exp2))
    z = _mm(jnp.concatenate(tops + bots, axis=0), mk.ones_bd)
    z_t, z_b = z[0:half * half], z[half * half:]
    o_t = z_t[0:half] * i_a[0:1, :]
    for s in range(1, half):
        o_t = o_t + z_t[s * half:(s + 1) * half] * i_a[s:s + 1, :]
    o_b = z_b[0:half] * i_a[0:1, :]
    for s in range(1, HG_SUB):
        o_b = o_b + z_b[s * half:(s + 1) * half] * i_a[s:s + 1, :]
    o = jnp.concatenate([o_t, o_b], axis=0)
    if a > 0:
        r = gc[lo - 1:lo, :]
        sc = _mm_nt(q_a * jnp.exp2(g_a - r), _stack_rows(kk[0:lo] * jnp.exp2(r - gc[0:lo]), mk))
        o = o + _mm(sc, _stack_rows(i_in[0:lo], mk))
    return o


def _hgrn_steps(mk, u_ref, lbl_ref, nw_ref, y_ref, st_ref, *, layer):
    logits = lbl_ref[...]
    e = jnp.exp(logits - jnp.max(logits, axis=0, keepdims=True))
    prob = e / jnp.sum(e, axis=0, keepdims=True)
    lb = jnp.sum(prob[0:layer + 1], axis=0, keepdims=True) - prob[0:1]

    chunks = _chunks(u_ref.shape[0])
    f = lb + (1.0 - lb) * _sigmoid(u_ref[:, 256:512])
    kk_all = 1.0 - f
    qq_all = _silu(u_ref[:, 0:256])
    yield
    gc_all = _chunk_cumsum(jnp.log(f)) * LOG2E
    yield

    qq = [qq_all[sl] for sl in chunks]
    kk = [kk_all[sl] for sl in chunks]
    gc = [gc_all[sl] for sl in chunks]
    ii = [u_ref[sl, 512:768] for sl in chunks]
    subs = [[] for _ in chunks]
    for a in range(CHUNK // HG_SUB):
        for c in range(len(chunks)):
            subs[c].append(_hgrn_intra_sub(qq[c], kk[c], gc[c], ii[c], a, mk))
            yield
    o_i = [jnp.concatenate(s, axis=0) for s in subs]
    qe = [q * jnp.exp2(g) for q, g in zip(qq, gc)]
    g_last = [g[CHUNK - 1:CHUNK, :] for g in gc]
    yield
    d_s = [_mm_tn(i, k * jnp.exp2(gl - g)) * mk.bd_f32
           for i, k, g, gl in zip(ii, kk, gc, g_last)]
    decay = [jnp.exp2(gl) for gl in g_last]
    yield

    st = st_ref[...]
    st_at = []
    for i in range(len(chunks)):
        st_at.append(st.astype(BF16))
        st = decay[i] * st + d_s[i]
    st_ref[...] = st
    yield
    o = [oi + _mm_nt(q, s) for oi, q, s in zip(o_i, qe, st_at)]
    yield
    on = [_head_rmsnorm(x, mk) for x in o]
    yield
    nw = nw_ref[...]
    for sl, x in zip(chunks, on):
        y_ref[sl, :] = x * nw * _silu(u_ref[sl, 768:1024])
    yield


def _gdn_steps(mk, u_ref, gb_ref, ga_ref, cw_ref, alog_ref, dtb_ref, nw_ref, y_ref, s_ref, prev_ref):
    rows = u_ref.shape[0]
    chunks = _chunks(rows)
    raw = u_ref[:, 0:768]
    qkv = _silu(_causal_conv(raw, prev_ref[...], cw_ref[...]))
    prev_ref[...] = raw[rows - 8:rows, :]
    yield
    q, k, v = qkv[:, 0:256], qkv[:, 256:512], qkv[:, 512:768]
    qn = q * lax.rsqrt(_mm_hilo(q * q, mk.ones_bd) + EPS) * (HEAD_DIM ** -0.5)
    kn = k * lax.rsqrt(_mm_hilo(k * k, mk.ones_bd) + EPS)
    yield
    gcum = _chunk_cumsum(-jnp.exp(alog_ref[...]) * _softplus(ga_ref[...] + dtb_ref[...]))
    be = _expand4(_sigmoid(gb_ref[...]), mk)
    ge = _expand4(gcum, mk)
    yield
    e_g = jnp.exp(ge)
    kb = kn * be
    vb = v * be
    kbg = kb * e_g
    q_dec = qn * e_g
    eye = jnp.where(mk.diag, 1.0, 0.0)
    yield

    e_dec = [jnp.exp(jnp.minimum(ge[sl] - _rowvec(ge[sl], mk), 0.0)) for sl in chunks]
    yield
    qk = [_mm_nt(jnp.concatenate([kb[sl], qn[sl]], axis=0), _bd(kn[sl], mk)) for sl in chunks]
    yield
    a_p = [x[0:CHUNK] * (e * mk.strict_f) for x, e in zip(qk, e_dec)]
    attn = [x[CHUNK:2 * CHUNK] * (e * mk.causal_f) for x, e in zip(qk, e_dec)]

    pw = [_mm(-x, _bd(-x, mk)) for x in a_p]
    t0 = [eye - x for x in a_p]
    yield
    for j in range(1, 6):
        r = [_mm(jnp.concatenate([t, m], axis=0), _bd(m, mk)) for t, m in zip(t0, pw)]
        t0 = [t + x[0:CHUNK] for t, x in zip(t0, r)]
        pw = [x[CHUNK:2 * CHUNK] for x in r]
        yield
    a_t0 = []
    for x, t in zip(a_p, t0):
        a_hi, a_lo = _split2(x)
        t_hi, t_lo = _split2(t)
        r = jnp.dot(jnp.concatenate([a_hi, a_lo], axis=0), _bd(t_hi, mk).astype(BF16), preferred_element_type=F32)
        a_t0.append(r[0:CHUNK] + r[CHUNK:2 * CHUNK]
                    + jnp.dot(a_hi, _bd(t_lo, mk).astype(BF16), preferred_element_type=F32))
    yield
    t_mat = [t + _mm(t, _bd(eye - t - at, mk)) for t, at in zip(t0, a_t0)]
    yield

    uw = [_mm(t, jnp.concatenate([_bd(vb[sl], mk), _bd(kbg[sl], mk)], axis=1)) for t, sl in zip(t_mat, chunks)]
    u = [x[:, 0:256] for x in uw]
    w = [x[:, 256:512] for x in uw]
    yield
    au = [_mm(at, jnp.concatenate([_bd(uu, mk), _bd(ww, mk)], axis=1)) for at, uu, ww in zip(attn, u, w)]
    o_i = [x[:, 0:256] for x in au]
    q2 = [q_dec[sl] - x[:, 256:512] for sl, x in zip(chunks, au)]
    yield
    last = [slice(sl.stop - 1, sl.stop) for sl in chunks]
    k_dec = [kn[sl] * _expand4(jnp.exp(gcum[ls] - gcum[sl]), mk) for sl, ls in zip(chunks, last)]
    fg = [_mm_tn(kd, jnp.concatenate([ww, uu], axis=1)) for kd, ww, uu in zip(k_dec, w, u)]
    yield
    f_p = [-_unbd(x[:, 0:256], mk) for x in fg]
    g_p = [_unbd(x[:, 256:512], mk) for x in fg]
    decay = [_expand4(jnp.exp(gcum[ls]), mk) for ls in last]
    yield

    s_p = s_ref[...]
    s_at = []
    for i in range(len(chunks)):
        s_bd = _bd(s_p, mk)
        s_at.append(s_bd)
        s_p = decay[i] * s_p + _mm(f_p[i], s_bd) + g_p[i]
        yield
    s_ref[...] = s_p
    o = [oi + _mm(q, s) for oi, q, s in zip(o_i, q2, s_at)]
    yield
    on = [_head_rmsnorm(x, mk) for x in o]
    yield
    nw = nw_ref[...]
    for sl, x in zip(chunks, on):
        y_ref[sl, :] = x * nw * _silu(u_ref[sl, 768:1024])
    yield


_DONE = object()


N_MIXER_PARAMS = 15


def _mixer_steps(um_ref, ug_ref, params, y_ref, states, layer):
    ib, fb, nw_a, cw_b, cb_b, dtb_b, alog_b, dskip_b, nw_b, lbl, nw_c, cw_d, alog_d, dtb_d, nw_d = params
    c_ref, n_ref, m_ref, sb_ref, prevb_ref, st_ref, sd_ref, prevd_ref = states
    mk = _Masks()
    win = lambda j: um_ref.at[:, pl.ds(j * 4 * GROUP_WIDTH, 4 * GROUP_WIDTH)]
    gate = lambda j: ug_ref.at[:, pl.ds(j * LANES, LANES)]
    out = lambda j: y_ref.at[:, pl.ds(j * GROUP_WIDTH, GROUP_WIDTH)]
    return [
        (_gdn_steps(mk, win(3), gate(3), gate(4), cw_d, alog_d, dtb_d, nw_d, out(3), sd_ref, prevd_ref), 1.0),
        (_hgrn_steps(mk, win(2), lbl, nw_c, out(2), st_ref, layer=layer), 1.0),
        (_mlstm_steps(mk, win(0), gate(0), gate(1), ib, fb, nw_a, out(0), c_ref, n_ref, m_ref), MLSTM_PACE),
        (_mamba_steps(mk, win(1), gate(2), cw_b, cb_b, dtb_b, alog_b, dskip_b, nw_b, out(1), sb_ref, prevb_ref),
         MAMBA_PACE),
    ]


def _run_round_robin(steps):
    rnd = 0
    while steps:
        due = [(g, p) for g, p in steps if int((rnd + 1) * p) > int(rnd * p)]
        done = [g for g, _ in due if next(g, _DONE) is _DONE]
        steps = [(g, p) for g, p in steps if g not in done]
        rnd += 1


def _mixers_body(*refs, layer):
    um_ref, ug_ref = refs[0:2]
    params = refs[2:2 + N_MIXER_PARAMS]
    y_ref = refs[2 + N_MIXER_PARAMS]
    states = refs[3 + N_MIXER_PARAMS:]

    @pl.when(pl.program_id(1) == 0)
    def _():
        for r in states:
            r[...] = jnp.zeros_like(r)

    _run_round_robin(_mixer_steps(um_ref, ug_ref, params, y_ref, states, layer))


def _mixer_param_specs(params, layer):
    return [pl.BlockSpec((None,) + p.shape[1:], lambda *_: (layer, 0, 0)) if per_layer
            else pl.BlockSpec(p.shape, lambda *_, nd=p.ndim: (0,) * nd) for p, per_layer in params]


def _mixer_state_shapes():
    state = pltpu.VMEM((GROUP_WIDTH, GROUP_WIDTH), F32)
    conv_tail = pltpu.VMEM((8, 3 * GROUP_WIDTH), F32)
    return [state, pltpu.VMEM((8, GROUP_WIDTH), F32), pltpu.VMEM((8, LANES), F32),
            state, conv_tail,
            state,
            pltpu.VMEM((HEAD_DIM, GROUP_WIDTH), F32), conv_tail]


def _mixers_call(u_main, u_gate, params, layer, seq_len):
    n = u_main.shape[0]
    tb = min(MIX_TOKENS, seq_len)
    tiles = seq_len // tb
    blk = lambda w: pl.BlockSpec((tb, w), lambda bi, i: (bi * tiles + i, 0))
    return pl.pallas_call(
        functools.partial(_mixers_body, layer=layer),
        grid=(n // seq_len, tiles),
        in_specs=[blk(u_main.shape[1]), blk(u_gate.shape[1])] + _mixer_param_specs(params, layer),
        out_specs=blk(N_MIXERS * GROUP_WIDTH),
        out_shape=jax.ShapeDtypeStruct((n, N_MIXERS * GROUP_WIDTH), F32),
        scratch_shapes=_mixer_state_shapes(),
        compiler_params=pltpu.CompilerParams(dimension_semantics=("arbitrary", "arbitrary"),
                                             vmem_limit_bytes=VMEM_LIMIT),
        name="mixers",
    )(u_main, u_gate, *[p for p, _ in params])


def _rows(v):
    return v.astype(F32)[:, None, :]


def _head_rows(v):
    return jnp.pad(_rows(v), ((0, 0), (0, 0), (0, LANES - v.shape[1])))


def _mixer_params(mlstm_i_bias, mlstm_f_bias, mlstm_norm, mamba_conv_w, mamba_conv_b, mamba_dt_bias, mamba_a_log,
                  mamba_d, mamba_norm, hgrn_lb_logits, hgrn_norm, gdn_conv_w, gdn_a_log, gdn_dt_bias, gdn_norm):
    per_layer = [
        _head_rows(mlstm_i_bias), _head_rows(mlstm_f_bias), _rows(mlstm_norm),
        mamba_conv_w.astype(F32), _rows(mamba_conv_b), _head_rows(mamba_dt_bias), _head_rows(mamba_a_log),
        _rows(jnp.repeat(mamba_d, HEAD_DIM, axis=1)), _rows(mamba_norm)]
    hgrn = [(hgrn_lb_logits.astype(F32), False), (_rows(hgrn_norm), True)]
    gdn = [gdn_conv_w.astype(F32), _head_rows(gdn_a_log), _head_rows(gdn_dt_bias), _rows(gdn_norm)]
    return [(p, True) for p in per_layer] + hgrn + [(p, True) for p in gdn]


_MLSTM0, _MAMBA0, _HGRN0, _GDN0 = 0, 1032, 2060, 3084
_MAIN_COLS = ((_MLSTM0, _MLSTM0 + 1024), (_MAMBA0, _MAMBA0 + 1024), (_HGRN0, _HGRN0 + 1024),
              (_GDN0, _GDN0 + 1024))
_GATE_COLS = (_MLSTM0 + 1024, _MLSTM0 + 1028, _MAMBA0 + 1024, _GDN0 + 1024, _GDN0 + 1028)
_MAIN_STARTS = tuple(a for a, _ in _MAIN_COLS)
_D_IN_PAD = 33 * LANES


def kernel(x, ffn1_norm, ffn1_w_gate, ffn1_w_up, ffn1_w_down, mix_norm, w_in, w_out, mlstm_i_bias, mlstm_f_bias, mlstm_norm, mamba_conv_w, mamba_conv_b, mamba_dt_bias, mamba_a_log, mamba_d, mamba_norm, hgrn_lb_logits, hgrn_norm, gdn_conv_w, gdn_a_log, gdn_dt_bias, gdn_norm, ffn2_norm, ffn2_w_gate, ffn2_w_up, ffn2_w_down, final_norm):
    b, s, d = x.shape
    depth = w_in.shape[0]
    x2d = x.reshape(b * s, d)
    ffn1 = (_rows(ffn1_norm), _to_bf16(ffn1_w_gate), _to_bf16(ffn1_w_up), _to_bf16(ffn1_w_down))
    ffn2 = (_rows(ffn2_norm), _to_bf16(ffn2_w_gate), _to_bf16(ffn2_w_up), _to_bf16(ffn2_w_down))
    w_pad = _to_bf16(w_in, _D_IN_PAD)
    w_o = _to_bf16(w_out)
    mix_nw = _rows(mix_norm)
    params = _mixer_params(mlstm_i_bias, mlstm_f_bias, mlstm_norm, mamba_conv_w, mamba_conv_b, mamba_dt_bias,
                           mamba_a_log, mamba_d, mamba_norm, hgrn_lb_logits, hgrn_norm, gdn_conv_w, gdn_a_log,
                           gdn_dt_bias, gdn_norm)
    for l in range(depth):
        x2d = _ffn_call(x2d, *ffn1, l)
        um, ug = _inproj_call(x2d, mix_nw, w_pad, l)
        y = _mixers_call(um, ug, params, l, s)
        x2d = _ffn_call(x2d, *ffn2, l, mix=(y, w_o),
                        final_w=final_norm.astype(F32).reshape(1, d) if l == depth - 1 else None)
    return x2d.reshape(b, s, d)
```

```python
import functools

import numpy as np
import jax
import jax.numpy as jnp
from jax import lax
from jax.experimental import pallas as pl
from jax.experimental.pallas import tpu as pltpu

F32 = jnp.float32
BF16 = jnp.bfloat16

D_MODEL = 1024
CHUNK = 64
N_HEADS = 4
HEAD_DIM = 64
GROUP_WIDTH = N_HEADS * HEAD_DIM
SSM_STATE = 128
CONV_K = 4
D_FF = 2816
EPS = 1e-6
NEG_BIG = -1e30
LOG2E = 1.4426950408889634
LANES = 128
N_GATES = 5

FFN_TOKENS = 512
CAST_STEPS = 2
W_PREP_STEPS = 8
MXU_DIM = 256
FF_SPLIT = 6 * MXU_DIM
N_MIXERS = 4
MLSTM_PACE = 0.7
MAMBA_PACE = 0.55
MIX_TOKENS = 256
VMEM_LIMIT = 56 * 1024 * 1024


def _iota(shape, dim):
    return lax.broadcasted_iota(jnp.int32, shape, dim)


def _mm(a, b):
    return jnp.dot(a.astype(BF16), b.astype(BF16), preferred_element_type=F32)


def _mm_nt(a, b):
    return lax.dot_general(a.astype(BF16), b.astype(BF16), (((1,), (1,)), ((), ())),
                           preferred_element_type=F32)


def _mm_tn(a, b):
    return lax.dot_general(a.astype(BF16), b.astype(BF16), (((0,), (0,)), ((), ())),
                           preferred_element_type=F32)


def _split2(a):
    hi = a.astype(BF16)
    return hi, (a - hi.astype(F32)).astype(BF16)


def _mm_hilo(a, b_bf16):
    hi, lo = _split2(a)
    return (jnp.dot(hi, b_bf16, preferred_element_type=F32)
            + jnp.dot(lo, b_bf16, preferred_element_type=F32))


def _sigmoid(x):
    return 0.5 * jnp.tanh(0.5 * x) + 0.5


def _silu(x):
    h = 0.5 * x
    return h * jnp.tanh(h) + h


def _softplus(x):
    return jnp.maximum(x, 0.0) + jnp.log1p(jnp.exp(-jnp.abs(x)))


class _Masks:
    def __init__(self):
        row = _iota((GROUP_WIDTH, 1), 0)
        lane = _iota((1, GROUP_WIDTH), 1)
        t = _iota((CHUNK, 1), 0)
        self.row, self.lane = row, lane
        self.lane_head = lane >> 6
        block_diag = (row >> 6) == self.lane_head
        self.ones_bd = block_diag.astype(BF16)
        self.bd_f32 = block_diag.astype(F32)
        s = lane & 63
        self.causal_f = (t >= s).astype(F32)
        self.strict_f = (t > s).astype(F32)
        self.diag = t == s


def _bd(x, mk):
    x = x.astype(BF16)
    return jnp.concatenate([x, x, x, x], axis=0) * mk.ones_bd


def _unbd(y, mk):
    ym = y * mk.bd_f32
    return ym[0:64] + ym[64:128] + ym[128:192] + ym[192:256]


def _decay(arg, mask_f, exp=jnp.exp):
    return exp(jnp.minimum(arg, 0.0)) * mask_f


def _expand4(g, mk):
    first = _iota((1, LANES), 1) < HEAD_DIM
    return jnp.concatenate([jnp.where(first, g[:, 0:1], g[:, 1:2]), jnp.where(first, g[:, 2:3], g[:, 3:4])], axis=1)


def _rowvec(ge, mk):
    return jnp.sum(jnp.where(mk.diag, ge, 0.0), axis=0, keepdims=True)


def _chunk_cumsum(x):
    rows = x.shape[0]
    r, c = _iota((rows, rows), 0), _iota((rows, rows), 1)
    tril = (((r >> 6) == (c >> 6)) & (r >= c)).astype(BF16)
    x1 = x.astype(BF16)
    r1 = x - x1.astype(F32)
    x2 = r1.astype(BF16)
    x3 = (r1 - x2.astype(F32)).astype(BF16)
    return (jnp.dot(tril, x1, preferred_element_type=F32) + jnp.dot(tril, x2, preferred_element_type=F32)
            + jnp.dot(tril, x3, preferred_element_type=F32))


def _chunk_cummax(x):
    t = _iota((x.shape[0], 1), 0) & 63
    k = 1
    while k < CHUNK:
        x = jnp.maximum(x, jnp.where(t >= k, pltpu.roll(x, k, 0), NEG_BIG))
        k *= 2
    return x


def _causal_conv(x, prev8, w):
    acc = x * w[CONV_K - 1:CONV_K, :]
    r8 = _iota((8, 1), 0)
    for j in range(1, CONV_K):
        xr = pltpu.roll(x, j, 0)
        top = jnp.where(r8 < j, pltpu.roll(prev8, j, 0), xr[0:8])
        xs = jnp.concatenate([top, xr[8:]], axis=0)
        acc = acc + xs * w[CONV_K - 1 - j:CONV_K - j, :]
    return acc


def _head_rmsnorm(o, mk):
    return o * lax.rsqrt(_mm_hilo(o * o, mk.ones_bd) * (1.0 / HEAD_DIM) + EPS)


def _chunks(n_rows):
    return [slice(c * CHUNK, (c + 1) * CHUNK) for c in range(n_rows // CHUNK)]


def _rms(x, w):
    return x * lax.rsqrt(jnp.mean(x * x, axis=-1, keepdims=True) + EPS) * w


def _ffn_body(*refs, with_mix, with_final):
    it = iter(refs)
    x_ref = next(it)
    if with_mix:
        y_ref, wo_ref = next(it), next(it)
    nw_ref, wg_ref, wu_ref, wd_ref = next(it), next(it), next(it), next(it)
    if with_final:
        fw_ref = next(it)
    o_ref = next(it)
    wg_s, wu_s, wd_s = next(it), next(it), next(it)
    step = pl.program_id(0)

    @pl.when(step < W_PREP_STEPS)
    def _():
        for src, dst in ((wg_ref, wg_s), (wu_ref, wu_s), (wd_ref, wd_s)):
            rows = src.shape[0]
            dst[pl.ds(pl.multiple_of(step * rows, rows), rows), :] = src[...].astype(BF16)

    @pl.when(step >= W_PREP_STEPS)
    def _():
        x = x_ref[...]
        if with_mix:
            x = x + jnp.dot(y_ref[...].astype(BF16), wo_ref[...], preferred_element_type=F32)
        xb = _rms(x, nw_ref[...]).astype(BF16)
        acc = jnp.zeros(x.shape, F32)
        for sl in (slice(0, FF_SPLIT), slice(FF_SPLIT, D_FF)):
            g = jnp.dot(xb, wg_s[:, sl], preferred_element_type=F32)
            u = jnp.dot(xb, wu_s[:, sl], preferred_element_type=F32)
            h = (_silu(g) * u).astype(BF16)
            acc = acc + jnp.dot(h, wd_s[sl, :], preferred_element_type=F32)
        out = x + 0.5 * acc
        if with_final:
            out = _rms(out, fw_ref[...])
        o_ref[...] = out


def _const_spec(shape):
    return pl.BlockSpec(shape, lambda *_: (0,) * len(shape), pipeline_mode=pl.Buffered(1))


def _layer_spec(a, layer):
    return pl.BlockSpec((None,) + a.shape[1:], lambda *_: (layer, 0, 0), pipeline_mode=pl.Buffered(1))


def _cast_body(w_ref, o_ref):
    cols = w_ref.shape[1]
    if o_ref.shape[1] != cols:
        o_ref[...] = jnp.zeros(o_ref.shape, BF16)
    o_ref[:, 0:cols] = w_ref[...].astype(BF16)


def _to_bf16(w, cols_out=None):
    depth, rows, cols = w.shape
    cols_out = cols_out or cols
    tr = rows // CAST_STEPS
    return pl.pallas_call(
        _cast_body,
        grid=(depth, CAST_STEPS),
        in_specs=[pl.BlockSpec((None, tr, cols), lambda l, i: (l, i, 0))],
        out_specs=pl.BlockSpec((None, tr, cols_out), lambda l, i: (l, i, 0)),
        out_shape=jax.ShapeDtypeStruct((depth, rows, cols_out), BF16),
        compiler_params=pltpu.CompilerParams(dimension_semantics=("arbitrary", "arbitrary"),
                                             vmem_limit_bytes=VMEM_LIMIT),
        name="cast",
    )(w)


def _ffn_call(x2d, nw, wg, wu, wd, layer, mix=None, final_w=None):
    n = x2d.shape[0]
    tm = min(FFN_TOKENS, n)
    tok = lambda w: pl.BlockSpec((tm, w), lambda i: (jnp.maximum(i - W_PREP_STEPS, 0), 0))
    w_rows = lambda w: pl.BlockSpec((None, w.shape[1] // W_PREP_STEPS, w.shape[2]),
                                    lambda i: (layer, jnp.minimum(i, W_PREP_STEPS - 1), 0))
    args, specs = [x2d], [tok(D_MODEL)]
    if mix is not None:
        y, wo = mix
        args += [y, wo]
        specs += [tok(y.shape[1]), _layer_spec(wo, layer)]
    args += [nw, wg, wu, wd]
    specs += [_layer_spec(nw, layer), w_rows(wg), w_rows(wu), w_rows(wd)]
    if final_w is not None:
        args.append(final_w)
        specs.append(_const_spec(final_w.shape))
    return pl.pallas_call(
        functools.partial(_ffn_body, with_mix=mix is not None, with_final=final_w is not None),
        grid=(W_PREP_STEPS + n // tm,),
        in_specs=specs,
        out_specs=tok(D_MODEL),
        out_shape=jax.ShapeDtypeStruct((n, D_MODEL), F32),
        scratch_shapes=[pltpu.VMEM(w.shape[1:], BF16) for w in (wg, wu, wd)],
        compiler_params=pltpu.CompilerParams(dimension_semantics=("arbitrary",),
                                             vmem_limit_bytes=VMEM_LIMIT),
        name="ffn",
    )(*args)


def _inproj_body(x_ref, nw_ref, w_ref, um_ref, ug_ref):
    xb = _rms(x_ref[...], nw_ref[...]).astype(BF16)
    u = jnp.dot(xb, w_ref[...], preferred_element_type=F32)
    for j, start in enumerate(_MAIN_STARTS):
        lo = (start // LANES) * LANES
        width = 4 * GROUP_WIDTH
        if start == lo:
            blk = u[:, lo:lo + width]
        else:
            win = u[:, lo:lo + width + LANES]
            blk = pltpu.roll(win, width + LANES - (start - lo), 1)[:, 0:width]
        um_ref[:, j * width:(j + 1) * width] = blk
    head_lane = _iota((1, LANES), 1) < N_HEADS
    for j, start in enumerate(_GATE_COLS):
        lo = (start // LANES) * LANES
        tile = u[:, lo:lo + LANES]
        if start != lo:
            tile = pltpu.roll(tile, LANES - (start - lo), 1)
        ug_ref[:, j * LANES:(j + 1) * LANES] = jnp.where(head_lane, tile, 0.0)


def _inproj_call(x2d, nw, w_pad, layer):
    n = x2d.shape[0]
    tm = min(FFN_TOKENS, n)
    wm, wg = 4 * 4 * GROUP_WIDTH, N_GATES * LANES
    return pl.pallas_call(
        _inproj_body,
        grid=(n // tm,),
        in_specs=[pl.BlockSpec((tm, D_MODEL), lambda i: (i, 0)), _layer_spec(nw, layer), _layer_spec(w_pad, layer)],
        out_specs=[pl.BlockSpec((tm, wm), lambda i: (i, 0)), pl.BlockSpec((tm, wg), lambda i: (i, 0))],
        out_shape=[jax.ShapeDtypeStruct((n, wm), F32), jax.ShapeDtypeStruct((n, wg), F32)],
        compiler_params=pltpu.CompilerParams(dimension_semantics=("arbitrary",),
                                             vmem_limit_bytes=VMEM_LIMIT),
        name="inproj",
    )(x2d, nw, w_pad)


def _mlstm_steps(mk, u_ref, gi_ref, gf_ref, ib_ref, fb_ref, nw_ref, y_ref, c_ref, n_ref, m_ref):
    chunks = _chunks(u_ref.shape[0])
    li = gi_ref[...] + ib_ref[...]
    xf = gf_ref[...] + fb_ref[...]
    lf = jnp.minimum(xf, 0.0) - jnp.log1p(jnp.exp(-jnp.abs(xf)))
    bcum = _chunk_cumsum(lf)
    a = li - bcum
    yield
    cmax = _chunk_cummax(a)
    yield

    m_run = m_ref[0:1, :]
    m_start, m_tot = [], []
    for sl in chunks:
        last = slice(sl.stop - 1, sl.stop)
        m_start.append(m_run)
        m_tot.append(jnp.maximum(m_run, cmax[last]))
        m_run = bcum[last] + m_tot[-1]
    m_ref[0:1, :] = m_run
    yield

    qs = [u_ref[sl, 0:256] * (HEAD_DIM ** -0.5) for sl in chunks]
    ks = [u_ref[sl, 256:512] for sl in chunks]
    vs = [u_ref[sl, 512:768] for sl in chunks]
    mx = [jnp.maximum(m0, cmax[sl]) for m0, sl in zip(m_start, chunks)]
    arow = [_rowvec(_expand4(a[sl], mk), mk) for sl in chunks]
    yield
    dmat = [_decay(ar - _expand4(m, mk), mk.causal_f) for ar, m in zip(arow, mx)]
    yield
    p = [_mm_nt(q, _bd(k, mk)) * d for q, k, d in zip(qs, ks, dmat)]
    yield
    num_i = [_mm(pp, _bd(v, mk)) for pp, v in zip(p, vs)]
    yield
    den_i = [_mm_hilo(pp, mk.ones_bd) for pp in p]
    yield
    w_int = [_expand4(jnp.exp(m0 - m), mk) for m0, m in zip(m_start, mx)]
    bound = [_expand4(jnp.exp(-(bcum[sl] + m)), mk) for sl, m in zip(chunks, mx)]
    yield
    kw = [k * _expand4(jnp.exp(a[sl] - mt), mk) for k, sl, mt in zip(ks, chunks, m_tot)]
    decay = [_expand4(jnp.exp(m0 - mt), mk) for m0, mt in zip(m_start, m_tot)]
    yield
    d_c = [_mm_tn(kk, v) * mk.bd_f32 for kk, v in zip(kw, vs)]
    d_n = [jnp.sum(kk, axis=0, keepdims=True) for kk in kw]
    yield

    c_bd, n_row = c_ref[...], n_ref[0:1, :]
    c_at, n_at = [], []
    for i in range(len(chunks)):
        c_at.append(c_bd.astype(BF16))
        n_at.append(n_row)
        c_bd = decay[i] * c_bd + d_c[i]
        n_row = decay[i] * n_row + d_n[i]
    c_ref[...] = c_bd
    n_ref[0:1, :] = n_row
    yield
    q_c = [_mm(q, c) for q, c in zip(qs, c_at)]
    yield
    q_n = [_mm_hilo(q * n, mk.ones_bd) for q, n in zip(qs, n_at)]
    yield
    h = [(ni + w * qc) / jnp.maximum(jnp.abs(di + w * qn), b)
         for ni, di, w, qc, qn, b in zip(num_i, den_i, w_int, q_c, q_n, bound)]
    yield
    hn = [_head_rmsnorm(x, mk) for x in h]
    yield
    nw = nw_ref[...]
    for sl, x in zip(chunks, hn):
        y_ref[sl, :] = x * nw * _sigmoid(u_ref[sl, 768:1024])
    yield


def _mamba_steps(mk, u_ref, gdt_ref, cw_ref, cb_ref, dtb_ref, alog_ref, dskip_ref, nw_ref, y_ref, s_ref, prev_ref):
    rows = u_ref.shape[0]
    chunks = _chunks(rows)
    raw = u_ref[:, 256:1024]
    xbc = _silu(_causal_conv(raw, prev_ref[...], cw_ref[...]) + cb_ref[...])
    prev_ref[...] = raw[rows - 8:rows, :]
    yield
    dt = _softplus(gdt_ref[...] + dtb_ref[...])
    cum = _chunk_cumsum(dt * (-jnp.exp(alog_ref[...])))
    grp_b = (((mk.row >> 6) >> 1) == (mk.lane >> 7)).astype(BF16)
    grp_s = ((mk.row >> 7) == (mk.lane_head >> 1)).astype(F32)
    yield

    xs = [xbc[sl, 0:256] for sl in chunks]
    bm = [xbc[sl, 256:512] for sl in chunks]
    cm = [xbc[sl, 512:768] for sl in chunks]
    cum_e = [_expand4(cum[sl], mk) for sl in chunks]
    yield
    seg = [_decay(ce - _rowvec(ce, mk), mk.causal_f) for ce in cum_e]
    yield
    dtrow = [_rowvec(_expand4(dt[sl], mk), mk) for sl in chunks]
    yield
    p = [_mm_nt(c, jnp.concatenate([b.astype(BF16)] * 4, axis=0) * grp_b) * sg * dr
         for c, b, sg, dr in zip(cm, bm, seg, dtrow)]
    yield
    y_i = [_mm(pp, _bd(x, mk)) + x * dskip_ref[...] for pp, x in zip(p, xs)]
    yield
    last = [slice(sl.stop - 1, sl.stop) for sl in chunks]
    xw = [x * _expand4(jnp.exp(cum[ls] - cum[sl]) * dt[sl], mk) for x, sl, ls in zip(xs, chunks, last)]
    yield
    d_s = [_mm_tn(b, w) * grp_s for b, w in zip(bm, xw)]
    yield
    decay = [_expand4(jnp.exp(cum[ls]), mk) for ls in last]
    e_cum = [jnp.exp(ce) for ce in cum_e]
    yield

    s_mat = s_ref[...]
    s_at = []
    for i in range(len(chunks)):
        s_at.append(s_mat.astype(BF16))
        s_mat = decay[i] * s_mat + d_s[i]
    s_ref[...] = s_mat
    yield
    y = [yi + e * _mm(c, s) for yi, e, c, s in zip(y_i, e_cum, cm, s_at)]
    yield
    nw = nw_ref[...]
    for sl, yc in zip(chunks, y):
        yz = yc * _silu(u_ref[sl, 0:256])
        halves = []
        for g in range(2):
            yg = yz[:, g * 128:(g + 1) * 128]
            halves.append(yg * lax.rsqrt(jnp.mean(yg * yg, axis=-1, keepdims=True) + EPS))
        y_ref[sl, :] = jnp.concatenate(halves, axis=-1) * nw
    yield


HG_SUB = 16


def _stack_rows(x, mk):
    n = x.shape[0]
    r = _iota((4 * n, 1), 0)
    blk = (r >= n).astype(jnp.int32) + (r >= 2 * n).astype(jnp.int32) + (r >= 3 * n).astype(jnp.int32)
    x = x.astype(BF16)
    return jnp.where(blk == mk.lane_head, jnp.concatenate([x, x, x, x], axis=0), jnp.zeros((), BF16))


def _hgrn_intra_sub(qq, kk, gc, i_in, a, mk):
    half = HG_SUB // 2
    t_half = _iota((half, 1), 0)
    lo = a * HG_SUB
    q_a, k_a, g_a, i_a = qq[lo:lo + HG_SUB], kk[lo:lo + HG_SUB], gc[lo:lo + HG_SUB], i_in[lo:lo + HG_SUB]
    q_t, q_b, g_t, g_b = q_a[0:half], q_a[half:], g_a[0:half], g_a[half:]
    tops, bots = [], []
    for s in range(HG_SUB):
        k_s, g_s = k_a[s:s + 1, :], g_a[s:s + 1, :]
        if s < half:
            tops.append(q_t * k_s * _decay(g_t - g_s, (t_half >= s).astype(F32), jnp.exp2))
            bots.append(q_b * k_s * jnp.exp2(g_b - g_s))
        else:
            bots.append(q_b * k_s * _decay(g_b - g_s, (t_half >= s - half).astype(F32), jnp.exp2))
    z = _mm(jnp.concatenate(tops + bots, axis=0), mk.ones_bd)
    z_t, z_b = z[0:half * half], z[half * half:]
    o_t = z_t[0:half] * i_a[0:1, :]
    for s in range(1, half):
        o_t = o_t + z_t[s * half:(s + 1) * half] * i_a[s:s + 1, :]
    o_b = z_b[0:half] * i_a[0:1, :]
    for s in range(1, HG_SUB):
        o_b = o_b + z_b[s * half:(s + 1) * half] * i_a[s:s + 1, :]
    o = jnp.concatenate([o_t, o_b], axis=0)
    if a > 0:
        r = gc[lo - 1:lo, :]
        sc = _mm_nt(q_a * jnp.exp2(g_a - r), _stack_rows(kk[0:lo] * jnp.exp2(r - gc[0:lo]), mk))
        o = o + _mm(sc, _stack_rows(i_in[0:lo], mk))
    return o


def _hgrn_steps(mk, u_ref, lbl_ref, nw_ref, y_ref, st_ref, *, layer):
    logits = lbl_ref[...]
    e = jnp.exp(logits - jnp.max(logits, axis=0, keepdims=True))
    prob = e / jnp.sum(e, axis=0, keepdims=True)
    lb = jnp.sum(prob[0:layer + 1], axis=0, keepdims=True) - prob[0:1]

    chunks = _chunks(u_ref.shape[0])
    f = lb + (1.0 - lb) * _sigmoid(u_ref[:, 256:512])
    kk_all = 1.0 - f
    qq_all = _silu(u_ref[:, 0:256])
    yield
    gc_all = _chunk_cumsum(jnp.log(f)) * LOG2E
    yield

    qq = [qq_all[sl] for sl in chunks]
    kk = [kk_all[sl] for sl in chunks]
    gc = [gc_all[sl] for sl in chunks]
    ii = [u_ref[sl, 512:768] for sl in chunks]
    subs = [[] for _ in chunks]
    for a in range(CHUNK // HG_SUB):
        for c in range(len(chunks)):
            subs[c].append(_hgrn_intra_sub(qq[c], kk[c], gc[c], ii[c], a, mk))
            yield
    o_i = [jnp.concatenate(s, axis=0) for s in subs]
    qe = [q * jnp.exp2(g) for q, g in zip(qq, gc)]
    g_last = [g[CHUNK - 1:CHUNK, :] for g in gc]
    yield
    d_s = [_mm_tn(i, k * jnp.exp2(gl - g)) * mk.bd_f32
           for i, k, g, gl in zip(ii, kk, gc, g_last)]
    decay = [jnp.exp2(gl) for gl in g_last]
    yield

    st = st_ref[...]
    st_at = []
    for i in range(len(chunks)):
        st_at.append(st.astype(BF16))
        st = decay[i] * st + d_s[i]
    st_ref[...] = st
    yield
    o = [oi + _mm_nt(q, s) for oi, q, s in zip(o_i, qe, st_at)]
    yield
    on = [_head_rmsnorm(x, mk) for x in o]
    yield
    nw = nw_ref[...]
    for sl, x in zip(chunks, on):
        y_ref[sl, :] = x * nw * _silu(u_ref[sl, 768:1024])
    yield


def _gdn_steps(mk, u_ref, gb_ref, ga_ref, cw_ref, alog_ref, dtb_ref, nw_ref, y_ref, s_ref, prev_ref):
    rows = u_ref.shape[0]
    chunks = _chunks(rows)
    raw = u_ref[:, 0:768]
    qkv = _silu(_causal_conv(raw, prev_ref[...], cw_ref[...]))
    prev_ref[...] = raw[rows - 8:rows, :]
    yield
    q, k, v = qkv[:, 0:256], qkv[:, 256:512], qkv[:, 512:768]
    qn = q * lax.rsqrt(_mm_hilo(q * q, mk.ones_bd) + EPS) * (HEAD_DIM ** -0.5)
    kn = k * lax.rsqrt(_mm_hilo(k * k, mk.ones_bd) + EPS)
    yield
    gcum = _chunk_cumsum(-jnp.exp(alog_ref[...]) * _softplus(ga_ref[...] + dtb_ref[...]))
    be = _expand4(_sigmoid(gb_ref[...]), mk)
    ge = _expand4(gcum, mk)
    yield
    e_g = jnp.exp(ge)
    kb = kn * be
    vb = v * be
    kbg = kb * e_g
    q_dec = qn * e_g
    eye = jnp.where(mk.diag, 1.0, 0.0)
    yield

    e_dec = [jnp.exp(jnp.minimum(ge[sl] - _rowvec(ge[sl], mk), 0.0)) for sl in chunks]
    yield
    qk = [_mm_nt(jnp.concatenate([kb[sl], qn[sl]], axis=0), _bd(kn[sl], mk)) for sl in chunks]
    yield
    a_p = [x[0:CHUNK] * (e * mk.strict_f) for x, e in zip(qk, e_dec)]
    attn = [x[CHUNK:2 * CHUNK] * (e * mk.causal_f) for x, e in zip(qk, e_dec)]

    pw = [_mm(-x, _bd(-x, mk)) for x in a_p]
    t0 = [eye - x for x in a_p]
    yield
    for j in range(1, 6):
        r = [_mm(jnp.concatenate([t, m], axis=0), _bd(m, mk)) for t, m in zip(t0, pw)]
        t0 = [t + x[0:CHUNK] for t, x in zip(t0, r)]
        pw = [x[CHUNK:2 * CHUNK] for x in r]
        yield
    a_t0 = []
    for x, t in zip(a_p, t0):
        a_hi, a_lo = _split2(x)
        t_hi, t_lo = _split2(t)
        r = jnp.dot(jnp.concatenate([a_hi, a_lo], axis=0), _bd(t_hi, mk).astype(BF16), preferred_element_type=F32)
        a_t0.append(r[0:CHUNK] + r[CHUNK:2 * CHUNK]
                    + jnp.dot(a_hi, _bd(t_lo, mk).astype(BF16), preferred_element_type=F32))
    yield
    t_mat = [t + _mm(t, _bd(eye - t - at, mk)) for t, at in zip(t0, a_t0)]
    yield

    uw = [_mm(t, jnp.concatenate([_bd(vb[sl], mk), _bd(kbg[sl], mk)], axis=1)) for t, sl in zip(t_mat, chunks)]
    u = [x[:, 0:256] for x in uw]
    w = [x[:, 256:512] for x in uw]
    yield
    au = [_mm(at, jnp.concatenate([_bd(uu, mk), _bd(ww, mk)], axis=1)) for at, uu, ww in zip(attn, u, w)]
    o_i = [x[:, 0:256] for x in au]
    q2 = [q_dec[sl] - x[:, 256:512] for sl, x in zip(chunks, au)]
    yield
    last = [slice(sl.stop - 1, sl.stop) for sl in chunks]
    k_dec = [kn[sl] * _expand4(jnp.exp(gcum[ls] - gcum[sl]), mk) for sl, ls in zip(chunks, last)]
    fg = [_mm_tn(kd, jnp.concatenate([ww, uu], axis=1)) for kd, ww, uu in zip(k_dec, w, u)]
    yield
    f_p = [-_unbd(x[:, 0:256], mk) for x in fg]
    g_p = [_unbd(x[:, 256:512], mk) for x in fg]
    decay = [_expand4(jnp.exp(gcum[ls]), mk) for ls in last]
    yield

    s_p = s_ref[...]
    s_at = []
    for i in range(len(chunks)):
        s_bd = _bd(s_p, mk)
        s_at.append(s_bd)
        s_p = decay[i] * s_p + _mm(f_p[i], s_bd) + g_p[i]
        yield
    s_ref[...] = s_p
    o = [oi + _mm(q, s) for oi, q, s in zip(o_i, q2, s_at)]
    yield
    on = [_head_rmsnorm(x, mk) for x in o]
    yield
    nw = nw_ref[...]
    for sl, x in zip(chunks, on):
        y_ref[sl, :] = x * nw * _silu(u_ref[sl, 768:1024])
    yield


_DONE = object()


N_MIXER_PARAMS = 15


def _mixer_steps(um_ref, ug_ref, params, y_ref, states, layer):
    ib, fb, nw_a, cw_b, cb_b, dtb_b, alog_b, dskip_b, nw_b, lbl, nw_c, cw_d, alog_d, dtb_d, nw_d = params
    c_ref, n_ref, m_ref, sb_ref, prevb_ref, st_ref, sd_ref, prevd_ref = states
    mk = _Masks()
    win = lambda j: um_ref.at[:, pl.ds(j * 4 * GROUP_WIDTH, 4 * GROUP_WIDTH)]
    gate = lambda j: ug_ref.at[:, pl.ds(j * LANES, LANES)]
    out = lambda j: y_ref.at[:, pl.ds(j * GROUP_WIDTH, GROUP_WIDTH)]
    return [
        (_gdn_steps(mk, win(3), gate(3), gate(4), cw_d, alog_d, dtb_d, nw_d, out(3), sd_ref, prevd_ref), 1.0),
        (_hgrn_steps(mk, win(2), lbl, nw_c, out(2), st_ref, layer=layer), 1.0),
        (_mlstm_steps(mk, win(0), gate(0), gate(1), ib, fb, nw_a, out(0), c_ref, n_ref, m_ref), MLSTM_PACE),
        (_mamba_steps(mk, win(1), gate(2), cw_b, cb_b, dtb_b, alog_b, dskip_b, nw_b, out(1), sb_ref, prevb_ref),
         MAMBA_PACE),
    ]


def _run_round_robin(steps):
    rnd = 0
    while steps:
        due = [(g, p) for g, p in steps if int((rnd + 1) * p) > int(rnd * p)]
        done = [g for g, _ in due if next(g, _DONE) is _DONE]
        steps = [(g, p) for g, p in steps if g not in done]
        rnd += 1


def _mixers_body(*refs, layer):
    um_ref, ug_ref = refs[0:2]
    params = refs[2:2 + N_MIXER_PARAMS]
    y_ref = refs[2 + N_MIXER_PARAMS]
    states = refs[3 + N_MIXER_PARAMS:]

    @pl.when(pl.program_id(1) == 0)
    def _():
        for r in states:
            r[...] = jnp.zeros_like(r)

    _run_round_robin(_mixer_steps(um_ref, ug_ref, params, y_ref, states, layer))


def _mixer_param_specs(params, layer):
    return [pl.BlockSpec((None,) + p.shape[1:], lambda *_: (layer, 0, 0)) if per_layer
            else pl.BlockSpec(p.shape, lambda *_, nd=p.ndim: (0,) * nd) for p, per_layer in params]


def _mixer_state_shapes():
    state = pltpu.VMEM((GROUP_WIDTH, GROUP_WIDTH), F32)
    conv_tail = pltpu.VMEM((8, 3 * GROUP_WIDTH), F32)
    return [state, pltpu.VMEM((8, GROUP_WIDTH), F32), pltpu.VMEM((8, LANES), F32),
            state, conv_tail,
            state,
            pltpu.VMEM((HEAD_DIM, GROUP_WIDTH), F32), conv_tail]


def _mixers_call(u_main, u_gate, params, layer, seq_len):
    n = u_main.shape[0]
    tb = min(MIX_TOKENS, seq_len)
    tiles = seq_len // tb
    blk = lambda w: pl.BlockSpec((tb, w), lambda bi, i: (bi * tiles + i, 0))
    return pl.pallas_call(
        functools.partial(_mixers_body, layer=layer),
        grid=(n // seq_len, tiles),
        in_specs=[blk(u_main.shape[1]), blk(u_gate.shape[1])] + _mixer_param_specs(params, layer),
        out_specs=blk(N_MIXERS * GROUP_WIDTH),
        out_shape=jax.ShapeDtypeStruct((n, N_MIXERS * GROUP_WIDTH), F32),
        scratch_shapes=_mixer_state_shapes(),
        compiler_params=pltpu.CompilerParams(dimension_semantics=("arbitrary", "arbitrary"),
                                             vmem_limit_bytes=VMEM_LIMIT),
        name="mixers",
    )(u_main, u_gate, *[p for p, _ in params])


def _rows(v):
    return v.astype(F32)[:, None, :]


def _head_rows(v):
    return jnp.pad(_rows(v), ((0, 0), (0, 0), (0, LANES - v.shape[1])))


def _mixer_params(mlstm_i_bias, mlstm_f_bias, mlstm_norm, mamba_conv_w, mamba_conv_b, mamba_dt_bias, mamba_a_log,
                  mamba_d, mamba_norm, hgrn_lb_logits, hgrn_norm, gdn_conv_w, gdn_a_log, gdn_dt_bias, gdn_norm):
    per_layer = [
        _head_rows(mlstm_i_bias), _head_rows(mlstm_f_bias), _rows(mlstm_norm),
        mamba_conv_w.astype(F32), _rows(mamba_conv_b), _head_rows(mamba_dt_bias), _head_rows(mamba_a_log),
        _rows(jnp.repeat(mamba_d, HEAD_DIM, axis=1)), _rows(mamba_norm)]
    hgrn = [(hgrn_lb_logits.astype(F32), False), (_rows(hgrn_norm), True)]
    gdn = [gdn_conv_w.astype(F32), _head_rows(gdn_a_log), _head_rows(gdn_dt_bias), _rows(gdn_norm)]
    return [(p, True) for p in per_layer] + hgrn + [(p, True) for p in gdn]


_MLSTM0, _MAMBA0, _HGRN0, _GDN0 = 0, 1032, 2060, 3084
_MAIN_COLS = ((_MLSTM0, _MLSTM0 + 1024), (_MAMBA0, _MAMBA0 + 1024), (_HGRN0, _HGRN0 + 1024),
              (_GDN0, _GDN0 + 1024))
_GATE_COLS = (_MLSTM0 + 1024, _MLSTM0 + 1028, _MAMBA0 + 1024, _GDN0 + 1024, _GDN0 + 1028)
_MAIN_STARTS = tuple(a for a, _ in _MAIN_COLS)
_D_IN_PAD = 33 * LANES


def kernel(x, ffn1_norm, ffn1_w_gate, ffn1_w_up, ffn1_w_down, mix_norm, w_in, w_out, mlstm_i_bias, mlstm_f_bias, mlstm_norm, mamba_conv_w, mamba_conv_b, mamba_dt_bias, mamba_a_log, mamba_d, mamba_norm, hgrn_lb_logits, hgrn_norm, gdn_conv_w, gdn_a_log, gdn_dt_bias, gdn_norm, ffn2_norm, ffn2_w_gate, ffn2_w_up, ffn2_w_down, final_norm):
    b, s, d = x.shape
    depth = w_in.shape[0]
    x2d = x.reshape(b * s, d)
    ffn1 = (_rows(ffn1_norm), ffn1_w_gate, ffn1_w_up, ffn1_w_down)
    ffn2 = (_rows(ffn2_norm), ffn2_w_gate, ffn2_w_up, ffn2_w_down)
    w_pad = _to_bf16(w_in, _D_IN_PAD)
    w_o = _to_bf16(w_out)
    mix_nw = _rows(mix_norm)
    params = _mixer_params(mlstm_i_bias, mlstm_f_bias, mlstm_norm, mamba_conv_w, mamba_conv_b, mamba_dt_bias,
                           mamba_a_log, mamba_d, mamba_norm, hgrn_lb_logits, hgrn_norm, gdn_conv_w, gdn_a_log,
                           gdn_dt_bias, gdn_norm)
    for l in range(depth):
        x2d = _ffn_call(x2d, *ffn1, l)
        um, ug = _inproj_call(x2d, mix_nw, w_pad, l)
        y = _mixers_call(um, ug, params, l, s)
        x2d = _ffn_call(x2d, *ffn2, l, mix=(y, w_o),
                        final_w=final_norm.astype(F32).reshape(1, d) if l == depth - 1 else None)
    return x2d.reshape(b, s, d)
```

```python
import functools

import numpy as np
import jax
import jax.numpy as jnp
from jax import lax
from jax.experimental import pallas as pl
from jax.experimental.pallas import tpu as pltpu

F32 = jnp.float32
BF16 = jnp.bfloat16

D_MODEL = 1024
CHUNK = 64
N_HEADS = 4
HEAD_DIM = 64
GROUP_WIDTH = N_HEADS * HEAD_DIM
SSM_STATE = 128
CONV_K = 4
D_FF = 2816
EPS = 1e-6
NEG_BIG = -1e30
LOG2E = 1.4426950408889634
LANES = 128
N_GATES = 5

FFN_TOKENS = 512
CAST_STEPS = 2
W_PREP_STEPS = 8
MXU_DIM = 256
FF_SPLIT = 6 * MXU_DIM
N_MIXERS = 4
MLSTM_PACE = 0.7
MAMBA_PACE = 0.55
MIX_TOKENS = 256
VMEM_LIMIT = 56 * 1024 * 1024


def _iota(shape, dim):
    return lax.broadcasted_iota(jnp.int32, shape, dim)


def _mm(a, b):
    return jnp.dot(a.astype(BF16), b.astype(BF16), preferred_element_type=F32)


def _mm_nt(a, b):
    return lax.dot_general(a.astype(BF16), b.astype(BF16), (((1,), (1,)), ((), ())),
                           preferred_element_type=F32)


def _mm_tn(a, b):
    return lax.dot_general(a.astype(BF16), b.astype(BF16), (((0,), (0,)), ((), ())),
                           preferred_element_type=F32)


def _split2(a):
    hi = a.astype(BF16)
    return hi, (a - hi.astype(F32)).astype(BF16)


def _mm_hilo(a, b_bf16):
    hi, lo = _split2(a)
    return (jnp.dot(hi, b_bf16, preferred_element_type=F32)
            + jnp.dot(lo, b_bf16, preferred_element_type=F32))


def _sigmoid(x):
    return 0.5 * jnp.tanh(0.5 * x) + 0.5


def _silu(x):
    h = 0.5 * x
    return h * jnp.tanh(h) + h


def _softplus(x):
    return jnp.maximum(x, 0.0) + jnp.log1p(jnp.exp(-jnp.abs(x)))


class _Masks:
    def __init__(self):
        row = _iota((GROUP_WIDTH, 1), 0)
        lane = _iota((1, GROUP_WIDTH), 1)
        t = _iota((CHUNK, 1), 0)
        self.row, self.lane = row, lane
        self.lane_head = lane >> 6
        block_diag = (row >> 6) == self.lane_head
        self.ones_bd = block_diag.astype(BF16)
        self.bd_f32 = block_diag.astype(F32)
        s = lane & 63
        self.causal_f = (t >= s).astype(F32)
        self.strict_f = (t > s).astype(F32)
        self.diag = t == s


def _bd(x, mk):
    x = x.astype(BF16)
    return jnp.concatenate([x, x, x, x], axis=0) * mk.ones_bd


def _unbd(y, mk):
    ym = y * mk.bd_f32
    return ym[0:64] + ym[64:128] + ym[128:192] + ym[192:256]


def _decay(arg, mask_f, exp=jnp.exp):
    return exp(jnp.minimum(arg, 0.0)) * mask_f


def _expand4(g, mk):
    first = _iota((1, LANES), 1) < HEAD_DIM
    return jnp.concatenate([jnp.where(first, g[:, 0:1], g[:, 1:2]), jnp.where(first, g[:, 2:3], g[:, 3:4])], axis=1)


def _rowvec(ge, mk):
    return jnp.sum(jnp.where(mk.diag, ge, 0.0), axis=0, keepdims=True)


def _chunk_cumsum(x):
    rows = x.shape[0]
    r, c = _iota((rows, rows), 0), _iota((rows, rows), 1)
    tril = (((r >> 6) == (c >> 6)) & (r >= c)).astype(BF16)
    x1 = x.astype(BF16)
    r1 = x - x1.astype(F32)
    x2 = r1.astype(BF16)
    x3 = (r1 - x2.astype(F32)).astype(BF16)
    return (jnp.dot(tril, x1, preferred_element_type=F32) + jnp.dot(tril, x2, preferred_element_type=F32)
            + jnp.dot(tril, x3, preferred_element_type=F32))


def _chunk_cummax(x):
    t = _iota((x.shape[0], 1), 0) & 63
    k = 1
    while k < CHUNK:
        x = jnp.maximum(x, jnp.where(t >= k, pltpu.roll(x, k, 0), NEG_BIG))
        k *= 2
    return x


def _causal_conv(x, prev8, w):
    acc = x * w[CONV_K - 1:CONV_K, :]
    r8 = _iota((8, 1), 0)
    for j in range(1, CONV_K):
        xr = pltpu.roll(x, j, 0)
        top = jnp.where(r8 < j, pltpu.roll(prev8, j, 0), xr[0:8])
        xs = jnp.concatenate([top, xr[8:]], axis=0)
        acc = acc + xs * w[CONV_K - 1 - j:CONV_K - j, :]
    return acc


def _head_rmsnorm(o, mk):
    return o * lax.rsqrt(_mm(o * o, mk.ones_bd) * (1.0 / HEAD_DIM) + EPS)


def _chunks(n_rows):
    return [slice(c * CHUNK, (c + 1) * CHUNK) for c in range(n_rows // CHUNK)]


def _rms(x, w):
    return x * lax.rsqrt(jnp.mean(x * x, axis=-1, keepdims=True) + EPS) * w


def _ffn_body(*refs, with_mix, with_final):
    it = iter(refs)
    x_ref = next(it)
    if with_mix:
        y_ref, wo_ref = next(it), next(it)
    nw_ref, wg_ref, wu_ref, wd_ref = next(it), next(it), next(it), next(it)
    if with_final:
        fw_ref = next(it)
    o_ref = next(it)
    wg_s, wu_s, wd_s = next(it), next(it), next(it)
    step = pl.program_id(0)

    @pl.when(step < W_PREP_STEPS)
    def _():
        for src, dst in ((wg_ref, wg_s), (wu_ref, wu_s), (wd_ref, wd_s)):
            rows = src.shape[0]
            dst[pl.ds(pl.multiple_of(step * rows, rows), rows), :] = src[...].astype(BF16)

    @pl.when(step >= W_PREP_STEPS)
    def _():
        x = x_ref[...]
        if with_mix:
            x = x + jnp.dot(y_ref[...].astype(BF16), wo_ref[...], preferred_element_type=F32)
        xb = _rms(x, nw_ref[...]).astype(BF16)
        acc = jnp.zeros(x.shape, F32)
        for sl in (slice(0, FF_SPLIT), slice(FF_SPLIT, D_FF)):
            g = jnp.dot(xb, wg_s[:, sl], preferred_element_type=F32)
            u = jnp.dot(xb, wu_s[:, sl], preferred_element_type=F32)
            h = (_silu(g) * u).astype(BF16)
            acc = acc + jnp.dot(h, wd_s[sl, :], preferred_element_type=F32)
        out = x + 0.5 * acc
        if with_final:
            out = _rms(out, fw_ref[...])
        o_ref[...] = out


def _const_spec(shape):
    return pl.BlockSpec(shape, lambda *_: (0,) * len(shape), pipeline_mode=pl.Buffered(1))


def _layer_spec(a, layer):
    return pl.BlockSpec((None,) + a.shape[1:], lambda *_: (layer, 0, 0), pipeline_mode=pl.Buffered(1))


def _cast_body(w_ref, o_ref):
    cols = w_ref.shape[1]
    if o_ref.shape[1] != cols:
        o_ref[...] = jnp.zeros(o_ref.shape, BF16)
    o_ref[:, 0:cols] = w_ref[...].astype(BF16)


def _to_bf16(w, cols_out=None):
    depth, rows, cols = w.shape
    cols_out = cols_out or cols
    tr = rows // CAST_STEPS
    return pl.pallas_call(
        _cast_body,
        grid=(depth, CAST_STEPS),
        in_specs=[pl.BlockSpec((None, tr, cols), lambda l, i: (l, i, 0))],
        out_specs=pl.BlockSpec((None, tr, cols_out), lambda l, i: (l, i, 0)),
        out_shape=jax.ShapeDtypeStruct((depth, rows, cols_out), BF16),
        compiler_params=pltpu.CompilerParams(dimension_semantics=("arbitrary", "arbitrary"),
                                             vmem_limit_bytes=VMEM_LIMIT),
        name="cast",
    )(w)


def _ffn_call(x2d, nw, wg, wu, wd, layer, mix=None, final_w=None):
    n = x2d.shape[0]
    tm = min(FFN_TOKENS, n)
    tok = lambda w: pl.BlockSpec((tm, w), lambda i: (jnp.maximum(i - W_PREP_STEPS, 0), 0))
    w_rows = lambda w: pl.BlockSpec((None, w.shape[1] // W_PREP_STEPS, w.shape[2]),
                                    lambda i: (layer, jnp.minimum(i, W_PREP_STEPS - 1), 0))
    args, specs = [x2d], [tok(D_MODEL)]
    if mix is not None:
        y, wo = mix
        args += [y, wo]
        specs += [tok(y.shape[1]), _layer_spec(wo, layer)]
    args += [nw, wg, wu, wd]
    specs += [_layer_spec(nw, layer), w_rows(wg), w_rows(wu), w_rows(wd)]
    if final_w is not None:
        args.append(final_w)
        specs.append(_const_spec(final_w.shape))
    return pl.pallas_call(
        functools.partial(_ffn_body, with_mix=mix is not None, with_final=final_w is not None),
        grid=(W_PREP_STEPS + n // tm,),
        in_specs=specs,
        out_specs=tok(D_MODEL),
        out_shape=jax.ShapeDtypeStruct((n, D_MODEL), F32),
        scratch_shapes=[pltpu.VMEM(w.shape[1:], BF16) for w in (wg, wu, wd)],
        compiler_params=pltpu.CompilerParams(dimension_semantics=("arbitrary",),
                                             vmem_limit_bytes=VMEM_LIMIT),
        name="ffn",
    )(*args)


def _inproj_body(x_ref, nw_ref, w_ref, um_ref, ug_ref):
    xb = _rms(x_ref[...], nw_ref[...]).astype(BF16)
    u = jnp.dot(xb, w_ref[...], preferred_element_type=F32)
    for j, start in enumerate(_MAIN_STARTS):
        lo = (start // LANES) * LANES
        width = 4 * GROUP_WIDTH
        if start == lo:
            blk = u[:, lo:lo + width]
        else:
            win = u[:, lo:lo + width + LANES]
            blk = pltpu.roll(win, width + LANES - (start - lo), 1)[:, 0:width]
        um_ref[:, j * width:(j + 1) * width] = blk
    head_lane = _iota((1, LANES), 1) < N_HEADS
    for j, start in enumerate(_GATE_COLS):
        lo = (start // LANES) * LANES
        tile = u[:, lo:lo + LANES]
        if start != lo:
            tile = pltpu.roll(tile, LANES - (start - lo), 1)
        ug_ref[:, j * LANES:(j + 1) * LANES] = jnp.where(head_lane, tile, 0.0)


def _inproj_call(x2d, nw, w_pad, layer):
    n = x2d.shape[0]
    tm = min(FFN_TOKENS, n)
    wm, wg = 4 * 4 * GROUP_WIDTH, N_GATES * LANES
    return pl.pallas_call(
        _inproj_body,
        grid=(n // tm,),
        in_specs=[pl.BlockSpec((tm, D_MODEL), lambda i: (i, 0)), _layer_spec(nw, layer), _layer_spec(w_pad, layer)],
        out_specs=[pl.BlockSpec((tm, wm), lambda i: (i, 0)), pl.BlockSpec((tm, wg), lambda i: (i, 0))],
        out_shape=[jax.ShapeDtypeStruct((n, wm), F32), jax.ShapeDtypeStruct((n, wg), F32)],
        compiler_params=pltpu.CompilerParams(dimension_semantics=("arbitrary",),
                                             vmem_limit_bytes=VMEM_LIMIT),
        name="inproj",
    )(x2d, nw, w_pad)


def _mlstm_steps(mk, u_ref, gi_ref, gf_ref, ib_ref, fb_ref, nw_ref, y_ref, c_ref, n_ref, m_ref):
    chunks = _chunks(u_ref.shape[0])
    li = gi_ref[...] + ib_ref[...]
    xf = gf_ref[...] + fb_ref[...]
    lf = jnp.minimum(xf, 0.0) - jnp.log1p(jnp.exp(-jnp.abs(xf)))
    bcum = _chunk_cumsum(lf)
    a = li - bcum
    yield
    cmax = _chunk_cummax(a)
    yield

    m_run = m_ref[0:1, :]
    m_start, m_tot = [], []
    for sl in chunks:
        last = slice(sl.stop - 1, sl.stop)
        m_start.append(m_run)
        m_tot.append(jnp.maximum(m_run, cmax[last]))
        m_run = bcum[last] + m_tot[-1]
    m_ref[0:1, :] = m_run
    yield

    qs = [u_ref[sl, 0:256] * (HEAD_DIM ** -0.5) for sl in chunks]
    ks = [u_ref[sl, 256:512] for sl in chunks]
    vs = [u_ref[sl, 512:768] for sl in chunks]
    mx = [jnp.maximum(m0, cmax[sl]) for m0, sl in zip(m_start, chunks)]
    arow = [_rowvec(_expand4(a[sl], mk), mk) for sl in chunks]
    yield
    dmat = [_decay(ar - _expand4(m, mk), mk.causal_f) for ar, m in zip(arow, mx)]
    yield
    p = [_mm_nt(q, _bd(k, mk)) * d for q, k, d in zip(qs, ks, dmat)]
    yield
    num_i = [_mm(pp, _bd(v, mk)) for pp, v in zip(p, vs)]
    yield
    den_i = [_mm_hilo(pp, mk.ones_bd) for pp in p]
    yield
    w_int = [_expand4(jnp.exp(m0 - m), mk) for m0, m in zip(m_start, mx)]
    bound = [_expand4(jnp.exp(-(bcum[sl] + m)), mk) for sl, m in zip(chunks, mx)]
    yield
    kw = [k * _expand4(jnp.exp(a[sl] - mt), mk) for k, sl, mt in zip(ks, chunks, m_tot)]
    decay = [_expand4(jnp.exp(m0 - mt), mk) for m0, mt in zip(m_start, m_tot)]
    yield
    d_c = [_mm_tn(kk, v) * mk.bd_f32 for kk, v in zip(kw, vs)]
    d_n = [jnp.sum(kk, axis=0, keepdims=True) for kk in kw]
    yield

    c_bd, n_row = c_ref[...], n_ref[0:1, :]
    c_at, n_at = [], []
    for i in range(len(chunks)):
        c_at.append(c_bd.astype(BF16))
        n_at.append(n_row)
        c_bd = decay[i] * c_bd + d_c[i]
        n_row = decay[i] * n_row + d_n[i]
    c_ref[...] = c_bd
    n_ref[0:1, :] = n_row
    yield
    q_c = [_mm(q, c) for q, c in zip(qs, c_at)]
    yield
    q_n = [_mm_hilo(q * n, mk.ones_bd) for q, n in zip(qs, n_at)]
    yield
    h = [(ni + w * qc) / jnp.maximum(jnp.abs(di + w * qn), b)
         for ni, di, w, qc, qn, b in zip(num_i, den_i, w_int, q_c, q_n, bound)]
    yield
    hn = [_head_rmsnorm(x, mk) for x in h]
    yield
    nw = nw_ref[...]
    for sl, x in zip(chunks, hn):
        y_ref[sl, :] = x * nw * _sigmoid(u_ref[sl, 768:1024])
    yield


def _mamba_steps(mk, u_ref, gdt_ref, cw_ref, cb_ref, dtb_ref, alog_ref, dskip_ref, nw_ref, y_ref, s_ref, prev_ref):
    rows = u_ref.shape[0]
    chunks = _chunks(rows)
    raw = u_ref[:, 256:1024]
    xbc = _silu(_causal_conv(raw, prev_ref[...], cw_ref[...]) + cb_ref[...])
    prev_ref[...] = raw[rows - 8:rows, :]
    yield
    dt = _softplus(gdt_ref[...] + dtb_ref[...])
    cum = _chunk_cumsum(dt * (-jnp.exp(alog_ref[...])))
    grp_b = (((mk.row >> 6) >> 1) == (mk.lane >> 7)).astype(BF16)
    grp_s = ((mk.row >> 7) == (mk.lane_head >> 1)).astype(F32)
    yield

    xs = [xbc[sl, 0:256] for sl in chunks]
    bm = [xbc[sl, 256:512] for sl in chunks]
    cm = [xbc[sl, 512:768] for sl in chunks]
    cum_e = [_expand4(cum[sl], mk) for sl in chunks]
    yield
    seg = [_decay(ce - _rowvec(ce, mk), mk.causal_f) for ce in cum_e]
    yield
    dtrow = [_rowvec(_expand4(dt[sl], mk), mk) for sl in chunks]
    yield
    p = [_mm_nt(c, jnp.concatenate([b.astype(BF16)] * 4, axis=0) * grp_b) * sg * dr
         for c, b, sg, dr in zip(cm, bm, seg, dtrow)]
    yield
    y_i = [_mm(pp, _bd(x, mk)) + x * dskip_ref[...] for pp, x in zip(p, xs)]
    yield
    last = [slice(sl.stop - 1, sl.stop) for sl in chunks]
    xw = [x * _expand4(jnp.exp(cum[ls] - cum[sl]) * dt[sl], mk) for x, sl, ls in zip(xs, chunks, last)]
    yield
    d_s = [_mm_tn(b, w) * grp_s for b, w in zip(bm, xw)]
    yield
    decay = [_expand4(jnp.exp(cum[ls]), mk) for ls in last]
    e_cum = [jnp.exp(ce) for ce in cum_e]
    yield

    s_mat = s_ref[...]
    s_at = []
    for i in range(len(chunks)):
        s_at.append(s_mat.astype(BF16))
        s_mat = decay[i] * s_mat + d_s[i]
    s_ref[...] = s_mat
    yield
    y = [yi + e * _mm(c, s) for yi, e, c, s in zip(y_i, e_cum, cm, s_at)]
    yield
    nw = nw_ref[...]
    for sl, yc in zip(chunks, y):
        yz = yc * _silu(u_ref[sl, 0:256])
        halves = []
        for g in range(2):
            yg = yz[:, g * 128:(g + 1) * 128]
            halves.append(yg * lax.rsqrt(jnp.mean(yg * yg, axis=-1, keepdims=True) + EPS))
        y_ref[sl, :] = jnp.concatenate(halves, axis=-1) * nw
    yield


HG_SUB = 16


def _stack_rows(x, mk):
    n = x.shape[0]
    r = _iota((4 * n, 1), 0)
    blk = (r >= n).astype(jnp.int32) + (r >= 2 * n).astype(jnp.int32) + (r >= 3 * n).astype(jnp.int32)
    x = x.astype(BF16)
    return jnp.where(blk == mk.lane_head, jnp.concatenate([x, x, x, x], axis=0), jnp.zeros((), BF16))


def _hgrn_intra_sub(qq, kk, gc, i_in, a, mk):
    half = HG_SUB // 2
    t_half = _iota((half, 1), 0)
    lo = a * HG_SUB
    q_a, k_a, g_a, i_a = qq[lo:lo + HG_SUB], kk[lo:lo + HG_SUB], gc[lo:lo + HG_SUB], i_in[lo:lo + HG_SUB]
    q_t, q_b, g_t, g_b = q_a[0:half], q_a[half:], g_a[0:half], g_a[half:]
    tops, bots = [], []
    for s in range(HG_SUB):
        k_s, g_s = k_a[s:s + 1, :], g_a[s:s + 1, :]
        if s < half:
            tops.append(q_t * k_s * _decay(g_t - g_s, (t_half >= s).astype(F32), jnp.exp2))
            bots.append(q_b * k_s * jnp.exp2(g_b - g_s))
        else:
            bots.append(q_b * k_s * _decay(g_b - g_s, (t_half >= s - half).astype(F32), jnp.exp2))
    z = _mm(jnp.concatenate(tops + bots, axis=0), mk.ones_bd)
    z_t, z_b = z[0:half * half], z[half * half:]
    o_t = z_t[0:half] * i_a[0:1, :]
    for s in range(1, half):
        o_t = o_t + z_t[s * half:(s + 1) * half] * i_a[s:s + 1, :]
    o_b = z_b[0:half] * i_a[0:1, :]
    for s in range(1, HG_SUB):
        o_b = o_b + z_b[s * half:(s + 1) * half] * i_a[s:s + 1, :]
    o = jnp.concatenate([o_t, o_b], axis=0)
    if a > 0:
        r = gc[lo - 1:lo, :]
        sc = _mm_nt(q_a * jnp.exp2(g_a - r), _stack_rows(kk[0:lo] * jnp.exp2(r - gc[0:lo]), mk))
        o = o + _mm(sc, _stack_rows(i_in[0:lo], mk))
    return o


def _hgrn_steps(mk, u_ref, lbl_ref, nw_ref, y_ref, st_ref, *, layer):
    logits = lbl_ref[...]
    e = jnp.exp(logits - jnp.max(logits, axis=0, keepdims=True))
    prob = e / jnp.sum(e, axis=0, keepdims=True)
    lb = jnp.sum(prob[0:layer + 1], axis=0, keepdims=True) - prob[0:1]

    chunks = _chunks(u_ref.shape[0])
    f = lb + (1.0 - lb) * _sigmoid(u_ref[:, 256:512])
    kk_all = 1.0 - f
    qq_all = _silu(u_ref[:, 0:256])
    yield
    gc_all = _chunk_cumsum(jnp.log(f)) * LOG2E
    yield

    qq = [qq_all[sl] for sl in chunks]
    kk = [kk_all[sl] for sl in chunks]
    gc = [gc_all[sl] for sl in chunks]
    ii = [u_ref[sl, 512:768] for sl in chunks]
    subs = [[] for _ in chunks]
    for a in range(CHUNK // HG_SUB):
        for c in range(len(chunks)):
            subs[c].append(_hgrn_intra_sub(qq[c], kk[c], gc[c], ii[c], a, mk))
            yield
    o_i = [jnp.concatenate(s, axis=0) for s in subs]
    qe = [q * jnp.exp2(g) for q, g in zip(qq, gc)]
    g_last = [g[CHUNK - 1:CHUNK, :] for g in gc]
    yield
    d_s = [_mm_tn(i, k * jnp.exp2(gl - g)) * mk.bd_f32
           for i, k, g, gl in zip(ii, kk, gc, g_last)]
    decay = [jnp.exp2(gl) for gl in g_last]
    yield

    st = st_ref[...]
    st_at = []
    for i in range(len(chunks)):
        st_at.append(st.astype(BF16))
        st = decay[i] * st + d_s[i]
    st_ref[...] = st
    yield
    o = [oi + _mm_nt(q, s) for oi, q, s in zip(o_i, qe, st_at)]
    yield
    on = [_head_rmsnorm(x, mk) for x in o]
    yield
    nw = nw_ref[...]
    for sl, x in zip(chunks, on):
        y_ref[sl, :] = x * nw * _silu(u_ref[sl, 768:1024])
    yield


def _gdn_steps(mk, u_ref, gb_ref, ga_ref, cw_ref, alog_ref, dtb_ref, nw_ref, y_ref, s_ref, prev_ref):
    rows = u_ref.shape[0]
    chunks = _chunks(rows)
    raw = u_ref[:, 0:768]
    qkv = _silu(_causal_conv(raw, prev_ref[...], cw_ref[...]))
    prev_ref[...] = raw[rows - 8:rows, :]
    yield
    q, k, v = qkv[:, 0:256], qkv[:, 256:512], qkv[:, 512:768]
    qn = q * lax.rsqrt(_mm(q * q, mk.ones_bd) + EPS) * (HEAD_DIM ** -0.5)
    kn = k * lax.rsqrt(_mm(k * k, mk.ones_bd) + EPS)
    yield
    gcum = _chunk_cumsum(-jnp.exp(alog_ref[...]) * _softplus(ga_ref[...] + dtb_ref[...]))
    be = _expand4(_sigmoid(gb_ref[...]), mk)
    ge = _expand4(gcum, mk)
    yield
    e_g = jnp.exp(ge)
    kb = kn * be
    vb = v * be
    kbg = kb * e_g
    q_dec = qn * e_g
    eye = jnp.where(mk.diag, 1.0, 0.0)
    yield

    e_dec = [jnp.exp(jnp.minimum(ge[sl] - _rowvec(ge[sl], mk), 0.0)) for sl in chunks]
    yield
    qk = [_mm_nt(jnp.concatenate([kb[sl], qn[sl]], axis=0), _bd(kn[sl], mk)) for sl in chunks]
    yield
    a_p = [x[0:CHUNK] * (e * mk.strict_f) for x, e in zip(qk, e_dec)]
    attn = [x[CHUNK:2 * CHUNK] * (e * mk.causal_f) for x, e in zip(qk, e_dec)]

    pw = [_mm(-x, _bd(-x, mk)) for x in a_p]
    t0 = [eye - x for x in a_p]
    yield
    for j in range(1, 6):
        r = [_mm(jnp.concatenate([t, m], axis=0), _bd(m, mk)) for t, m in zip(t0, pw)]
        t0 = [t + x[0:CHUNK] for t, x in zip(t0, r)]
        pw = [x[CHUNK:2 * CHUNK] for x in r]
        yield
    a_t0 = []
    for x, t in zip(a_p, t0):
        a_hi, a_lo = _split2(x)
        t_hi, t_lo = _split2(t)
        r = jnp.dot(jnp.concatenate([a_hi, a_lo], axis=0), _bd(t_hi, mk).astype(BF16), preferred_element_type=F32)
        a_t0.append(r[0:CHUNK] + r[CHUNK:2 * CHUNK]
                    + jnp.dot(a_hi, _bd(t_lo, mk).astype(BF16), preferred_element_type=F32))
    yield
    t_mat = [t + _mm(t, _bd(eye - t - at, mk)) for t, at in zip(t0, a_t0)]
    yield

    uw = [_mm(t, jnp.concatenate([_bd(vb[sl], mk), _bd(kbg[sl], mk)], axis=1)) for t, sl in zip(t_mat, chunks)]
    u = [x[:, 0:256] for x in uw]
    w = [x[:, 256:512] for x in uw]
    yield
    au = [_mm(at, jnp.concatenate([_bd(uu, mk), _bd(ww, mk)], axis=1)) for at, uu, ww in zip(attn, u, w)]
    o_i = [x[:, 0:256] for x in au]
    q2 = [q_dec[sl] - x[:, 256:512] for sl, x in zip(chunks, au)]
    yield
    last = [slice(sl.stop - 1, sl.stop) for sl in chunks]
    k_dec = [kn[sl] * _expand4(jnp.exp(gcum[ls] - gcum[sl]), mk) for sl, ls in zip(chunks, last)]
    fg = [_mm_tn(kd, jnp.concatenate([ww, uu], axis=1)) for kd, ww, uu in zip(k_dec, w, u)]
    yield
    f_p = [-_unbd(x[:, 0:256], mk) for x in fg]
    g_p = [_unbd(x[:, 256:512], mk) for x in fg]
    decay = [_expand4(jnp.exp(gcum[ls]), mk) for ls in last]
    yield

    s_p = s_ref[...]
    s_at = []
    for i in range(len(chunks)):
        s_bd = _bd(s_p, mk)
        s_at.append(s_bd)
        s_p = decay[i] * s_p + _mm(f_p[i], s_bd) + g_p[i]
        yield
    s_ref[...] = s_p
    o = [oi + _mm(q, s) for oi, q, s in zip(o_i, q2, s_at)]
    yield
    on = [_head_rmsnorm(x, mk) for x in o]
    yield
    nw = nw_ref[...]
    for sl, x in zip(chunks, on):
        y_ref[sl, :] = x * nw * _silu(u_ref[sl, 768:1024])
    yield


_DONE = object()


N_MIXER_PARAMS = 15


def _mixer_steps(um_ref, ug_ref, params, y_ref, states, layer):
    ib, fb, nw_a, cw_b, cb_b, dtb_b, alog_b, dskip_b, nw_b, lbl, nw_c, cw_d, alog_d, dtb_d, nw_d = params
    c_ref, n_ref, m_ref, sb_ref, prevb_ref, st_ref, sd_ref, prevd_ref = states
    mk = _Masks()
    win = lambda j: um_ref.at[:, pl.ds(j * 4 * GROUP_WIDTH, 4 * GROUP_WIDTH)]
    gate = lambda j: ug_ref.at[:, pl.ds(j * LANES, LANES)]
    out = lambda j: y_ref.at[:, pl.ds(j * GROUP_WIDTH, GROUP_WIDTH)]
    return [
        (_gdn_steps(mk, win(3), gate(3), gate(4), cw_d, alog_d, dtb_d, nw_d, out(3), sd_ref, prevd_ref), 1.0),
        (_hgrn_steps(mk, win(2), lbl, nw_c, out(2), st_ref, layer=layer), 1.0),
        (_mlstm_steps(mk, win(0), gate(0), gate(1), ib, fb, nw_a, out(0), c_ref, n_ref, m_ref), MLSTM_PACE),
        (_mamba_steps(mk, win(1), gate(2), cw_b, cb_b, dtb_b, alog_b, dskip_b, nw_b, out(1), sb_ref, prevb_ref),
         MAMBA_PACE),
    ]


def _run_round_robin(steps):
    rnd = 0
    while steps:
        due = [(g, p) for g, p in steps if int((rnd + 1) * p) > int(rnd * p)]
        done = [g for g, _ in due if next(g, _DONE) is _DONE]
        steps = [(g, p) for g, p in steps if g not in done]
        rnd += 1


def _mixers_body(*refs, layer):
    um_ref, ug_ref = refs[0:2]
    params = refs[2:2 + N_MIXER_PARAMS]
    y_ref = refs[2 + N_MIXER_PARAMS]
    states = refs[3 + N_MIXER_PARAMS:]

    @pl.when(pl.program_id(1) == 0)
    def _():
        for r in states:
            r[...] = jnp.zeros_like(r)

    _run_round_robin(_mixer_steps(um_ref, ug_ref, params, y_ref, states, layer))


def _mixer_param_specs(params, layer):
    return [pl.BlockSpec((None,) + p.shape[1:], lambda *_: (layer, 0, 0)) if per_layer
            else pl.BlockSpec(p.shape, lambda *_, nd=p.ndim: (0,) * nd) for p, per_layer in params]


def _mixer_state_shapes():
    state = pltpu.VMEM((GROUP_WIDTH, GROUP_WIDTH), F32)
    conv_tail = pltpu.VMEM((8, 3 * GROUP_WIDTH), F32)
    return [state, pltpu.VMEM((8, GROUP_WIDTH), F32), pltpu.VMEM((8, LANES), F32),
            state, conv_tail,
            state,
            pltpu.VMEM((HEAD_DIM, GROUP_WIDTH), F32), conv_tail]


def _mixers_call(u_main, u_gate, params, layer, seq_len):
    n = u_main.shape[0]
    tb = min(MIX_TOKENS, seq_len)
    tiles = seq_len // tb
    blk = lambda w: pl.BlockSpec((tb, w), lambda bi, i: (bi * tiles + i, 0))
    return pl.pallas_call(
        functools.partial(_mixers_body, layer=layer),
        grid=(n // seq_len, tiles),
        in_specs=[blk(u_main.shape[1]), blk(u_gate.shape[1])] + _mixer_param_specs(params, layer),
        out_specs=blk(N_MIXERS * GROUP_WIDTH),
        out_shape=jax.ShapeDtypeStruct((n, N_MIXERS * GROUP_WIDTH), F32),
        scratch_shapes=_mixer_state_shapes(),
        compiler_params=pltpu.CompilerParams(dimension_semantics=("arbitrary", "arbitrary"),
                                             vmem_limit_bytes=VMEM_LIMIT),
        name="mixers",
    )(u_main, u_gate, *[p for p, _ in params])


def _rows(v):
    return v.astype(F32)[:, None, :]


def _head_rows(v):
    return jnp.pad(_rows(v), ((0, 0), (0, 0), (0, LANES - v.shape[1])))


def _mixer_params(mlstm_i_bias, mlstm_f_bias, mlstm_norm, mamba_conv_w, mamba_conv_b, mamba_dt_bias, mamba_a_log,
                  mamba_d, mamba_norm, hgrn_lb_logits, hgrn_norm, gdn_conv_w, gdn_a_log, gdn_dt_bias, gdn_norm):
    per_layer = [
        _head_rows(mlstm_i_bias), _head_rows(mlstm_f_bias), _rows(mlstm_norm),
        mamba_conv_w.astype(F32), _rows(mamba_conv_b), _head_rows(mamba_dt_bias), _head_rows(mamba_a_log),
        _rows(jnp.repeat(mamba_d, HEAD_DIM, axis=1)), _rows(mamba_norm)]
    hgrn = [(hgrn_lb_logits.astype(F32), False), (_rows(hgrn_norm), True)]
    gdn = [gdn_conv_w.astype(F32), _head_rows(gdn_a_log), _head_rows(gdn_dt_bias), _rows(gdn_norm)]
    return [(p, True) for p in per_layer] + hgrn + [(p, True) for p in gdn]


_MLSTM0, _MAMBA0, _HGRN0, _GDN0 = 0, 1032, 2060, 3084
_MAIN_COLS = ((_MLSTM0, _MLSTM0 + 1024), (_MAMBA0, _MAMBA0 + 1024), (_HGRN0, _HGRN0 + 1024),
              (_GDN0, _GDN0 + 1024))
_GATE_COLS = (_MLSTM0 + 1024, _MLSTM0 + 1028, _MAMBA0 + 1024, _GDN0 + 1024, _GDN0 + 1028)
_MAIN_STARTS = tuple(a for a, _ in _MAIN_COLS)
_D_IN_PAD = 33 * LANES


def kernel(x, ffn1_norm, ffn1_w_gate, ffn1_w_up, ffn1_w_down, mix_norm, w_in, w_out, mlstm_i_bias, mlstm_f_bias, mlstm_norm, mamba_conv_w, mamba_conv_b, mamba_dt_bias, mamba_a_log, mamba_d, mamba_norm, hgrn_lb_logits, hgrn_norm, gdn_conv_w, gdn_a_log, gdn_dt_bias, gdn_norm, ffn2_norm, ffn2_w_gate, ffn2_w_up, ffn2_w_down, final_norm):
    b, s, d = x.shape
    depth = w_in.shape[0]
    x2d = x.reshape(b * s, d)
    ffn1 = (_rows(ffn1_norm), ffn1_w_gate, ffn1_w_up, ffn1_w_down)
    ffn2 = (_rows(ffn2_norm), ffn2_w_gate, ffn2_w_up, ffn2_w_down)
    w_pad = _to_bf16(w_in, _D_IN_PAD)
    w_o = _to_bf16(w_out)
    mix_nw = _rows(mix_norm)
    params = _mixer_params(mlstm_i_bias, mlstm_f_bias, mlstm_norm, mamba_conv_w, mamba_conv_b, mamba_dt_bias,
                           mamba_a_log, mamba_d, mamba_norm, hgrn_lb_logits, hgrn_norm, gdn_conv_w, gdn_a_log,
                           gdn_dt_bias, gdn_norm)
    for l in range(depth):
        x2d = _ffn_call(x2d, *ffn1, l)
        um, ug = _inproj_call(x2d, mix_nw, w_pad, l)
        y = _mixers_call(um, ug, params, l, s)
        x2d = _ffn_call(x2d, *ffn2, l, mix=(y, w_o),
                        final_w=final_norm.astype(F32).reshape(1, d) if l == depth - 1 else None)
    return x2d.reshape(b, s, d)
```

```python
import functools

import numpy as np
import jax
import jax.numpy as jnp
from jax import lax
from jax.experimental import pallas as pl
from jax.experimental.pallas import tpu as pltpu

F32 = jnp.float32
BF16 = jnp.bfloat16

D_MODEL = 1024
CHUNK = 64
N_HEADS = 4
HEAD_DIM = 64
GROUP_WIDTH = N_HEADS * HEAD_DIM
SSM_STATE = 128
CONV_K = 4
D_FF = 2816
EPS = 1e-6
NEG_BIG = -1e30
LOG2E = 1.4426950408889634
LANES = 128
N_GATES = 5

FFN_TOKENS = 512
CAST_STEPS = 2
W_PREP_STEPS = 8
MXU_DIM = 256
FF_SPLIT = 6 * MXU_DIM
N_MIXERS = 4
MLSTM_PACE = 0.7
MAMBA_PACE = 0.55
MIX_TOKENS = 256
VMEM_LIMIT = 56 * 1024 * 1024


def _iota(shape, dim):
    return lax.broadcasted_iota(jnp.int32, shape, dim)


def _mm(a, b):
    return jnp.dot(a.astype(BF16), b.astype(BF16), preferred_element_type=F32)


def _mm_nt(a, b):
    return lax.dot_general(a.astype(BF16), b.astype(BF16), (((1,), (1,)), ((), ())),
                           preferred_element_type=F32)


def _mm_tn(a, b):
    return lax.dot_general(a.astype(BF16), b.astype(BF16), (((0,), (0,)), ((), ())),
                           preferred_element_type=F32)


def _split2(a):
    hi = a.astype(BF16)
    return hi, (a - hi.astype(F32)).astype(BF16)


def _mm_hilo(a, b_bf16):
    hi, lo = _split2(a)
    return (jnp.dot(hi, b_bf16, preferred_element_type=F32)
            + jnp.dot(lo, b_bf16, preferred_element_type=F32))


def _sigmoid(x):
    return 0.5 * jnp.tanh(0.5 * x) + 0.5


def _silu(x):
    h = 0.5 * x
    return h * jnp.tanh(h) + h


def _softplus(x):
    return jnp.maximum(x, 0.0) + jnp.log1p(jnp.exp(-jnp.abs(x)))


class _Masks:
    def __init__(self):
        row = _iota((GROUP_WIDTH, 1), 0)
        lane = _iota((1, GROUP_WIDTH), 1)
        t = _iota((CHUNK, 1), 0)
        self.row, self.lane = row, lane
        self.lane_head = lane >> 6
        block_diag = (row >> 6) == self.lane_head
        self.ones_bd = block_diag.astype(BF16)
        self.bd_f32 = block_diag.astype(F32)
        s = lane & 63
        self.causal_f = (t >= s).astype(F32)
        self.strict_f = (t > s).astype(F32)
        self.diag = t == s


def _bd(x, mk):
    x = x.astype(BF16)
    return jnp.concatenate([x, x, x, x], axis=0) * mk.ones_bd


def _unbd(y, mk):
    ym = y * mk.bd_f32
    return ym[0:64] + ym[64:128] + ym[128:192] + ym[192:256]


def _decay(arg, mask_f, exp=jnp.exp):
    return exp(jnp.minimum(arg, 0.0)) * mask_f


def _expand4(g, mk):
    first = _iota((1, LANES), 1) < HEAD_DIM
    return jnp.concatenate([jnp.where(first, g[:, 0:1], g[:, 1:2]), jnp.where(first, g[:, 2:3], g[:, 3:4])], axis=1)


def _rowvec(ge, mk):
    return jnp.sum(jnp.where(mk.diag, ge, 0.0), axis=0, keepdims=True)


def _chunk_cumsum(x):
    rows = x.shape[0]
    r, c = _iota((rows, rows), 0), _iota((rows, rows), 1)
    tril = (((r >> 6) == (c >> 6)) & (r >= c)).astype(BF16)
    x1 = x.astype(BF16)
    r1 = x - x1.astype(F32)
    x2 = r1.astype(BF16)
    x3 = (r1 - x2.astype(F32)).astype(BF16)
    return (jnp.dot(tril, x1, preferred_element_type=F32) + jnp.dot(tril, x2, preferred_element_type=F32)
            + jnp.dot(tril, x3, preferred_element_type=F32))


def _chunk_cummax(x):
    t = _iota((x.shape[0], 1), 0) & 63
    k = 1
    while k < CHUNK:
        x = jnp.maximum(x, jnp.where(t >= k, pltpu.roll(x, k, 0), NEG_BIG))
        k *= 2
    return x


def _causal_conv(x, prev8, w):
    acc = x * w[CONV_K - 1:CONV_K, :]
    r8 = _iota((8, 1), 0)
    for j in range(1, CONV_K):
        xr = pltpu.roll(x, j, 0)
        top = jnp.where(r8 < j, pltpu.roll(prev8, j, 0), xr[0:8])
        xs = jnp.concatenate([top, xr[8:]], axis=0)
        acc = acc + xs * w[CONV_K - 1 - j:CONV_K - j, :]
    return acc


def _head_rmsnorm(o, mk):
    return o * lax.rsqrt(_mm(o * o, mk.ones_bd) * (1.0 / HEAD_DIM) + EPS)


def _chunks(n_rows):
    return [slice(c * CHUNK, (c + 1) * CHUNK) for c in range(n_rows // CHUNK)]


def _rms(x, w):
    return x * lax.rsqrt(jnp.mean(x * x, axis=-1, keepdims=True) + EPS) * w


def _ffn_body(*refs, with_mix, with_final):
    it = iter(refs)
    x_ref = next(it)
    if with_mix:
        y_ref, wo_ref = next(it), next(it)
    nw_ref, wg_ref, wu_ref, wd_ref = next(it), next(it), next(it), next(it)
    if with_final:
        fw_ref = next(it)
    o_ref = next(it)
    wg_s, wu_s, wd_s = next(it), next(it), next(it)
    step = pl.program_id(0)

    @pl.when(step < W_PREP_STEPS)
    def _():
        for src, dst in ((wg_ref, wg_s), (wu_ref, wu_s), (wd_ref, wd_s)):
            rows = src.shape[0]
            dst[pl.ds(pl.multiple_of(step * rows, rows), rows), :] = src[...].astype(BF16)

    @pl.when(step >= W_PREP_STEPS)
    def _():
        x = x_ref[...]
        if with_mix:
            x = x + jnp.dot(y_ref[...].astype(BF16), wo_ref[...], preferred_element_type=F32)
        xb = _rms(x, nw_ref[...]).astype(BF16)
        acc = jnp.zeros(x.shape, F32)
        for sl in (slice(0, FF_SPLIT), slice(FF_SPLIT, D_FF)):
            g = jnp.dot(xb, wg_s[:, sl], preferred_element_type=F32)
            u = jnp.dot(xb, wu_s[:, sl], preferred_element_type=F32)
            h = (_silu(g) * u).astype(BF16)
            acc = acc + jnp.dot(h, wd_s[sl, :], preferred_element_type=F32)
        out = x + 0.5 * acc
        if with_final:
            out = _rms(out, fw_ref[...])
        o_ref[...] = out


def _const_spec(shape):
    return pl.BlockSpec(shape, lambda *_: (0,) * len(shape), pipeline_mode=pl.Buffered(1))


def _layer_spec(a, layer):
    return pl.BlockSpec((None,) + a.shape[1:], lambda *_: (layer, 0, 0), pipeline_mode=pl.Buffered(1))


def _cast_body(w_ref, o_ref):
    cols = w_ref.shape[1]
    if o_ref.shape[1] != cols:
        o_ref[...] = jnp.zeros(o_ref.shape, BF16)
    o_ref[:, 0:cols] = w_ref[...].astype(BF16)


def _to_bf16(w, cols_out=None):
    depth, rows, cols = w.shape
    cols_out = cols_out or cols
    tr = rows // CAST_STEPS
    return pl.pallas_call(
        _cast_body,
        grid=(depth, CAST_STEPS),
        in_specs=[pl.BlockSpec((None, tr, cols), lambda l, i: (l, i, 0))],
        out_specs=pl.BlockSpec((None, tr, cols_out), lambda l, i: (l, i, 0)),
        out_shape=jax.ShapeDtypeStruct((depth, rows, cols_out), BF16),
        compiler_params=pltpu.CompilerParams(dimension_semantics=("arbitrary", "arbitrary"),
                                             vmem_limit_bytes=VMEM_LIMIT),
        name="cast",
    )(w)


def _ffn_call(x2d, nw, wg, wu, wd, layer, mix=None, final_w=None):
    n = x2d.shape[0]
    tm = min(FFN_TOKENS, n)
    tok = lambda w: pl.BlockSpec((tm, w), lambda i: (jnp.maximum(i - W_PREP_STEPS, 0), 0))
    w_rows = lambda w: pl.BlockSpec((None, w.shape[1] // W_PREP_STEPS, w.shape[2]),
                                    lambda i: (layer, jnp.minimum(i, W_PREP_STEPS - 1), 0))
    args, specs = [x2d], [tok(D_MODEL)]
    if mix is not None:
        y, wo = mix
        args += [y, wo]
        specs += [tok(y.shape[1]), _layer_spec(wo, layer)]
    args += [nw, wg, wu, wd]
    specs += [_layer_spec(nw, layer), w_rows(wg), w_rows(wu), w_rows(wd)]
    if final_w is not None:
        args.append(final_w)
        specs.append(_const_spec(final_w.shape))
    return pl.pallas_call(
        functools.partial(_ffn_body, with_mix=mix is not None, with_final=final_w is not None),
        grid=(W_PREP_STEPS + n // tm,),
        in_specs=specs,
        out_specs=tok(D_MODEL),
        out_shape=jax.ShapeDtypeStruct((n, D_MODEL), F32),
        scratch_shapes=[pltpu.VMEM(w.shape[1:], BF16) for w in (wg, wu, wd)],
        compiler_params=pltpu.CompilerParams(dimension_semantics=("arbitrary",),
                                             vmem_limit_bytes=VMEM_LIMIT),
        name="ffn",
    )(*args)


def _inproj_body(x_ref, nw_ref, w_ref, um_ref, ug_ref):
    xb = _rms(x_ref[...], nw_ref[...]).astype(BF16)
    u = jnp.dot(xb, w_ref[...], preferred_element_type=F32)
    for j, start in enumerate(_MAIN_STARTS):
        lo = (start // LANES) * LANES
        width = 4 * GROUP_WIDTH
        if start == lo:
            blk = u[:, lo:lo + width]
        else:
            win = u[:, lo:lo + width + LANES]
            blk = pltpu.roll(win, width + LANES - (start - lo), 1)[:, 0:width]
        um_ref[:, j * width:(j + 1) * width] = blk
    head_lane = _iota((1, LANES), 1) < N_HEADS
    for j, start in enumerate(_GATE_COLS):
        lo = (start // LANES) * LANES
        tile = u[:, lo:lo + LANES]
        if start != lo:
            tile = pltpu.roll(tile, LANES - (start - lo), 1)
        ug_ref[:, j * LANES:(j + 1) * LANES] = jnp.where(head_lane, tile, 0.0)


def _inproj_call(x2d, nw, w_pad, layer):
    n = x2d.shape[0]
    tm = min(FFN_TOKENS, n)
    wm, wg = 4 * 4 * GROUP_WIDTH, N_GATES * LANES
    return pl.pallas_call(
        _inproj_body,
        grid=(n // tm,),
        in_specs=[pl.BlockSpec((tm, D_MODEL), lambda i: (i, 0)), _layer_spec(nw, layer), _layer_spec(w_pad, layer)],
        out_specs=[pl.BlockSpec((tm, wm), lambda i: (i, 0)), pl.BlockSpec((tm, wg), lambda i: (i, 0))],
        out_shape=[jax.ShapeDtypeStruct((n, wm), F32), jax.ShapeDtypeStruct((n, wg), F32)],
        compiler_params=pltpu.CompilerParams(dimension_semantics=("arbitrary",),
                                             vmem_limit_bytes=VMEM_LIMIT),
        name="inproj",
    )(x2d, nw, w_pad)


def _mlstm_steps(mk, u_ref, gi_ref, gf_ref, ib_ref, fb_ref, nw_ref, y_ref, c_ref, n_ref, m_ref):
    chunks = _chunks(u_ref.shape[0])
    li = gi_ref[...] + ib_ref[...]
    xf = gf_ref[...] + fb_ref[...]
    lf = jnp.minimum(xf, 0.0) - jnp.log1p(jnp.exp(-jnp.abs(xf)))
    bcum = _chunk_cumsum(lf)
    a = li - bcum
    yield
    cmax = _chunk_cummax(a)
    yield

    m_run = m_ref[0:1, :]
    m_start, m_tot = [], []
    for sl in chunks:
        last = slice(sl.stop - 1, sl.stop)
        m_start.append(m_run)
        m_tot.append(jnp.maximum(m_run, cmax[last]))
        m_run = bcum[last] + m_tot[-1]
    m_ref[0:1, :] = m_run
    yield

    qs = [u_ref[sl, 0:256] * (HEAD_DIM ** -0.5) for sl in chunks]
    ks = [u_ref[sl, 256:512] for sl in chunks]
    vs = [u_ref[sl, 512:768] for sl in chunks]
    mx = [jnp.maximum(m0, cmax[sl]) for m0, sl in zip(m_start, chunks)]
    arow = [_rowvec(_expand4(a[sl], mk), mk) for sl in chunks]
    yield
    dmat = [_decay(ar - _expand4(m, mk), mk.causal_f) for ar, m in zip(arow, mx)]
    yield
    p = [_mm_nt(q, _bd(k, mk)) * d for q, k, d in zip(qs, ks, dmat)]
    yield
    num_i = [_mm(pp, _bd(v, mk)) for pp, v in zip(p, vs)]
    yield
    den_i = [_mm_hilo(pp, mk.ones_bd) for pp in p]
    yield
    w_int = [_expand4(jnp.exp(m0 - m), mk) for m0, m in zip(m_start, mx)]
    bound = [_expand4(jnp.exp(-(bcum[sl] + m)), mk) for sl, m in zip(chunks, mx)]
    yield
    kw = [k * _expand4(jnp.exp(a[sl] - mt), mk) for k, sl, mt in zip(ks, chunks, m_tot)]
    decay = [_expand4(jnp.exp(m0 - mt), mk) for m0, mt in zip(m_start, m_tot)]
    yield
    d_c = [_mm_tn(kk, v) * mk.bd_f32 for kk, v in zip(kw, vs)]
    d_n = [jnp.sum(kk, axis=0, keepdims=True) for kk in kw]
    yield

    c_bd, n_row = c_ref[...], n_ref[0:1, :]
    c_at, n_at = [], []
    for i in range(len(chunks)):
        c_at.append(c_bd.astype(BF16))
        n_at.append(n_row)
        c_bd = decay[i] * c_bd + d_c[i]
        n_row = decay[i] * n_row + d_n[i]
    c_ref[...] = c_bd
    n_ref[0:1, :] = n_row
    yield
    q_c = [_mm(q, c) for q, c in zip(qs, c_at)]
    yield
    q_n = [_mm_hilo(q * n, mk.ones_bd) for q, n in zip(qs, n_at)]
    yield
    h = [(ni + w * qc) / jnp.maximum(jnp.abs(di + w * qn), b)
         for ni, di, w, qc, qn, b in zip(num_i, den_i, w_int, q_c, q_n, bound)]
    yield
    hn = [_head_rmsnorm(x, mk) for x in h]
    yield
    nw = nw_ref[...]
    for sl, x in zip(chunks, hn):
        y_ref[sl, :] = x * nw * _sigmoid(u_ref[sl, 768:1024])
    yield


def _mamba_steps(mk, u_ref, gdt_ref, cw_ref, cb_ref, dtb_ref, alog_ref, dskip_ref, nw_ref, y_ref, s_ref, prev_ref):
    rows = u_ref.shape[0]
    chunks = _chunks(rows)
    raw = u_ref[:, 256:1024]
    xbc = _silu(_causal_conv(raw, prev_ref[...], cw_ref[...]) + cb_ref[...])
    prev_ref[...] = raw[rows - 8:rows, :]
    yield
    dt = _softplus(gdt_ref[...] + dtb_ref[...])
    cum = _chunk_cumsum(dt * (-jnp.exp(alog_ref[...])))
    grp_b = (((mk.row >> 6) >> 1) == (mk.lane >> 7)).astype(BF16)
    grp_s = ((mk.row >> 7) == (mk.lane_head >> 1)).astype(F32)
    yield

    xs = [xbc[sl, 0:256] for sl in chunks]
    bm = [xbc[sl, 256:512] for sl in chunks]
    cm = [xbc[sl, 512:768] for sl in chunks]
    cum_e = [_expand4(cum[sl], mk) for sl in chunks]
    yield
    seg = [_decay(ce - _rowvec(ce, mk), mk.causal_f) for ce in cum_e]
    yield
    dtrow = [_rowvec(_expand4(dt[sl], mk), mk) for sl in chunks]
    yield
    p = [_mm_nt(c, jnp.concatenate([b.astype(BF16)] * 4, axis=0) * grp_b) * sg * dr
         for c, b, sg, dr in zip(cm, bm, seg, dtrow)]
    yield
    y_i = [_mm(pp, _bd(x, mk)) + x * dskip_ref[...] for pp, x in zip(p, xs)]
    yield
    last = [slice(sl.stop - 1, sl.stop) for sl in chunks]
    xw = [x * _expand4(jnp.exp(cum[ls] - cum[sl]) * dt[sl], mk) for x, sl, ls in zip(xs, chunks, last)]
    yield
    d_s = [_mm_tn(b, w) * grp_s for b, w in zip(bm, xw)]
    yield
    decay = [_expand4(jnp.exp(cum[ls]), mk) for ls in last]
    e_cum = [jnp.exp(ce) for ce in cum_e]
    yield

    s_mat = s_ref[...]
    s_at = []
    for i in range(len(chunks)):
        s_at.append(s_mat.astype(BF16))
        s_mat = decay[i] * s_mat + d_s[i]
    s_ref[...] = s_mat
    yield
    y = [yi + e * _mm(c, s) for yi, e, c, s in zip(y_i, e_cum, cm, s_at)]
    yield
    nw = nw_ref[...]
    for sl, yc in zip(chunks, y):
        yz = yc * _silu(u_ref[sl, 0:256])
        halves = []
        for g in range(2):
            yg = yz[:, g * 128:(g + 1) * 128]
            halves.append(yg * lax.rsqrt(jnp.mean(yg * yg, axis=-1, keepdims=True) + EPS))
        y_ref[sl, :] = jnp.concatenate(halves, axis=-1) * nw
    yield


HG_SUB = 16


def _stack_rows(x, mk):
    n = x.shape[0]
    r = _iota((4 * n, 1), 0)
    blk = (r >= n).astype(jnp.int32) + (r >= 2 * n).astype(jnp.int32) + (r >= 3 * n).astype(jnp.int32)
    x = x.astype(BF16)
    return jnp.where(blk == mk.lane_head, jnp.concatenate([x, x, x, x], axis=0), jnp.zeros((), BF16))


def _hgrn_intra_sub(qq, kk, gc, i_in, a, mk):
    half = HG_SUB // 2
    t_half = _iota((half, 1), 0)
    lo = a * HG_SUB
    q_a, k_a, g_a, i_a = qq[lo:lo + HG_SUB], kk[lo:lo + HG_SUB], gc[lo:lo + HG_SUB], i_in[lo:lo + HG_SUB]
    q_t, q_b, g_t, g_b = q_a[0:half], q_a[half:], g_a[0:half], g_a[half:]
    tops, bots = [], []
    for s in range(HG_SUB):
        k_s, g_s = k_a[s:s + 1, :], g_a[s:s + 1, :]
        if s < half:
            tops.append(q_t * k_s * _decay(g_t - g_s, (t_half >= s).astype(F32), jnp.exp2))
            bots.append(q_b * k_s * jnp.exp2(g_b - g_s))
        else:
            bots.append(q_b * k_s * _decay(g_b - g_s, (t_half >= s - half).astype(F32), jnp.exp2))
    z = _mm(jnp.concatenate(tops + bots, axis=0), mk.ones_bd)
    z_t, z_b = z[0:half * half], z[half * half:]
    o_t = z_t[0:half] * i_a[0:1, :]
    for s in range(1, half):
        o_t = o_t + z_t[s * half:(s + 1) * half] * i_a[s:s + 1, :]
    o_b = z_b[0:half] * i_a[0:1, :]
    for s in range(1, HG_SUB):
        o_b = o_b + z_b[s * half:(s + 1) * half] * i_a[s:s + 1, :]
    o = jnp.concatenate([o_t, o_b], axis=0)
    if a > 0:
        r = gc[lo - 1:lo, :]
        sc = _mm_nt(q_a * jnp.exp2(g_a - r), _stack_rows(kk[0:lo] * jnp.exp2(r - gc[0:lo]), mk))
        o = o + _mm(sc, _stack_rows(i_in[0:lo], mk))
    return o


def _hgrn_steps(mk, u_ref, lbl_ref, nw_ref, y_ref, st_ref, *, layer):
    logits = lbl_ref[...]
    e = jnp.exp(logits - jnp.max(logits, axis=0, keepdims=True))
    prob = e / jnp.sum(e, axis=0, keepdims=True)
    lb = jnp.sum(prob[0:layer + 1], axis=0, keepdims=True) - prob[0:1]

    chunks = _chunks(u_ref.shape[0])
    f = lb + (1.0 - lb) * _sigmoid(u_ref[:, 256:512])
    kk_all = 1.0 - f
    qq_all = _silu(u_ref[:, 0:256])
    yield
    gc_all = _chunk_cumsum(jnp.log(f)) * LOG2E
    yield

    qq = [qq_all[sl] for sl in chunks]
    kk = [kk_all[sl] for sl in chunks]
    gc = [gc_all[sl] for sl in chunks]
    ii = [u_ref[sl, 512:768] for sl in chunks]
    subs = [[] for _ in chunks]
    for a in range(CHUNK // HG_SUB):
        for c in range(len(chunks)):
            subs[c].append(_hgrn_intra_sub(qq[c], kk[c], gc[c], ii[c], a, mk))
            yield
    o_i = [jnp.concatenate(s, axis=0) for s in subs]
    qe = [q * jnp.exp2(g) for q, g in zip(qq, gc)]
    g_last = [g[CHUNK - 1:CHUNK, :] for g in gc]
    yield
    d_s = [_mm_tn(i, k * jnp.exp2(gl - g)) * mk.bd_f32
           for i, k, g, gl in zip(ii, kk, gc, g_last)]
    decay = [jnp.exp2(gl) for gl in g_last]
    yield

    st = st_ref[...]
    st_at = []
    for i in range(len(chunks)):
        st_at.append(st.astype(BF16))
        st = decay[i] * st + d_s[i]
    st_ref[...] = st
    yield
    o = [oi + _mm_nt(q, s) for oi, q, s in zip(o_i, qe, st_at)]
    yield
    on = [_head_rmsnorm(x, mk) for x in o]
    yield
    nw = nw_ref[...]
    for sl, x in zip(chunks, on):
        y_ref[sl, :] = x * nw * _silu(u_ref[sl, 768:1024])
    yield


def _gdn_steps(mk, u_ref, gb_ref, ga_ref, cw_ref, alog_ref, dtb_ref, nw_ref, y_ref, s_ref, prev_ref):
    rows = u_ref.shape[0]
    chunks = _chunks(rows)
    raw = u_ref[:, 0:768]
    qkv = _silu(_causal_conv(raw, prev_ref[...], cw_ref[...]))
    prev_ref[...] = raw[rows - 8:rows, :]
    yield
    q, k, v = qkv[:, 0:256], qkv[:, 256:512], qkv[:, 512:768]
    qn = q * lax.rsqrt(_mm(q * q, mk.ones_bd) + EPS) * (HEAD_DIM ** -0.5)
    kn = k * lax.rsqrt(_mm(k * k, mk.ones_bd) + EPS)
    yield
    gcum = _chunk_cumsum(-jnp.exp(alog_ref[...]) * _softplus(ga_ref[...] + dtb_ref[...]))
    be = _expand4(_sigmoid(gb_ref[...]), mk)
    ge = _expand4(gcum, mk)
    yield
    e_g = jnp.exp(ge)
    kb = kn * be
    vb = v * be
    kbg = kb * e_g
    q_dec = qn * e_g
    eye = jnp.where(mk.diag, 1.0, 0.0)
    yield

    e_dec = [jnp.exp(jnp.minimum(ge[sl] - _rowvec(ge[sl], mk), 0.0)) for sl in chunks]
    yield
    qk = [_mm_nt(jnp.concatenate([kb[sl], qn[sl]], axis=0), _bd(kn[sl], mk)) for sl in chunks]
    yield
    a_p = [x[0:CHUNK] * (e * mk.strict_f) for x, e in zip(qk, e_dec)]
    attn = [x[CHUNK:2 * CHUNK] * (e * mk.causal_f) for x, e in zip(qk, e_dec)]

    pw = [_mm(-x, _bd(-x, mk)) for x in a_p]
    t0 = [eye - x for x in a_p]
    yield
    for j in range(1, 6):
        r = [_mm(jnp.concatenate([t, m], axis=0), _bd(m, mk)) for t, m in zip(t0, pw)]
        t0 = [t + x[0:CHUNK] for t, x in zip(t0, r)]
        pw = [x[CHUNK:2 * CHUNK] for x in r]
        yield
    a_t0 = []
    for x, t in zip(a_p, t0):
        a_hi, a_lo = _split2(x)
        t_hi, t_lo = _split2(t)
        r = jnp.dot(jnp.concatenate([a_hi, a_lo], axis=0), _bd(t_hi, mk).astype(BF16), preferred_element_type=F32)
        a_t0.append(r[0:CHUNK] + r[CHUNK:2 * CHUNK]
                    + jnp.dot(a_hi, _bd(t_lo, mk).astype(BF16), preferred_element_type=F32))
    yield
    t_mat = [t + _mm(t, _bd(eye - t - at, mk)) for t, at in zip(t0, a_t0)]
    yield

    uw = [_mm(t, jnp.concatenate([_bd(vb[sl], mk), _bd(kbg[sl], mk)], axis=1)) for t, sl in zip(t_mat, chunks)]
    u = [x[:, 0:256] for x in uw]
    w = [x[:, 256:512] for x in uw]
    yield
    au = [_mm(at, jnp.concatenate([_bd(uu, mk), _bd(ww, mk)], axis=1)) for at, uu, ww in zip(attn, u, w)]
    o_i = [x[:, 0:256] for x in au]
    q2 = [q_dec[sl] - x[:, 256:512] for sl, x in zip(chunks, au)]
    yield
    last = [slice(sl.stop - 1, sl.stop) for sl in chunks]
    k_dec = [kn[sl] * _expand4(jnp.exp(gcum[ls] - gcum[sl]), mk) for sl, ls in zip(chunks, last)]
    fg = [_mm_tn(kd, jnp.concatenate([ww, uu], axis=1)) for kd, ww, uu in zip(k_dec, w, u)]
    yield
    f_p = [-_unbd(x[:, 0:256], mk) for x in fg]
    g_p = [_unbd(x[:, 256:512], mk) for x in fg]
    decay = [_expand4(jnp.exp(gcum[ls]), mk) for ls in last]
    yield

    s_p = s_ref[...]
    s_at = []
    for i in range(len(chunks)):
        s_bd = _bd(s_p, mk)
        s_at.append(s_bd)
        s_p = decay[i] * s_p + _mm(f_p[i], s_bd) + g_p[i]
        yield
    s_ref[...] = s_p
    o = [oi + _mm(q, s) for oi, q, s in zip(o_i, q2, s_at)]
    yield
    on = [_head_rmsnorm(x, mk) for x in o]
    yield
    nw = nw_ref[...]
    for sl, x in zip(chunks, on):
        y_ref[sl, :] = x * nw * _silu(u_ref[sl, 768:1024])
    yield


_DONE = object()


N_MIXER_PARAMS = 15


def _mixer_steps(um_ref, ug_ref, params, y_ref, states, layer):
    ib, fb, nw_a, cw_b, cb_b, dtb_b, alog_b, dskip_b, nw_b, lbl, nw_c, cw_d, alog_d, dtb_d, nw_d = params
    c_ref, n_ref, m_ref, sb_ref, prevb_ref, st_ref, sd_ref, prevd_ref = states
    mk = _Masks()
    win = lambda j: um_ref.at[:, pl.ds(j * 4 * GROUP_WIDTH, 4 * GROUP_WIDTH)]
    gate = lambda j: ug_ref.at[:, pl.ds(j * LANES, LANES)]
    out = lambda j: y_ref.at[:, pl.ds(j * GROUP_WIDTH, GROUP_WIDTH)]
    return [
        (_gdn_steps(mk, win(3), gate(3), gate(4), cw_d, alog_d, dtb_d, nw_d, out(3), sd_ref, prevd_ref), 1.0),
        (_mlstm_steps(mk, win(0), gate(0), gate(1), ib, fb, nw_a, out(0), c_ref, n_ref, m_ref), MLSTM_PACE),
        (_hgrn_steps(mk, win(2), lbl, nw_c, out(2), st_ref, layer=layer), 1.0),
        (_mamba_steps(mk, win(1), gate(2), cw_b, cb_b, dtb_b, alog_b, dskip_b, nw_b, out(1), sb_ref, prevb_ref),
         MAMBA_PACE),
    ]


def _run_round_robin(steps):
    rnd = 0
    while steps:
        due = [(g, p) for g, p in steps if int((rnd + 1) * p) > int(rnd * p)]
        done = [g for g, _ in due if next(g, _DONE) is _DONE]
        steps = [(g, p) for g, p in steps if g not in done]
        rnd += 1


def _mixers_body(*refs, layer):
    um_ref, ug_ref = refs[0:2]
    params = refs[2:2 + N_MIXER_PARAMS]
    y_ref = refs[2 + N_MIXER_PARAMS]
    states = refs[3 + N_MIXER_PARAMS:]

    @pl.when(pl.program_id(1) == 0)
    def _():
        for r in states:
            r[...] = jnp.zeros_like(r)

    _run_round_robin(_mixer_steps(um_ref, ug_ref, params, y_ref, states, layer))


def _mixer_param_specs(params, layer):
    return [pl.BlockSpec((None,) + p.shape[1:], lambda *_: (layer, 0, 0)) if per_layer
            else pl.BlockSpec(p.shape, lambda *_, nd=p.ndim: (0,) * nd) for p, per_layer in params]


def _mixer_state_shapes():
    state = pltpu.VMEM((GROUP_WIDTH, GROUP_WIDTH), F32)
    conv_tail = pltpu.VMEM((8, 3 * GROUP_WIDTH), F32)
    return [state, pltpu.VMEM((8, GROUP_WIDTH), F32), pltpu.VMEM((8, LANES), F32),
            state, conv_tail,
            state,
            pltpu.VMEM((HEAD_DIM, GROUP_WIDTH), F32), conv_tail]


def _mixers_call(u_main, u_gate, params, layer, seq_len):
    n = u_main.shape[0]
    tb = min(MIX_TOKENS, seq_len)
    tiles = seq_len // tb
    blk = lambda w: pl.BlockSpec((tb, w), lambda bi, i: (bi * tiles + i, 0))
    return pl.pallas_call(
        functools.partial(_mixers_body, layer=layer),
        grid=(n // seq_len, tiles),
        in_specs=[blk(u_main.shape[1]), blk(u_gate.shape[1])] + _mixer_param_specs(params, layer),
        out_specs=blk(N_MIXERS * GROUP_WIDTH),
        out_shape=jax.ShapeDtypeStruct((n, N_MIXERS * GROUP_WIDTH), F32),
        scratch_shapes=_mixer_state_shapes(),
        compiler_params=pltpu.CompilerParams(dimension_semantics=("arbitrary", "arbitrary"),
                                             vmem_limit_bytes=VMEM_LIMIT),
        name="mixers",
    )(u_main, u_gate, *[p for p, _ in params])


def _rows(v):
    return v.astype(F32)[:, None, :]


def _head_rows(v):
    return jnp.pad(_rows(v), ((0, 0), (0, 0), (0, LANES - v.shape[1])))


def _mixer_params(mlstm_i_bias, mlstm_f_bias, mlstm_norm, mamba_conv_w, mamba_conv_b, mamba_dt_bias, mamba_a_log,
                  mamba_d, mamba_norm, hgrn_lb_logits, hgrn_norm, gdn_conv_w, gdn_a_log, gdn_dt_bias, gdn_norm):
    per_layer = [
        _head_rows(mlstm_i_bias), _head_rows(mlstm_f_bias), _rows(mlstm_norm),
        mamba_conv_w.astype(F32), _rows(mamba_conv_b), _head_rows(mamba_dt_bias), _head_rows(mamba_a_log),
        _rows(jnp.repeat(mamba_d, HEAD_DIM, axis=1)), _rows(mamba_norm)]
    hgrn = [(hgrn_lb_logits.astype(F32), False), (_rows(hgrn_norm), True)]
    gdn = [gdn_conv_w.astype(F32), _head_rows(gdn_a_log), _head_rows(gdn_dt_bias), _rows(gdn_norm)]
    return [(p, True) for p in per_layer] + hgrn + [(p, True) for p in gdn]


_MLSTM0, _MAMBA0, _HGRN0, _GDN0 = 0, 1032, 2060, 3084
_MAIN_COLS = ((_MLSTM0, _MLSTM0 + 1024), (_MAMBA0, _MAMBA0 + 1024), (_HGRN0, _HGRN0 + 1024),
              (_GDN0, _GDN0 + 1024))
_GATE_COLS = (_MLSTM0 + 1024, _MLSTM0 + 1028, _MAMBA0 + 1024, _GDN0 + 1024, _GDN0 + 1028)
_MAIN_STARTS = tuple(a for a, _ in _MAIN_COLS)
_D_IN_PAD = 33 * LANES


def kernel(x, ffn1_norm, ffn1_w_gate, ffn1_w_up, ffn1_w_down, mix_norm, w_in, w_out, mlstm_i_bias, mlstm_f_bias, mlstm_norm, mamba_conv_w, mamba_conv_b, mamba_dt_bias, mamba_a_log, mamba_d, mamba_norm, hgrn_lb_logits, hgrn_norm, gdn_conv_w, gdn_a_log, gdn_dt_bias, gdn_norm, ffn2_norm, ffn2_w_gate, ffn2_w_up, ffn2_w_down, final_norm):
    b, s, d = x.shape
    depth = w_in.shape[0]
    x2d = x.reshape(b * s, d)
    ffn1 = (_rows(ffn1_norm), ffn1_w_gate, ffn1_w_up, ffn1_w_down)
    ffn2 = (_rows(ffn2_norm), ffn2_w_gate, ffn2_w_up, ffn2_w_down)
    w_pad = _to_bf16(w_in, _D_IN_PAD)
    w_o = _to_bf16(w_out)
    mix_nw = _rows(mix_norm)
    params = _mixer_params(mlstm_i_bias, mlstm_f_bias, mlstm_norm, mamba_conv_w, mamba_conv_b, mamba_dt_bias,
                           mamba_a_log, mamba_d, mamba_norm, hgrn_lb_logits, hgrn_norm, gdn_conv_w, gdn_a_log,
                           gdn_dt_bias, gdn_norm)
    for l in range(depth):
        x2d = _ffn_call(x2d, *ffn1, l)
        um, ug = _inproj_call(x2d, mix_nw, w_pad, l)
        y = _mixers_call(um, ug, params, l, s)
        x2d = _ffn_call(x2d, *ffn2, l, mix=(y, w_o),
                        final_w=final_norm.astype(F32).reshape(1, d) if l == depth - 1 else None)
    return x2d.reshape(b, s, d)
```

```python
import functools

import numpy as np
import jax
import jax.numpy as jnp
from jax import lax
from jax.experimental import pallas as pl
from jax.experimental.pallas import tpu as pltpu

F32 = jnp.float32
BF16 = jnp.bfloat16

D_MODEL = 1024
CHUNK = 64
N_HEADS = 4
HEAD_DIM = 64
GROUP_WIDTH = N_HEADS * HEAD_DIM
SSM_STATE = 128
CONV_K = 4
D_FF = 2816
EPS = 1e-6
NEG_BIG = -1e30
LOG2E = 1.4426950408889634
LANES = 128
N_GATES = 5

FFN_TOKENS = 512
CAST_STEPS = 2
W_PREP_STEPS = 8
MXU_DIM = 256
FF_SPLIT = 6 * MXU_DIM
N_MIXERS = 4
MLSTM_PACE = 0.7
MAMBA_PACE = 0.55
MIX_TOKENS = 256
VMEM_LIMIT = 56 * 1024 * 1024


def _iota(shape, dim):
    return lax.broadcasted_iota(jnp.int32, shape, dim)


def _mm(a, b):
    return jnp.dot(a.astype(BF16), b.astype(BF16), preferred_element_type=F32)


def _mm_nt(a, b):
    return lax.dot_general(a.astype(BF16), b.astype(BF16), (((1,), (1,)), ((), ())),
                           preferred_element_type=F32)


def _mm_tn(a, b):
    return lax.dot_general(a.astype(BF16), b.astype(BF16), (((0,), (0,)), ((), ())),
                           preferred_element_type=F32)


def _split2(a):
    hi = a.astype(BF16)
    return hi, (a - hi.astype(F32)).astype(BF16)


def _mm_hilo(a, b_bf16):
    hi, lo = _split2(a)
    return (jnp.dot(hi, b_bf16, preferred_element_type=F32)
            + jnp.dot(lo, b_bf16, preferred_element_type=F32))


def _sigmoid(x):
    return 0.5 * jnp.tanh(0.5 * x) + 0.5


def _silu(x):
    h = 0.5 * x
    return h * jnp.tanh(h) + h


def _softplus(x):
    return jnp.maximum(x, 0.0) + jnp.log1p(jnp.exp(-jnp.abs(x)))


class _Masks:
    def __init__(self):
        row = _iota((GROUP_WIDTH, 1), 0)
        lane = _iota((1, GROUP_WIDTH), 1)
        t = _iota((CHUNK, 1), 0)
        self.row, self.lane = row, lane
        self.lane_head = lane >> 6
        block_diag = (row >> 6) == self.lane_head
        self.ones_bd = block_diag.astype(BF16)
        self.bd_f32 = block_diag.astype(F32)
        s = lane & 63
        self.causal_f = (t >= s).astype(F32)
        self.strict_f = (t > s).astype(F32)
        self.diag = t == s


def _bd(x, mk):
    x = x.astype(BF16)
    return jnp.concatenate([x, x, x, x], axis=0) * mk.ones_bd


def _unbd(y, mk):
    ym = y * mk.bd_f32
    return ym[0:64] + ym[64:128] + ym[128:192] + ym[192:256]


def _decay(arg, mask_f, exp=jnp.exp):
    return exp(jnp.minimum(arg, 0.0)) * mask_f


def _expand4(g, mk):
    first = _iota((1, LANES), 1) < HEAD_DIM
    return jnp.concatenate([jnp.where(first, g[:, 0:1], g[:, 1:2]), jnp.where(first, g[:, 2:3], g[:, 3:4])], axis=1)


def _rowvec(ge, mk):
    return jnp.sum(jnp.where(mk.diag, ge, 0.0), axis=0, keepdims=True)


def _chunk_cumsum(x):
    rows = x.shape[0]
    r, c = _iota((rows, rows), 0), _iota((rows, rows), 1)
    tril = (((r >> 6) == (c >> 6)) & (r >= c)).astype(BF16)
    x1 = x.astype(BF16)
    r1 = x - x1.astype(F32)
    x2 = r1.astype(BF16)
    x3 = (r1 - x2.astype(F32)).astype(BF16)
    return (jnp.dot(tril, x1, preferred_element_type=F32) + jnp.dot(tril, x2, preferred_element_type=F32)
            + jnp.dot(tril, x3, preferred_element_type=F32))


def _chunk_cummax(x):
    t = _iota((x.shape[0], 1), 0) & 63
    k = 1
    while k < CHUNK:
        x = jnp.maximum(x, jnp.where(t >= k, pltpu.roll(x, k, 0), NEG_BIG))
        k *= 2
    return x


def _causal_conv(x, prev8, w):
    acc = x * w[CONV_K - 1:CONV_K, :]
    r8 = _iota((8, 1), 0)
    for j in range(1, CONV_K):
        xr = pltpu.roll(x, j, 0)
        top = jnp.where(r8 < j, pltpu.roll(prev8, j, 0), xr[0:8])
        xs = jnp.concatenate([top, xr[8:]], axis=0)
        acc = acc + xs * w[CONV_K - 1 - j:CONV_K - j, :]
    return acc


def _head_rmsnorm(o, mk):
    return o * lax.rsqrt(_mm(o * o, mk.ones_bd) * (1.0 / HEAD_DIM) + EPS)


def _chunks(n_rows):
    return [slice(c * CHUNK, (c + 1) * CHUNK) for c in range(n_rows // CHUNK)]


def _rms(x, w):
    return x * lax.rsqrt(jnp.mean(x * x, axis=-1, keepdims=True) + EPS) * w


def _ffn_body(*refs, with_mix, with_final):
    it = iter(refs)
    x_ref = next(it)
    if with_mix:
        y_ref, wo_ref = next(it), next(it)
    nw_ref, wg_ref, wu_ref, wd_ref = next(it), next(it), next(it), next(it)
    if with_final:
        fw_ref = next(it)
    o_ref = next(it)
    wg_s, wu_s, wd_s = next(it), next(it), next(it)
    step = pl.program_id(0)

    @pl.when(step < W_PREP_STEPS)
    def _():
        for src, dst in ((wg_ref, wg_s), (wu_ref, wu_s), (wd_ref, wd_s)):
            rows = src.shape[0]
            dst[pl.ds(pl.multiple_of(step * rows, rows), rows), :] = src[...].astype(BF16)

    @pl.when(step >= W_PREP_STEPS)
    def _():
        x = x_ref[...]
        if with_mix:
            x = x + jnp.dot(y_ref[...].astype(BF16), wo_ref[...], preferred_element_type=F32)
        xb = _rms(x, nw_ref[...]).astype(BF16)
        acc = jnp.zeros(x.shape, F32)
        for sl in (slice(0, FF_SPLIT), slice(FF_SPLIT, D_FF)):
            g = jnp.dot(xb, wg_s[:, sl], preferred_element_type=F32)
            u = jnp.dot(xb, wu_s[:, sl], preferred_element_type=F32)
            h = (_silu(g) * u).astype(BF16)
            acc = acc + jnp.dot(h, wd_s[sl, :], preferred_element_type=F32)
        out = x + 0.5 * acc
        if with_final:
            out = _rms(out, fw_ref[...])
        o_ref[...] = out


def _const_spec(shape):
    return pl.BlockSpec(shape, lambda *_: (0,) * len(shape), pipeline_mode=pl.Buffered(1))


def _layer_spec(a, layer):
    return pl.BlockSpec((None,) + a.shape[1:], lambda *_: (layer, 0, 0), pipeline_mode=pl.Buffered(1))


def _cast_body(w_ref, o_ref):
    cols = w_ref.shape[1]
    if o_ref.shape[1] != cols:
        o_ref[...] = jnp.zeros(o_ref.shape, BF16)
    o_ref[:, 0:cols] = w_ref[...].astype(BF16)


def _to_bf16(w, cols_out=None):
    depth, rows, cols = w.shape
    cols_out = cols_out or cols
    tr = rows // CAST_STEPS
    return pl.pallas_call(
        _cast_body,
        grid=(depth, CAST_STEPS),
        in_specs=[pl.BlockSpec((None, tr, cols), lambda l, i: (l, i, 0))],
        out_specs=pl.BlockSpec((None, tr, cols_out), lambda l, i: (l, i, 0)),
        out_shape=jax.ShapeDtypeStruct((depth, rows, cols_out), BF16),
        compiler_params=pltpu.CompilerParams(dimension_semantics=("arbitrary", "arbitrary"),
                                             vmem_limit_bytes=VMEM_LIMIT),
        name="cast",
    )(w)


def _ffn_call(x2d, nw, wg, wu, wd, layer, mix=None, final_w=None):
    n = x2d.shape[0]
    tm = min(FFN_TOKENS, n)
    tok = lambda w: pl.BlockSpec((tm, w), lambda i: (jnp.maximum(i - W_PREP_STEPS, 0), 0))
    w_rows = lambda w: pl.BlockSpec((None, w.shape[1] // W_PREP_STEPS, w.shape[2]),
                                    lambda i: (layer, jnp.minimum(i, W_PREP_STEPS - 1), 0))
    args, specs = [x2d], [tok(D_MODEL)]
    if mix is not None:
        y, wo = mix
        args += [y, wo]
        specs += [tok(y.shape[1]), _layer_spec(wo, layer)]
    args += [nw, wg, wu, wd]
    specs += [_layer_spec(nw, layer), w_rows(wg), w_rows(wu), w_rows(wd)]
    if final_w is not None:
        args.append(final_w)
        specs.append(_const_spec(final_w.shape))
    return pl.pallas_call(
        functools.partial(_ffn_body, with_mix=mix is not None, with_final=final_w is not None),
        grid=(W_PREP_STEPS + n // tm,),
        in_specs=specs,
        out_specs=tok(D_MODEL),
        out_shape=jax.ShapeDtypeStruct((n, D_MODEL), F32),
        scratch_shapes=[pltpu.VMEM(w.shape[1:], BF16) for w in (wg, wu, wd)],
        compiler_params=pltpu.CompilerParams(dimension_semantics=("arbitrary",),
                                             vmem_limit_bytes=VMEM_LIMIT),
        name="ffn",
    )(*args)


def _inproj_body(x_ref, nw_ref, w_ref, um_ref, ug_ref):
    xb = _rms(x_ref[...], nw_ref[...]).astype(BF16)
    u = jnp.dot(xb, w_ref[...], preferred_element_type=F32)
    for j, start in enumerate(_MAIN_STARTS):
        lo = (start // LANES) * LANES
        width = 4 * GROUP_WIDTH
        if start == lo:
            blk = u[:, lo:lo + width]
        else:
            win = u[:, lo:lo + width + LANES]
            blk = pltpu.roll(win, width + LANES - (start - lo), 1)[:, 0:width]
        um_ref[:, j * width:(j + 1) * width] = blk
    head_lane = _iota((1, LANES), 1) < N_HEADS
    for j, start in enumerate(_GATE_COLS):
        lo = (start // LANES) * LANES
        tile = u[:, lo:lo + LANES]
        if start != lo:
            tile = pltpu.roll(tile, LANES - (start - lo), 1)
        ug_ref[:, j * LANES:(j + 1) * LANES] = jnp.where(head_lane, tile, 0.0)


def _inproj_call(x2d, nw, w_pad, layer):
    n = x2d.shape[0]
    tm = min(FFN_TOKENS, n)
    wm, wg = 4 * 4 * GROUP_WIDTH, N_GATES * LANES
    return pl.pallas_call(
        _inproj_body,
        grid=(n // tm,),
        in_specs=[pl.BlockSpec((tm, D_MODEL), lambda i: (i, 0)), _layer_spec(nw, layer), _layer_spec(w_pad, layer)],
        out_specs=[pl.BlockSpec((tm, wm), lambda i: (i, 0)), pl.BlockSpec((tm, wg), lambda i: (i, 0))],
        out_shape=[jax.ShapeDtypeStruct((n, wm), F32), jax.ShapeDtypeStruct((n, wg), F32)],
        compiler_params=pltpu.CompilerParams(dimension_semantics=("arbitrary",),
                                             vmem_limit_bytes=VMEM_LIMIT),
        name="inproj",
    )(x2d, nw, w_pad)


def _mlstm_steps(mk, u_ref, gi_ref, gf_ref, ib_ref, fb_ref, nw_ref, y_ref, c_ref, n_ref, m_ref):
    chunks = _chunks(u_ref.shape[0])
    li = gi_ref[...] + ib_ref[...]
    xf = gf_ref[...] + fb_ref[...]
    lf = jnp.minimum(xf, 0.0) - jnp.log1p(jnp.exp(-jnp.abs(xf)))
    bcum = _chunk_cumsum(lf)
    a = li - bcum
    yield
    cmax = _chunk_cummax(a)
    yield

    m_run = m_ref[0:1, :]
    m_start, m_tot = [], []
    for sl in chunks:
        last = slice(sl.stop - 1, sl.stop)
        m_start.append(m_run)
        m_tot.append(jnp.maximum(m_run, cmax[last]))
        m_run = bcum[last] + m_tot[-1]
    m_ref[0:1, :] = m_run
    yield

    qs = [u_ref[sl, 0:256] * (HEAD_DIM ** -0.5) for sl in chunks]
    ks = [u_ref[sl, 256:512] for sl in chunks]
    vs = [u_ref[sl, 512:768] for sl in chunks]
    mx = [jnp.maximum(m0, cmax[sl]) for m0, sl in zip(m_start, chunks)]
    arow = [_rowvec(_expand4(a[sl], mk), mk) for sl in chunks]
    yield
    dmat = [_decay(ar - _expand4(m, mk), mk.causal_f) for ar, m in zip(arow, mx)]
    yield
    p = [_mm_nt(q, _bd(k, mk)) * d for q, k, d in zip(qs, ks, dmat)]
    yield
    num_i = [_mm(pp, _bd(v, mk)) for pp, v in zip(p, vs)]
    yield
    den_i = [_mm_hilo(pp, mk.ones_bd) for pp in p]
    yield
    w_int = [_expand4(jnp.exp(m0 - m), mk) for m0, m in zip(m_start, mx)]
    bound = [_expand4(jnp.exp(-(bcum[sl] + m)), mk) for sl, m in zip(chunks, mx)]
    yield
    kw = [k * _expand4(jnp.exp(a[sl] - mt), mk) for k, sl, mt in zip(ks, chunks, m_tot)]
    decay = [_expand4(jnp.exp(m0 - mt), mk) for m0, mt in zip(m_start, m_tot)]
    yield
    d_c = [_mm_tn(kk, v) * mk.bd_f32 for kk, v in zip(kw, vs)]
    d_n = [jnp.sum(kk, axis=0, keepdims=True) for kk in kw]
    yield

    c_bd, n_row = c_ref[...], n_ref[0:1, :]
    c_at, n_at = [], []
    for i in range(len(chunks)):
        c_at.append(c_bd.astype(BF16))
        n_at.append(n_row)
        c_bd = decay[i] * c_bd + d_c[i]
        n_row = decay[i] * n_row + d_n[i]
    c_ref[...] = c_bd
    n_ref[0:1, :] = n_row
    yield
    q_c = [_mm(q, c) for q, c in zip(qs, c_at)]
    yield
    q_n = [_mm_hilo(q * n, mk.ones_bd) for q, n in zip(qs, n_at)]
    yield
    h = [(ni + w * qc) / jnp.maximum(jnp.abs(di + w * qn), b)
         for ni, di, w, qc, qn, b in zip(num_i, den_i, w_int, q_c, q_n, bound)]
    yield
    hn = [_head_rmsnorm(x, mk) for x in h]
    yield
    nw = nw_ref[...]
    for sl, x in zip(chunks, hn):
        y_ref[sl, :] = x * nw * _sigmoid(u_ref[sl, 768:1024])
    yield


def _mamba_steps(mk, u_ref, gdt_ref, cw_ref, cb_ref, dtb_ref, alog_ref, dskip_ref, nw_ref, y_ref, s_ref, prev_ref):
    rows = u_ref.shape[0]
    chunks = _chunks(rows)
    raw = u_ref[:, 256:1024]
    xbc = _silu(_causal_conv(raw, prev_ref[...], cw_ref[...]) + cb_ref[...])
    prev_ref[...] = raw[rows - 8:rows, :]
    yield
    dt = _softplus(gdt_ref[...] + dtb_ref[...])
    cum = _chunk_cumsum(dt * (-jnp.exp(alog_ref[...])))
    grp_b = (((mk.row >> 6) >> 1) == (mk.lane >> 7)).astype(BF16)
    grp_s = ((mk.row >> 7) == (mk.lane_head >> 1)).astype(F32)
    yield

    xs = [xbc[sl, 0:256] for sl in chunks]
    bm = [xbc[sl, 256:512] for sl in chunks]
    cm = [xbc[sl, 512:768] for sl in chunks]
    cum_e = [_expand4(cum[sl], mk) for sl in chunks]
    yield
    seg = [_decay(ce - _rowvec(ce, mk), mk.causal_f) for ce in cum_e]
    yield
    dtrow = [_rowvec(_expand4(dt[sl], mk), mk) for sl in chunks]
    yield
    p = [_mm_nt(c, jnp.concatenate([b.astype(BF16)] * 4, axis=0) * grp_b) * sg * dr
         for c, b, sg, dr in zip(cm, bm, seg, dtrow)]
    yield
    y_i = [_mm(pp, _bd(x, mk)) + x * dskip_ref[...] for pp, x in zip(p, xs)]
    yield
    last = [slice(sl.stop - 1, sl.stop) for sl in chunks]
    xw = [x * _expand4(jnp.exp(cum[ls] - cum[sl]) * dt[sl], mk) for x, sl, ls in zip(xs, chunks, last)]
    yield
    d_s = [_mm_tn(b, w) * grp_s for b, w in zip(bm, xw)]
    yield
    decay = [_expand4(jnp.exp(cum[ls]), mk) for ls in last]
    e_cum = [jnp.exp(ce) for ce in cum_e]
    yield

    s_mat = s_ref[...]
    s_at = []
    for i in range(len(chunks)):
        s_at.append(s_mat.astype(BF16))
        s_mat = decay[i] * s_mat + d_s[i]
    s_ref[...] = s_mat
    yield
    y = [yi + e * _mm(c, s) for yi, e, c, s in zip(y_i, e_cum, cm, s_at)]
    yield
    nw = nw_ref[...]
    for sl, yc in zip(chunks, y):
        yz = yc * _silu(u_ref[sl, 0:256])
        halves = []
        for g in range(2):
            yg = yz[:, g * 128:(g + 1) * 128]
            halves.append(yg * lax.rsqrt(jnp.mean(yg * yg, axis=-1, keepdims=True) + EPS))
        y_ref[sl, :] = jnp.concatenate(halves, axis=-1) * nw
    yield


HG_SUB = 16


def _stack_rows(x, mk):
    n = x.shape[0]
    r = _iota((4 * n, 1), 0)
    blk = (r >= n).astype(jnp.int32) + (r >= 2 * n).astype(jnp.int32) + (r >= 3 * n).astype(jnp.int32)
    x = x.astype(BF16)
    return jnp.where(blk == mk.lane_head, jnp.concatenate([x, x, x, x], axis=0), jnp.zeros((), BF16))


def _hgrn_intra_sub(qq, kk, gc, i_in, a, mk):
    half = HG_SUB // 2
    t_half = _iota((half, 1), 0)
    lo = a * HG_SUB
    q_a, k_a, g_a, i_a = qq[lo:lo + HG_SUB], kk[lo:lo + HG_SUB], gc[lo:lo + HG_SUB], i_in[lo:lo + HG_SUB]
    q_t, q_b, g_t, g_b = q_a[0:half], q_a[half:], g_a[0:half], g_a[half:]
    tops, bots = [], []
    for s in range(HG_SUB):
        k_s, g_s = k_a[s:s + 1, :], g_a[s:s + 1, :]
        if s < half:
            tops.append(q_t * k_s * _decay(g_t - g_s, (t_half >= s).astype(F32), jnp.exp2))
            bots.append(q_b * k_s * jnp.exp2(g_b - g_s))
        else:
            bots.append(q_b * k_s * _decay(g_b - g_s, (t_half >= s - half).astype(F32), jnp.exp2))
    z = _mm(jnp.concatenate(tops + bots, axis=0), mk.ones_bd)
    z_t, z_b = z[0:half * half], z[half * half:]
    o_t = z_t[0:half] * i_a[0:1, :]
    for s in range(1, half):
        o_t = o_t + z_t[s * half:(s + 1) * half] * i_a[s:s + 1, :]
    o_b = z_b[0:half] * i_a[0:1, :]
    for s in range(1, HG_SUB):
        o_b = o_b + z_b[s * half:(s + 1) * half] * i_a[s:s + 1, :]
    o = jnp.concatenate([o_t, o_b], axis=0)
    if a > 0:
        r = gc[lo - 1:lo, :]
        sc = _mm_nt(q_a * jnp.exp2(g_a - r), _stack_rows(kk[0:lo] * jnp.exp2(r - gc[0:lo]), mk))
        o = o + _mm(sc, _stack_rows(i_in[0:lo], mk))
    return o


def _hgrn_steps(mk, u_ref, lbl_ref, nw_ref, y_ref, st_ref, *, layer):
    logits = lbl_ref[...]
    e = jnp.exp(logits - jnp.max(logits, axis=0, keepdims=True))
    prob = e / jnp.sum(e, axis=0, keepdims=True)
    lb = jnp.sum(prob[0:layer + 1], axis=0, keepdims=True) - prob[0:1]

    chunks = _chunks(u_ref.shape[0])
    f = lb + (1.0 - lb) * _sigmoid(u_ref[:, 256:512])
    kk_all = 1.0 - f
    qq_all = _silu(u_ref[:, 0:256])
    yield
    gc_all = _chunk_cumsum(jnp.log(f)) * LOG2E
    yield

    qq = [qq_all[sl] for sl in chunks]
    kk = [kk_all[sl] for sl in chunks]
    gc = [gc_all[sl] for sl in chunks]
    ii = [u_ref[sl, 512:768] for sl in chunks]
    subs = [[] for _ in chunks]
    for c in range(len(chunks)):
        for a in range(CHUNK // HG_SUB):
            subs[c].append(_hgrn_intra_sub(qq[c], kk[c], gc[c], ii[c], a, mk))
            yield
    o_i = [jnp.concatenate(s, axis=0) for s in subs]
    qe = [q * jnp.exp2(g) for q, g in zip(qq, gc)]
    g_last = [g[CHUNK - 1:CHUNK, :] for g in gc]
    yield
    d_s = [_mm_tn(i, k * jnp.exp2(gl - g)) * mk.bd_f32
           for i, k, g, gl in zip(ii, kk, gc, g_last)]
    decay = [jnp.exp2(gl) for gl in g_last]
    yield

    st = st_ref[...]
    st_at = []
    for i in range(len(chunks)):
        st_at.append(st.astype(BF16))
        st = decay[i] * st + d_s[i]
    st_ref[...] = st
    yield
    o = [oi + _mm_nt(q, s) for oi, q, s in zip(o_i, qe, st_at)]
    yield
    on = [_head_rmsnorm(x, mk) for x in o]
    yield
    nw = nw_ref[...]
    for sl, x in zip(chunks, on):
        y_ref[sl, :] = x * nw * _silu(u_ref[sl, 768:1024])
    yield


def _gdn_steps(mk, u_ref, gb_ref, ga_ref, cw_ref, alog_ref, dtb_ref, nw_ref, y_ref, s_ref, prev_ref):
    rows = u_ref.shape[0]
    chunks = _chunks(rows)
    raw = u_ref[:, 0:768]
    qkv = _silu(_causal_conv(raw, prev_ref[...], cw_ref[...]))
    prev_ref[...] = raw[rows - 8:rows, :]
    yield
    q, k, v = qkv[:, 0:256], qkv[:, 256:512], qkv[:, 512:768]
    qn = q * lax.rsqrt(_mm(q * q, mk.ones_bd) + EPS) * (HEAD_DIM ** -0.5)
    kn = k * lax.rsqrt(_mm(k * k, mk.ones_bd) + EPS)
    yield
    gcum = _chunk_cumsum(-jnp.exp(alog_ref[...]) * _softplus(ga_ref[...] + dtb_ref[...]))
    be = _expand4(_sigmoid(gb_ref[...]), mk)
    ge = _expand4(gcum, mk)
    yield
    e_g = jnp.exp(ge)
    kb = kn * be
    vb = v * be
    kbg = kb * e_g
    q_dec = qn * e_g
    eye = jnp.where(mk.diag, 1.0, 0.0)
    yield

    e_dec = [jnp.exp(jnp.minimum(ge[sl] - _rowvec(ge[sl], mk), 0.0)) for sl in chunks]
    yield
    qk = [_mm_nt(jnp.concatenate([kb[sl], qn[sl]], axis=0), _bd(kn[sl], mk)) for sl in chunks]
    yield
    a_p = [x[0:CHUNK] * (e * mk.strict_f) for x, e in zip(qk, e_dec)]
    attn = [x[CHUNK:2 * CHUNK] * (e * mk.causal_f) for x, e in zip(qk, e_dec)]

    pw = [_mm(-x, _bd(-x, mk)) for x in a_p]
    t0 = [eye - x for x in a_p]
    yield
    for j in range(1, 6):
        r = [_mm(jnp.concatenate([t, m], axis=0), _bd(m, mk)) for t, m in zip(t0, pw)]
        t0 = [t + x[0:CHUNK] for t, x in zip(t0, r)]
        pw = [x[CHUNK:2 * CHUNK] for x in r]
        yield
    a_t0 = []
    for x, t in zip(a_p, t0):
        a_hi, a_lo = _split2(x)
        t_hi, t_lo = _split2(t)
        r = jnp.dot(jnp.concatenate([a_hi, a_lo], axis=0), _bd(t_hi, mk).astype(BF16), preferred_element_type=F32)
        a_t0.append(r[0:CHUNK] + r[CHUNK:2 * CHUNK]
                    + jnp.dot(a_hi, _bd(t_lo, mk).astype(BF16), preferred_element_type=F32))
    yield
    t_mat = [t + _mm(t, _bd(eye - t - at, mk)) for t, at in zip(t0, a_t0)]
    yield

    uw = [_mm(t, jnp.concatenate([_bd(vb[sl], mk), _bd(kbg[sl], mk)], axis=1)) for t, sl in zip(t_mat, chunks)]
    u = [x[:, 0:256] for x in uw]
    w = [x[:, 256:512] for x in uw]
    yield
    au = [_mm(at, jnp.concatenate([_bd(uu, mk), _bd(ww, mk)], axis=1)) for at, uu, ww in zip(attn, u, w)]
    o_i = [x[:, 0:256] for x in au]
    q2 = [q_dec[sl] - x[:, 256:512] for sl, x in zip(chunks, au)]
    yield
    last = [slice(sl.stop - 1, sl.stop) for sl in chunks]
    k_dec = [kn[sl] * _expand4(jnp.exp(gcum[ls] - gcum[sl]), mk) for sl, ls in zip(chunks, last)]
    fg = [_mm_tn(kd, jnp.concatenate([ww, uu], axis=1)) for kd, ww, uu in zip(k_dec, w, u)]
    yield
    f_p = [-_unbd(x[:, 0:256], mk) for x in fg]
    g_p = [_unbd(x[:, 256:512], mk) for x in fg]
    decay = [_expand4(jnp.exp(gcum[ls]), mk) for ls in last]
    yield

    s_p = s_ref[...]
    s_at = []
    for i in range(len(chunks)):
        s_bd = _bd(s_p, mk)
        s_at.append(s_bd)
        s_p = decay[i] * s_p + _mm(f_p[i], s_bd) + g_p[i]
        yield
    s_ref[...] = s_p
    o = [oi + _mm(q, s) for oi, q, s in zip(o_i, q2, s_at)]
    yield
    on = [_head_rmsnorm(x, mk) for x in o]
    yield
    nw = nw_ref[...]
    for sl, x in zip(chunks, on):
        y_ref[sl, :] = x * nw * _silu(u_ref[sl, 768:1024])
    yield


_DONE = object()


N_MIXER_PARAMS = 15


def _mixer_steps(um_ref, ug_ref, params, y_ref, states, layer):
    ib, fb, nw_a, cw_b, cb_b, dtb_b, alog_b, dskip_b, nw_b, lbl, nw_c, cw_d, alog_d, dtb_d, nw_d = params
    c_ref, n_ref, m_ref, sb_ref, prevb_ref, st_ref, sd_ref, prevd_ref = states
    mk = _Masks()
    win = lambda j: um_ref.at[:, pl.ds(j * 4 * GROUP_WIDTH, 4 * GROUP_WIDTH)]
    gate = lambda j: ug_ref.at[:, pl.ds(j * LANES, LANES)]
    out = lambda j: y_ref.at[:, pl.ds(j * GROUP_WIDTH, GROUP_WIDTH)]
    return [
        (_gdn_steps(mk, win(3), gate(3), gate(4), cw_d, alog_d, dtb_d, nw_d, out(3), sd_ref, prevd_ref), 1.0),
        (_mlstm_steps(mk, win(0), gate(0), gate(1), ib, fb, nw_a, out(0), c_ref, n_ref, m_ref), MLSTM_PACE),
        (_hgrn_steps(mk, win(2), lbl, nw_c, out(2), st_ref, layer=layer), 1.0),
        (_mamba_steps(mk, win(1), gate(2), cw_b, cb_b, dtb_b, alog_b, dskip_b, nw_b, out(1), sb_ref, prevb_ref),
         MAMBA_PACE),
    ]


def _run_round_robin(steps):
    rnd = 0
    while steps:
        due = [(g, p) for g, p in steps if int((rnd + 1) * p) > int(rnd * p)]
        done = [g for g, _ in due if next(g, _DONE) is _DONE]
        steps = [(g, p) for g, p in steps if g not in done]
        rnd += 1


def _mixers_body(*refs, layer):
    um_ref, ug_ref = refs[0:2]
    params = refs[2:2 + N_MIXER_PARAMS]
    y_ref = refs[2 + N_MIXER_PARAMS]
    states = refs[3 + N_MIXER_PARAMS:]

    @pl.when(pl.program_id(1) == 0)
    def _():
        for r in states:
            r[...] = jnp.zeros_like(r)

    _run_round_robin(_mixer_steps(um_ref, ug_ref, params, y_ref, states, layer))


def _mixer_param_specs(params, layer):
    return [pl.BlockSpec((None,) + p.shape[1:], lambda *_: (layer, 0, 0)) if per_layer
            else pl.BlockSpec(p.shape, lambda *_, nd=p.ndim: (0,) * nd) for p, per_layer in params]


def _mixer_state_shapes():
    state = pltpu.VMEM((GROUP_WIDTH, GROUP_WIDTH), F32)
    conv_tail = pltpu.VMEM((8, 3 * GROUP_WIDTH), F32)
    return [state, pltpu.VMEM((8, GROUP_WIDTH), F32), pltpu.VMEM((8, LANES), F32),
            state, conv_tail,
            state,
            pltpu.VMEM((HEAD_DIM, GROUP_WIDTH), F32), conv_tail]


def _mixers_call(u_main, u_gate, params, layer, seq_len):
    n = u_main.shape[0]
    tb = min(MIX_TOKENS, seq_len)
    tiles = seq_len // tb
    blk = lambda w: pl.BlockSpec((tb, w), lambda bi, i: (bi * tiles + i, 0))
    return pl.pallas_call(
        functools.partial(_mixers_body, layer=layer),
        grid=(n // seq_len, tiles),
        in_specs=[blk(u_main.shape[1]), blk(u_gate.shape[1])] + _mixer_param_specs(params, layer),
        out_specs=blk(N_MIXERS * GROUP_WIDTH),
        out_shape=jax.ShapeDtypeStruct((n, N_MIXERS * GROUP_WIDTH), F32),
        scratch_shapes=_mixer_state_shapes(),
        compiler_params=pltpu.CompilerParams(dimension_semantics=("arbitrary", "arbitrary"),
                                             vmem_limit_bytes=VMEM_LIMIT),
        name="mixers",
    )(u_main, u_gate, *[p for p, _ in params])


def _rows(v):
    return v.astype(F32)[:, None, :]


def _head_rows(v):
    return jnp.pad(_rows(v), ((0, 0), (0, 0), (0, LANES - v.shape[1])))


def _mixer_params(mlstm_i_bias, mlstm_f_bias, mlstm_norm, mamba_conv_w, mamba_conv_b, mamba_dt_bias, mamba_a_log,
                  mamba_d, mamba_norm, hgrn_lb_logits, hgrn_norm, gdn_conv_w, gdn_a_log, gdn_dt_bias, gdn_norm):
    per_layer = [
        _head_rows(mlstm_i_bias), _head_rows(mlstm_f_bias), _rows(mlstm_norm),
        mamba_conv_w.astype(F32), _rows(mamba_conv_b), _head_rows(mamba_dt_bias), _head_rows(mamba_a_log),
        _rows(jnp.repeat(mamba_d, HEAD_DIM, axis=1)), _rows(mamba_norm)]
    hgrn = [(hgrn_lb_logits.astype(F32), False), (_rows(hgrn_norm), True)]
    gdn = [gdn_conv_w.astype(F32), _head_rows(gdn_a_log), _head_rows(gdn_dt_bias), _rows(gdn_norm)]
    return [(p, True) for p in per_layer] + hgrn + [(p, True) for p in gdn]


_MLSTM0, _MAMBA0, _HGRN0, _GDN0 = 0, 1032, 2060, 3084
_MAIN_COLS = ((_MLSTM0, _MLSTM0 + 1024), (_MAMBA0, _MAMBA0 + 1024), (_HGRN0, _HGRN0 + 1024),
              (_GDN0, _GDN0 + 1024))
_GATE_COLS = (_MLSTM0 + 1024, _MLSTM0 + 1028, _MAMBA0 + 1024, _GDN0 + 1024, _GDN0 + 1028)
_MAIN_STARTS = tuple(a for a, _ in _MAIN_COLS)
_D_IN_PAD = 33 * LANES


def kernel(x, ffn1_norm, ffn1_w_gate, ffn1_w_up, ffn1_w_down, mix_norm, w_in, w_out, mlstm_i_bias, mlstm_f_bias, mlstm_norm, mamba_conv_w, mamba_conv_b, mamba_dt_bias, mamba_a_log, mamba_d, mamba_norm, hgrn_lb_logits, hgrn_norm, gdn_conv_w, gdn_a_log, gdn_dt_bias, gdn_norm, ffn2_norm, ffn2_w_gate, ffn2_w_up, ffn2_w_down, final_norm):
    b, s, d = x.shape
    depth = w_in.shape[0]
    x2d = x.reshape(b * s, d)
    ffn1 = (_rows(ffn1_norm), ffn1_w_gate, ffn1_w_up, ffn1_w_down)
    ffn2 = (_rows(ffn2_norm), ffn2_w_gate, ffn2_w_up, ffn2_w_down)
    w_pad = _to_bf16(w_in, _D_IN_PAD)
    w_o = _to_bf16(w_out)
    mix_nw = _rows(mix_norm)
    params = _mixer_params(mlstm_i_bias, mlstm_f_bias, mlstm_norm, mamba_conv_w, mamba_conv_b, mamba_dt_bias,
                           mamba_a_log, mamba_d, mamba_norm, hgrn_lb_logits, hgrn_norm, gdn_conv_w, gdn_a_log,
                           gdn_dt_bias, gdn_norm)
    for l in range(depth):
        x2d = _ffn_call(x2d, *ffn1, l)
        um, ug = _inproj_call(x2d, mix_nw, w_pad, l)
        y = _mixers_call(um, ug, params, l, s)
        x2d = _ffn_call(x2d, *ffn2, l, mix=(y, w_o),
                        final_w=final_norm.astype(F32).reshape(1, d) if l == depth - 1 else None)
    return x2d.reshape(b, s, d)
```

```python
import functools

import numpy as np
import jax
import jax.numpy as jnp
from jax import lax
from jax.experimental import pallas as pl
from jax.experimental.pallas import tpu as pltpu

F32 = jnp.float32
BF16 = jnp.bfloat16

D_MODEL = 1024
CHUNK = 64
N_HEADS = 4
HEAD_DIM = 64
GROUP_WIDTH = N_HEADS * HEAD_DIM
SSM_STATE = 128
CONV_K = 4
D_FF = 2816
EPS = 1e-6
NEG_BIG = -1e30
LOG2E = 1.4426950408889634
LANES = 128
N_GATES = 5

FFN_TOKENS = 512
CAST_STEPS = 2
W_PREP_STEPS = 8
MXU_DIM = 256
FF_SPLIT = 6 * MXU_DIM
N_MIXERS = 4
MLSTM_PACE = 0.7
MAMBA_PACE = 0.55
MIX_TOKENS = 256
VMEM_LIMIT = 56 * 1024 * 1024


def _iota(shape, dim):
    return lax.broadcasted_iota(jnp.int32, shape, dim)


def _mm(a, b):
    return jnp.dot(a.astype(BF16), b.astype(BF16), preferred_element_type=F32)


def _mm_nt(a, b):
    return lax.dot_general(a.astype(BF16), b.astype(BF16), (((1,), (1,)), ((), ())),
                           preferred_element_type=F32)


def _mm_tn(a, b):
    return lax.dot_general(a.astype(BF16), b.astype(BF16), (((0,), (0,)), ((), ())),
                           preferred_element_type=F32)


def _split2(a):
    hi = a.astype(BF16)
    return hi, (a - hi.astype(F32)).astype(BF16)


def _mm_hilo(a, b_bf16):
    hi, lo = _split2(a)
    return (jnp.dot(hi, b_bf16, preferred_element_type=F32)
            + jnp.dot(lo, b_bf16, preferred_element_type=F32))


def _sigmoid(x):
    return 0.5 * jnp.tanh(0.5 * x) + 0.5


def _silu(x):
    h = 0.5 * x
    return h * jnp.tanh(h) + h


def _softplus(x):
    return jnp.maximum(x, 0.0) + jnp.log1p(jnp.exp(-jnp.abs(x)))


class _Masks:
    def __init__(self):
        row = _iota((GROUP_WIDTH, 1), 0)
        lane = _iota((1, GROUP_WIDTH), 1)
        t = _iota((CHUNK, 1), 0)
        self.row, self.lane = row, lane
        self.lane_head = lane >> 6
        block_diag = (row >> 6) == self.lane_head
        self.ones_bd = block_diag.astype(BF16)
        self.bd_f32 = block_diag.astype(F32)
        s = lane & 63
        self.causal_f = (t >= s).astype(F32)
        self.strict_f = (t > s).astype(F32)
        self.diag = t == s


def _bd(x, mk):
    x = x.astype(BF16)
    return jnp.concatenate([x, x, x, x], axis=0) * mk.ones_bd


def _unbd(y, mk):
    ym = y * mk.bd_f32
    return ym[0:64] + ym[64:128] + ym[128:192] + ym[192:256]


def _decay(arg, mask_f, exp=jnp.exp):
    return exp(jnp.minimum(arg, 0.0)) * mask_f


def _expand4(g, mk):
    first = _iota((1, LANES), 1) < HEAD_DIM
    return jnp.concatenate([jnp.where(first, g[:, 0:1], g[:, 1:2]), jnp.where(first, g[:, 2:3], g[:, 3:4])], axis=1)


def _rowvec(ge, mk):
    return jnp.sum(jnp.where(mk.diag, ge, 0.0), axis=0, keepdims=True)


def _chunk_cumsum(x):
    rows = x.shape[0]
    r, c = _iota((rows, rows), 0), _iota((rows, rows), 1)
    tril = (((r >> 6) == (c >> 6)) & (r >= c)).astype(BF16)
    x1 = x.astype(BF16)
    r1 = x - x1.astype(F32)
    x2 = r1.astype(BF16)
    x3 = (r1 - x2.astype(F32)).astype(BF16)
    return (jnp.dot(tril, x1, preferred_element_type=F32) + jnp.dot(tril, x2, preferred_element_type=F32)
            + jnp.dot(tril, x3, preferred_element_type=F32))


def _chunk_cummax(x):
    t = _iota((x.shape[0], 1), 0) & 63
    k = 1
    while k < CHUNK:
        x = jnp.maximum(x, jnp.where(t >= k, pltpu.roll(x, k, 0), NEG_BIG))
        k *= 2
    return x


def _causal_conv(x, prev8, w):
    acc = x * w[CONV_K - 1:CONV_K, :]
    r8 = _iota((8, 1), 0)
    for j in range(1, CONV_K):
        xr = pltpu.roll(x, j, 0)
        top = jnp.where(r8 < j, pltpu.roll(prev8, j, 0), xr[0:8])
        xs = jnp.concatenate([top, xr[8:]], axis=0)
        acc = acc + xs * w[CONV_K - 1 - j:CONV_K - j, :]
    return acc


def _head_rmsnorm(o, mk):
    return o * lax.rsqrt(_mm(o * o, mk.ones_bd) * (1.0 / HEAD_DIM) + EPS)


def _chunks(n_rows):
    return [slice(c * CHUNK, (c + 1) * CHUNK) for c in range(n_rows // CHUNK)]


def _rms(x, w):
    return x * lax.rsqrt(jnp.mean(x * x, axis=-1, keepdims=True) + EPS) * w


def _ffn_body(*refs, with_mix, with_final):
    it = iter(refs)
    x_ref = next(it)
    if with_mix:
        y_ref, wo_ref = next(it), next(it)
    nw_ref, wg_ref, wu_ref, wd_ref = next(it), next(it), next(it), next(it)
    if with_final:
        fw_ref = next(it)
    o_ref = next(it)
    wg_s, wu_s, wd_s = next(it), next(it), next(it)
    step = pl.program_id(0)

    @pl.when(step < W_PREP_STEPS)
    def _():
        for src, dst in ((wg_ref, wg_s), (wu_ref, wu_s), (wd_ref, wd_s)):
            rows = src.shape[0]
            dst[pl.ds(pl.multiple_of(step * rows, rows), rows), :] = src[...].astype(BF16)

    @pl.when(step >= W_PREP_STEPS)
    def _():
        x = x_ref[...]
        if with_mix:
            x = x + jnp.dot(y_ref[...].astype(BF16), wo_ref[...], preferred_element_type=F32)
        xb = _rms(x, nw_ref[...]).astype(BF16)
        acc = jnp.zeros(x.shape, F32)
        for sl in (slice(0, FF_SPLIT), slice(FF_SPLIT, D_FF)):
            g = jnp.dot(xb, wg_s[:, sl], preferred_element_type=F32)
            u = jnp.dot(xb, wu_s[:, sl], preferred_element_type=F32)
            h = (_silu(g) * u).astype(BF16)
            acc = acc + jnp.dot(h, wd_s[sl, :], preferred_element_type=F32)
        out = x + 0.5 * acc
        if with_final:
            out = _rms(out, fw_ref[...])
        o_ref[...] = out


def _const_spec(shape):
    return pl.BlockSpec(shape, lambda *_: (0,) * len(shape), pipeline_mode=pl.Buffered(1))


def _layer_spec(a, layer):
    return pl.BlockSpec((None,) + a.shape[1:], lambda *_: (layer, 0, 0), pipeline_mode=pl.Buffered(1))


def _cast_body(w_ref, o_ref):
    cols = w_ref.shape[1]
    if o_ref.shape[1] != cols:
        o_ref[...] = jnp.zeros(o_ref.shape, BF16)
    o_ref[:, 0:cols] = w_ref[...].astype(BF16)


def _to_bf16(w, cols_out=None):
    depth, rows, cols = w.shape
    cols_out = cols_out or cols
    tr = rows // CAST_STEPS
    return pl.pallas_call(
        _cast_body,
        grid=(depth, CAST_STEPS),
        in_specs=[pl.BlockSpec((None, tr, cols), lambda l, i: (l, i, 0))],
        out_specs=pl.BlockSpec((None, tr, cols_out), lambda l, i: (l, i, 0)),
        out_shape=jax.ShapeDtypeStruct((depth, rows, cols_out), BF16),
        compiler_params=pltpu.CompilerParams(dimension_semantics=("arbitrary", "arbitrary"),
                                             vmem_limit_bytes=VMEM_LIMIT),
        name="cast",
    )(w)


def _ffn_call(x2d, nw, wg, wu, wd, layer, mix=None, final_w=None):
    n = x2d.shape[0]
    tm = min(FFN_TOKENS, n)
    tok = lambda w: pl.BlockSpec((tm, w), lambda i: (jnp.maximum(i - W_PREP_STEPS, 0), 0))
    w_rows = lambda w: pl.BlockSpec((None, w.shape[1] // W_PREP_STEPS, w.shape[2]),
                                    lambda i: (layer, jnp.minimum(i, W_PREP_STEPS - 1), 0))
    args, specs = [x2d], [tok(D_MODEL)]
    if mix is not None:
        y, wo = mix
        args += [y, wo]
        specs += [tok(y.shape[1]), _layer_spec(wo, layer)]
    args += [nw, wg, wu, wd]
    specs += [_layer_spec(nw, layer), w_rows(wg), w_rows(wu), w_rows(wd)]
    if final_w is not None:
        args.append(final_w)
        specs.append(_const_spec(final_w.shape))
    return pl.pallas_call(
        functools.partial(_ffn_body, with_mix=mix is not None, with_final=final_w is not None),
        grid=(W_PREP_STEPS + n // tm,),
        in_specs=specs,
        out_specs=tok(D_MODEL),
        out_shape=jax.ShapeDtypeStruct((n, D_MODEL), F32),
        scratch_shapes=[pltpu.VMEM(w.shape[1:], BF16) for w in (wg, wu, wd)],
        compiler_params=pltpu.CompilerParams(dimension_semantics=("arbitrary",),
                                             vmem_limit_bytes=VMEM_LIMIT),
        name="ffn",
    )(*args)


def _inproj_body(x_ref, nw_ref, w_ref, um_ref, ug_ref):
    xb = _rms(x_ref[...], nw_ref[...]).astype(BF16)
    u = jnp.dot(xb, w_ref[...], preferred_element_type=F32)
    for j, start in enumerate(_MAIN_STARTS):
        lo = (start // LANES) * LANES
        width = 4 * GROUP_WIDTH
        if start == lo:
            blk = u[:, lo:lo + width]
        else:
            win = u[:, lo:lo + width + LANES]
            blk = pltpu.roll(win, width + LANES - (start - lo), 1)[:, 0:width]
        um_ref[:, j * width:(j + 1) * width] = blk
    head_lane = _iota((1, LANES), 1) < N_HEADS
    for j, start in enumerate(_GATE_COLS):
        lo = (start // LANES) * LANES
        tile = u[:, lo:lo + LANES]
        if start != lo:
            tile = pltpu.roll(tile, LANES - (start - lo), 1)
        ug_ref[:, j * LANES:(j + 1) * LANES] = jnp.where(head_lane, tile, 0.0)


def _inproj_call(x2d, nw, w_pad, layer):
    n = x2d.shape[0]
    tm = min(FFN_TOKENS, n)
    wm, wg = 4 * 4 * GROUP_WIDTH, N_GATES * LANES
    return pl.pallas_call(
        _inproj_body,
        grid=(n // tm,),
        in_specs=[pl.BlockSpec((tm, D_MODEL), lambda i: (i, 0)), _layer_spec(nw, layer), _layer_spec(w_pad, layer)],
        out_specs=[pl.BlockSpec((tm, wm), lambda i: (i, 0)), pl.BlockSpec((tm, wg), lambda i: (i, 0))],
        out_shape=[jax.ShapeDtypeStruct((n, wm), F32), jax.ShapeDtypeStruct((n, wg), F32)],
        compiler_params=pltpu.CompilerParams(dimension_semantics=("arbitrary",),
                                             vmem_limit_bytes=VMEM_LIMIT),
        name="inproj",
    )(x2d, nw, w_pad)


def _mlstm_steps(mk, u_ref, gi_ref, gf_ref, ib_ref, fb_ref, nw_ref, y_ref, c_ref, n_ref, m_ref):
    chunks = _chunks(u_ref.shape[0])
    li = gi_ref[...] + ib_ref[...]
    xf = gf_ref[...] + fb_ref[...]
    lf = jnp.minimum(xf, 0.0) - jnp.log1p(jnp.exp(-jnp.abs(xf)))
    bcum = _chunk_cumsum(lf)
    a = li - bcum
    yield
    cmax = _chunk_cummax(a)
    yield

    m_run = m_ref[0:1, :]
    m_start, m_tot = [], []
    for sl in chunks:
        last = slice(sl.stop - 1, sl.stop)
        m_start.append(m_run)
        m_tot.append(jnp.maximum(m_run, cmax[last]))
        m_run = bcum[last] + m_tot[-1]
    m_ref[0:1, :] = m_run
    yield

    qs = [u_ref[sl, 0:256] * (HEAD_DIM ** -0.5) for sl in chunks]
    ks = [u_ref[sl, 256:512] for sl in chunks]
    vs = [u_ref[sl, 512:768] for sl in chunks]
    mx = [jnp.maximum(m0, cmax[sl]) for m0, sl in zip(m_start, chunks)]
    arow = [_rowvec(_expand4(a[sl], mk), mk) for sl in chunks]
    yield
    dmat = [_decay(ar - _expand4(m, mk), mk.causal_f) for ar, m in zip(arow, mx)]
    yield
    p = [_mm_nt(q, _bd(k, mk)) * d for q, k, d in zip(qs, ks, dmat)]
    yield
    num_i = [_mm(pp, _bd(v, mk)) for pp, v in zip(p, vs)]
    yield
    den_i = [_mm_hilo(pp, mk.ones_bd) for pp in p]
    yield
    w_int = [_expand4(jnp.exp(m0 - m), mk) for m0, m in zip(m_start, mx)]
    bound = [_expand4(jnp.exp(-(bcum[sl] + m)), mk) for sl, m in zip(chunks, mx)]
    yield
    kw = [k * _expand4(jnp.exp(a[sl] - mt), mk) for k, sl, mt in zip(ks, chunks, m_tot)]
    decay = [_expand4(jnp.exp(m0 - mt), mk) for m0, mt in zip(m_start, m_tot)]
    yield
    d_c = [_mm_tn(kk, v) * mk.bd_f32 for kk, v in zip(kw, vs)]
    d_n = [jnp.sum(kk, axis=0, keepdims=True) for kk in kw]
    yield

    c_bd, n_row = c_ref[...], n_ref[0:1, :]
    c_at, n_at = [], []
    for i in range(len(chunks)):
        c_at.append(c_bd.astype(BF16))
        n_at.append(n_row)
        c_bd = decay[i] * c_bd + d_c[i]
        n_row = decay[i] * n_row + d_n[i]
    c_ref[...] = c_bd
    n_ref[0:1, :] = n_row
    yield
    q_c = [_mm(q, c) for q, c in zip(qs, c_at)]
    yield
    q_n = [_mm_hilo(q * n, mk.ones_bd) for q, n in zip(qs, n_at)]
    yield
    h = [(ni + w * qc) / jnp.maximum(jnp.abs(di + w * qn), b)
         for ni, di, w, qc, qn, b in zip(num_i, den_i, w_int, q_c, q_n, bound)]
    yield
    hn = [_head_rmsnorm(x, mk) for x in h]
    yield
    nw = nw_ref[...]
    for sl, x in zip(chunks, hn):
        y_ref[sl, :] = (x * nw * _sigmoid(u_ref[sl, 768:1024])).astype(y_ref.dtype)
    yield


def _mamba_steps(mk, u_ref, gdt_ref, cw_ref, cb_ref, dtb_ref, alog_ref, dskip_ref, nw_ref, y_ref, s_ref, prev_ref):
    rows = u_ref.shape[0]
    chunks = _chunks(rows)
    raw = u_ref[:, 256:1024]
    xbc = _silu(_causal_conv(raw, prev_ref[...], cw_ref[...]) + cb_ref[...])
    prev_ref[...] = raw[rows - 8:rows, :]
    yield
    dt = _softplus(gdt_ref[...] + dtb_ref[...])
    cum = _chunk_cumsum(dt * (-jnp.exp(alog_ref[...])))
    grp_b = (((mk.row >> 6) >> 1) == (mk.lane >> 7)).astype(BF16)
    grp_s = ((mk.row >> 7) == (mk.lane_head >> 1)).astype(F32)
    yield

    xs = [xbc[sl, 0:256] for sl in chunks]
    bm = [xbc[sl, 256:512] for sl in chunks]
    cm = [xbc[sl, 512:768] for sl in chunks]
    cum_e = [_expand4(cum[sl], mk) for sl in chunks]
    yield
    seg = [_decay(ce - _rowvec(ce, mk), mk.causal_f) for ce in cum_e]
    yield
    dtrow = [_rowvec(_expand4(dt[sl], mk), mk) for sl in chunks]
    yield
    p = [_mm_nt(c, jnp.concatenate([b.astype(BF16)] * 4, axis=0) * grp_b) * sg * dr
         for c, b, sg, dr in zip(cm, bm, seg, dtrow)]
    yield
    y_i = [_mm(pp, _bd(x, mk)) + x * dskip_ref[...] for pp, x in zip(p, xs)]
    yield
    last = [slice(sl.stop - 1, sl.stop) for sl in chunks]
    xw = [x * _expand4(jnp.exp(cum[ls] - cum[sl]) * dt[sl], mk) for x, sl, ls in zip(xs, chunks, last)]
    yield
    d_s = [_mm_tn(b, w) * grp_s for b, w in zip(bm, xw)]
    yield
    decay = [_expand4(jnp.exp(cum[ls]), mk) for ls in last]
    e_cum = [jnp.exp(ce) for ce in cum_e]
    yield

    s_mat = s_ref[...]
    s_at = []
    for i in range(len(chunks)):
        s_at.append(s_mat.astype(BF16))
        s_mat = decay[i] * s_mat + d_s[i]
    s_ref[...] = s_mat
    yield
    y = [yi + e * _mm(c, s) for yi, e, c, s in zip(y_i, e_cum, cm, s_at)]
    yield
    nw = nw_ref[...]
    for sl, yc in zip(chunks, y):
        yz = yc * _silu(u_ref[sl, 0:256])
        halves = []
        for g in range(2):
            yg = yz[:, g * 128:(g + 1) * 128]
            halves.append(yg * lax.rsqrt(jnp.mean(yg * yg, axis=-1, keepdims=True) + EPS))
        y_ref[sl, :] = (jnp.concatenate(halves, axis=-1) * nw).astype(y_ref.dtype)
    yield


HG_SUB = 16


def _stack_rows(x, mk):
    n = x.shape[0]
    r = _iota((4 * n, 1), 0)
    blk = (r >= n).astype(jnp.int32) + (r >= 2 * n).astype(jnp.int32) + (r >= 3 * n).astype(jnp.int32)
    x = x.astype(BF16)
    return jnp.where(blk == mk.lane_head, jnp.concatenate([x, x, x, x], axis=0), jnp.zeros((), BF16))


def _hgrn_intra_sub(qq, kk, gc, i_in, a, mk):
    half = HG_SUB // 2
    t_half = _iota((half, 1), 0)
    lo = a * HG_SUB
    q_a, k_a, g_a, i_a = qq[lo:lo + HG_SUB], kk[lo:lo + HG_SUB], gc[lo:lo + HG_SUB], i_in[lo:lo + HG_SUB]
    q_t, q_b, g_t, g_b = q_a[0:half], q_a[half:], g_a[0:half], g_a[half:]
    tops, bots = [], []
    for s in range(HG_SUB):
        k_s, g_s = k_a[s:s + 1, :], g_a[s:s + 1, :]
        if s < half:
            tops.append(q_t * k_s * _decay(g_t - g_s, (t_half >= s).astype(F32), jnp.exp2))
            bots.append(q_b * k_s * jnp.exp2(g_b - g_s))
        else:
            bots.append(q_b * k_s * _decay(g_b - g_s, (t_half >= s - half).astype(F32), jnp.exp2))
    z = _mm(jnp.concatenate(tops + bots, axis=0), mk.ones_bd)
    z_t, z_b = z[0:half * half], z[half * half:]
    o_t = z_t[0:half] * i_a[0:1, :]
    for s in range(1, half):
        o_t = o_t + z_t[s * half:(s + 1) * half] * i_a[s:s + 1, :]
    o_b = z_b[0:half] * i_a[0:1, :]
    for s in range(1, HG_SUB):
        o_b = o_b + z_b[s * half:(s + 1) * half] * i_a[s:s + 1, :]
    o = jnp.concatenate([o_t, o_b], axis=0)
    if a > 0:
        r = gc[lo - 1:lo, :]
        sc = _mm_nt(q_a * jnp.exp2(g_a - r), _stack_rows(kk[0:lo] * jnp.exp2(r - gc[0:lo]), mk))
        o = o + _mm(sc, _stack_rows(i_in[0:lo], mk))
    return o


def _hgrn_steps(mk, u_ref, lbl_ref, nw_ref, y_ref, st_ref, *, layer):
    logits = lbl_ref[...]
    e = jnp.exp(logits - jnp.max(logits, axis=0, keepdims=True))
    prob = e / jnp.sum(e, axis=0, keepdims=True)
    lb = jnp.sum(prob[0:layer + 1], axis=0, keepdims=True) - prob[0:1]

    chunks = _chunks(u_ref.shape[0])
    f = lb + (1.0 - lb) * _sigmoid(u_ref[:, 256:512])
    kk_all = 1.0 - f
    qq_all = _silu(u_ref[:, 0:256])
    yield
    gc_all = _chunk_cumsum(jnp.log(f)) * LOG2E
    yield

    qq = [qq_all[sl] for sl in chunks]
    kk = [kk_all[sl] for sl in chunks]
    gc = [gc_all[sl] for sl in chunks]
    ii = [u_ref[sl, 512:768] for sl in chunks]
    subs = [[] for _ in chunks]
    for c in range(len(chunks)):
        for a in range(CHUNK // HG_SUB):
            subs[c].append(_hgrn_intra_sub(qq[c], kk[c], gc[c], ii[c], a, mk))
            yield
    o_i = [jnp.concatenate(s, axis=0) for s in subs]
    qe = [q * jnp.exp2(g) for q, g in zip(qq, gc)]
    g_last = [g[CHUNK - 1:CHUNK, :] for g in gc]
    yield
    d_s = [_mm_tn(i, k * jnp.exp2(gl - g)) * mk.bd_f32
           for i, k, g, gl in zip(ii, kk, gc, g_last)]
    decay = [jnp.exp2(gl) for gl in g_last]
    yield

    st = st_ref[...]
    st_at = []
    for i in range(len(chunks)):
        st_at.append(st.astype(BF16))
        st = decay[i] * st + d_s[i]
    st_ref[...] = st
    yield
    o = [oi + _mm_nt(q, s) for oi, q, s in zip(o_i, qe, st_at)]
    yield
    on = [_head_rmsnorm(x, mk) for x in o]
    yield
    nw = nw_ref[...]
    for sl, x in zip(chunks, on):
        y_ref[sl, :] = (x * nw * _silu(u_ref[sl, 768:1024])).astype(y_ref.dtype)
    yield


def _gdn_steps(mk, u_ref, gb_ref, ga_ref, cw_ref, alog_ref, dtb_ref, nw_ref, y_ref, s_ref, prev_ref):
    rows = u_ref.shape[0]
    chunks = _chunks(rows)
    raw = u_ref[:, 0:768]
    qkv = _silu(_causal_conv(raw, prev_ref[...], cw_ref[...]))
    prev_ref[...] = raw[rows - 8:rows, :]
    yield
    q, k, v = qkv[:, 0:256], qkv[:, 256:512], qkv[:, 512:768]
    qn = q * lax.rsqrt(_mm(q * q, mk.ones_bd) + EPS) * (HEAD_DIM ** -0.5)
    kn = k * lax.rsqrt(_mm(k * k, mk.ones_bd) + EPS)
    yield
    gcum = _chunk_cumsum(-jnp.exp(alog_ref[...]) * _softplus(ga_ref[...] + dtb_ref[...]))
    be = _expand4(_sigmoid(gb_ref[...]), mk)
    ge = _expand4(gcum, mk)
    yield
    e_g = jnp.exp(ge)
    kb = kn * be
    vb = v * be
    kbg = kb * e_g
    q_dec = qn * e_g
    eye = jnp.where(mk.diag, 1.0, 0.0)
    yield

    e_dec = [jnp.exp(jnp.minimum(ge[sl] - _rowvec(ge[sl], mk), 0.0)) for sl in chunks]
    yield
    qk = [_mm_nt(jnp.concatenate([kb[sl], qn[sl]], axis=0), _bd(kn[sl], mk)) for sl in chunks]
    yield
    a_p = [x[0:CHUNK] * (e * mk.strict_f) for x, e in zip(qk, e_dec)]
    attn = [x[CHUNK:2 * CHUNK] * (e * mk.causal_f) for x, e in zip(qk, e_dec)]

    pw = [_mm(-x, _bd(-x, mk)) for x in a_p]
    t0 = [eye - x for x in a_p]
    yield
    for j in range(1, 6):
        r = [_mm(jnp.concatenate([t, m], axis=0), _bd(m, mk)) for t, m in zip(t0, pw)]
        t0 = [t + x[0:CHUNK] for t, x in zip(t0, r)]
        pw = [x[CHUNK:2 * CHUNK] for x in r]
        yield
    a_t0 = []
    for x, t in zip(a_p, t0):
        a_hi, a_lo = _split2(x)
        t_hi, t_lo = _split2(t)
        r = jnp.dot(jnp.concatenate([a_hi, a_lo], axis=0), _bd(t_hi, mk).astype(BF16), preferred_element_type=F32)
        a_t0.append(r[0:CHUNK] + r[CHUNK:2 * CHUNK]
                    + jnp.dot(a_hi, _bd(t_lo, mk).astype(BF16), preferred_element_type=F32))
    yield
    t_mat = [t + _mm(t, _bd(eye - t - at, mk)) for t, at in zip(t0, a_t0)]
    yield

    uw = [_mm(t, jnp.concatenate([_bd(vb[sl], mk), _bd(kbg[sl], mk)], axis=1)) for t, sl in zip(t_mat, chunks)]
    u = [x[:, 0:256] for x in uw]
    w = [x[:, 256:512] for x in uw]
    yield
    au = [_mm(at, jnp.concatenate([_bd(uu, mk), _bd(ww, mk)], axis=1)) for at, uu, ww in zip(attn, u, w)]
    o_i = [x[:, 0:256] for x in au]
    q2 = [q_dec[sl] - x[:, 256:512] for sl, x in zip(chunks, au)]
    yield
    last = [slice(sl.stop - 1, sl.stop) for sl in chunks]
    k_dec = [kn[sl] * _expand4(jnp.exp(gcum[ls] - gcum[sl]), mk) for sl, ls in zip(chunks, last)]
    fg = [_mm_tn(kd, jnp.concatenate([ww, uu], axis=1)) for kd, ww, uu in zip(k_dec, w, u)]
    yield
    f_p = [-_unbd(x[:, 0:256], mk) for x in fg]
    g_p = [_unbd(x[:, 256:512], mk) for x in fg]
    decay = [_expand4(jnp.exp(gcum[ls]), mk) for ls in last]
    yield

    s_p = s_ref[...]
    s_at = []
    for i in range(len(chunks)):
        s_bd = _bd(s_p, mk)
        s_at.append(s_bd)
        s_p = decay[i] * s_p + _mm(f_p[i], s_bd) + g_p[i]
        yield
    s_ref[...] = s_p
    o = [oi + _mm(q, s) for oi, q, s in zip(o_i, q2, s_at)]
    yield
    on = [_head_rmsnorm(x, mk) for x in o]
    yield
    nw = nw_ref[...]
    for sl, x in zip(chunks, on):
        y_ref[sl, :] = (x * nw * _silu(u_ref[sl, 768:1024])).astype(y_ref.dtype)
    yield


_DONE = object()


N_MIXER_PARAMS = 15


def _mixer_steps(um_ref, ug_ref, params, y_ref, states, layer):
    ib, fb, nw_a, cw_b, cb_b, dtb_b, alog_b, dskip_b, nw_b, lbl, nw_c, cw_d, alog_d, dtb_d, nw_d = params
    c_ref, n_ref, m_ref, sb_ref, prevb_ref, st_ref, sd_ref, prevd_ref = states
    mk = _Masks()
    win = lambda j: um_ref.at[:, pl.ds(j * 4 * GROUP_WIDTH, 4 * GROUP_WIDTH)]
    gate = lambda j: ug_ref.at[:, pl.ds(j * LANES, LANES)]
    out = lambda j: y_ref.at[:, pl.ds(j * GROUP_WIDTH, GROUP_WIDTH)]
    return [
        (_gdn_steps(mk, win(3), gate(3), gate(4), cw_d, alog_d, dtb_d, nw_d, out(3), sd_ref, prevd_ref), 1.0),
        (_mlstm_steps(mk, win(0), gate(0), gate(1), ib, fb, nw_a, out(0), c_ref, n_ref, m_ref), MLSTM_PACE),
        (_hgrn_steps(mk, win(2), lbl, nw_c, out(2), st_ref, layer=layer), 1.0),
        (_mamba_steps(mk, win(1), gate(2), cw_b, cb_b, dtb_b, alog_b, dskip_b, nw_b, out(1), sb_ref, prevb_ref),
         MAMBA_PACE),
    ]


def _run_round_robin(steps):
    rnd = 0
    while steps:
        due = [(g, p) for g, p in steps if int((rnd + 1) * p) > int(rnd * p)]
        done = [g for g, _ in due if next(g, _DONE) is _DONE]
        steps = [(g, p) for g, p in steps if g not in done]
        rnd += 1


def _mixers_body(*refs, layer):
    um_ref, ug_ref = refs[0:2]
    params = refs[2:2 + N_MIXER_PARAMS]
    y_ref = refs[2 + N_MIXER_PARAMS]
    states = refs[3 + N_MIXER_PARAMS:]

    @pl.when(pl.program_id(1) == 0)
    def _():
        for r in states:
            r[...] = jnp.zeros_like(r)

    _run_round_robin(_mixer_steps(um_ref, ug_ref, params, y_ref, states, layer))


def _mixer_param_specs(params, layer):
    return [pl.BlockSpec((None,) + p.shape[1:], lambda *_: (layer, 0, 0)) if per_layer
            else pl.BlockSpec(p.shape, lambda *_, nd=p.ndim: (0,) * nd) for p, per_layer in params]


def _mixer_state_shapes():
    state = pltpu.VMEM((GROUP_WIDTH, GROUP_WIDTH), F32)
    conv_tail = pltpu.VMEM((8, 3 * GROUP_WIDTH), F32)
    return [state, pltpu.VMEM((8, GROUP_WIDTH), F32), pltpu.VMEM((8, LANES), F32),
            state, conv_tail,
            state,
            pltpu.VMEM((HEAD_DIM, GROUP_WIDTH), F32), conv_tail]


def _mixers_call(u_main, u_gate, params, layer, seq_len):
    n = u_main.shape[0]
    tb = min(MIX_TOKENS, seq_len)
    tiles = seq_len // tb
    blk = lambda w: pl.BlockSpec((tb, w), lambda bi, i: (bi * tiles + i, 0))
    return pl.pallas_call(
        functools.partial(_mixers_body, layer=layer),
        grid=(n // seq_len, tiles),
        in_specs=[blk(u_main.shape[1]), blk(u_gate.shape[1])] + _mixer_param_specs(params, layer),
        out_specs=blk(N_MIXERS * GROUP_WIDTH),
        out_shape=jax.ShapeDtypeStruct((n, N_MIXERS * GROUP_WIDTH), BF16),
        scratch_shapes=_mixer_state_shapes(),
        compiler_params=pltpu.CompilerParams(dimension_semantics=("arbitrary", "arbitrary"),
                                             vmem_limit_bytes=VMEM_LIMIT),
        name="mixers",
    )(u_main, u_gate, *[p for p, _ in params])


def _rows(v):
    return v.astype(F32)[:, None, :]


def _head_rows(v):
    return jnp.pad(_rows(v), ((0, 0), (0, 0), (0, LANES - v.shape[1])))


def _mixer_params(mlstm_i_bias, mlstm_f_bias, mlstm_norm, mamba_conv_w, mamba_conv_b, mamba_dt_bias, mamba_a_log,
                  mamba_d, mamba_norm, hgrn_lb_logits, hgrn_norm, gdn_conv_w, gdn_a_log, gdn_dt_bias, gdn_norm):
    per_layer = [
        _head_rows(mlstm_i_bias), _head_rows(mlstm_f_bias), _rows(mlstm_norm),
        mamba_conv_w.astype(F32), _rows(mamba_conv_b), _head_rows(mamba_dt_bias), _head_rows(mamba_a_log),
        _rows(jnp.repeat(mamba_d, HEAD_DIM, axis=1)), _rows(mamba_norm)]
    hgrn = [(hgrn_lb_logits.astype(F32), False), (_rows(hgrn_norm), True)]
    gdn = [gdn_conv_w.astype(F32), _head_rows(gdn_a_log), _head_rows(gdn_dt_bias), _rows(gdn_norm)]
    return [(p, True) for p in per_layer] + hgrn + [(p, True) for p in gdn]


_MLSTM0, _MAMBA0, _HGRN0, _GDN0 = 0, 1032, 2060, 3084
_MAIN_COLS = ((_MLSTM0, _MLSTM0 + 1024), (_MAMBA0, _MAMBA0 + 1024), (_HGRN0, _HGRN0 + 1024),
              (_GDN0, _GDN0 + 1024))
_GATE_COLS = (_MLSTM0 + 1024, _MLSTM0 + 1028, _MAMBA0 + 1024, _GDN0 + 1024, _GDN0 + 1028)
_MAIN_STARTS = tuple(a for a, _ in _MAIN_COLS)
_D_IN_PAD = 33 * LANES


def kernel(x, ffn1_norm, ffn1_w_gate, ffn1_w_up, ffn1_w_down, mix_norm, w_in, w_out, mlstm_i_bias, mlstm_f_bias, mlstm_norm, mamba_conv_w, mamba_conv_b, mamba_dt_bias, mamba_a_log, mamba_d, mamba_norm, hgrn_lb_logits, hgrn_norm, gdn_conv_w, gdn_a_log, gdn_dt_bias, gdn_norm, ffn2_norm, ffn2_w_gate, ffn2_w_up, ffn2_w_down, final_norm):
    b, s, d = x.shape
    depth = w_in.shape[0]
    x2d = x.reshape(b * s, d)
    ffn1 = (_rows(ffn1_norm), ffn1_w_gate, ffn1_w_up, ffn1_w_down)
    ffn2 = (_rows(ffn2_norm), ffn2_w_gate, ffn2_w_up, ffn2_w_down)
    w_pad = _to_bf16(w_in, _D_IN_PAD)
    w_o = _to_bf16(w_out)
    mix_nw = _rows(mix_norm)
    params = _mixer_params(mlstm_i_bias, mlstm_f_bias, mlstm_norm, mamba_conv_w, mamba_conv_b, mamba_dt_bias,
                           mamba_a_log, mamba_d, mamba_norm, hgrn_lb_logits, hgrn_norm, gdn_conv_w, gdn_a_log,
                           gdn_dt_bias, gdn_norm)
    for l in range(depth):
        x2d = _ffn_call(x2d, *ffn1, l)
        um, ug = _inproj_call(x2d, mix_nw, w_pad, l)
        y = _mixers_call(um, ug, params, l, s)
        x2d = _ffn_call(x2d, *ffn2, l, mix=(y, w_o),
                        final_w=final_norm.astype(F32).reshape(1, d) if l == depth - 1 else None)
    return x2d.reshape(b, s, d)
```
